```python
import jax, jax.numpy as jnp
from jax import lax
import numpy as np


D_MODEL = 1024
BATCH = 8
SEQ = 4096
DEPTH = 4

CHUNK = 64
N_MIXERS = 4
EPS = 1e-6

RET_HEADS = 4
RET_DK = D_MODEL // RET_HEADS
RET_DV = 2 * D_MODEL // RET_HEADS
ROPE_BASE = 10000.0

ATT_HEADS = 16
ATT_DH = D_MODEL // ATT_HEADS
ATT_LEFT_CHUNKS = 8
ATT_LEFT = ATT_LEFT_CHUNKS * CHUNK
ATT_BAND = ATT_LEFT + CHUNK
ATT_MAX_REL = 256
ATT_N_REL = CHUNK + ATT_MAX_REL

GLA_HEADS = 4
GLA_DK = D_MODEL // 2 // GLA_HEADS
GLA_DV = D_MODEL // GLA_HEADS
GLA_GATE_RANK = 16
GLA_GATE_NORM = 16.0

MLSTM_HEADS = 4
MLSTM_INNER = 2 * D_MODEL
MLSTM_DH = MLSTM_INNER // MLSTM_HEADS
MLSTM_CONV = 4
MLSTM_QKV_BLOCK = 4
MLSTM_N_BLOCKS = MLSTM_INNER // MLSTM_QKV_BLOCK

N_EXPERTS = 32
TOP_K = 4
D_EXPERT = D_MODEL
SWIGLU_LIMIT = 7.0
SWIGLU_ALPHA = 1.702

kernel_name = 'hybrid_streaming_encoder_trunk'


def _layers_of(kind):
    return len(range(kind, DEPTH, N_MIXERS))


def rms_norm(x, g):
    xf = x.astype(jnp.float32)
    y = xf * lax.rsqrt(jnp.mean(xf * xf, axis=-1, keepdims=True) + EPS)
    return (y * g).astype(x.dtype)


def head_norm(x, g, center):
    xf = x.astype(jnp.float32)
    if center:
        xf = xf - jnp.mean(xf, axis=-1, keepdims=True)
    y = xf * lax.rsqrt(jnp.mean(xf * xf, axis=-1, keepdims=True) + EPS)
    return (y * g[None, :, None, :]).astype(x.dtype)


def to_heads(t, n_heads):
    b, s, _ = t.shape
    return t.reshape(b, s, n_heads, -1).transpose(0, 2, 1, 3)


def from_heads(t):
    b, h, s, d = t.shape
    return t.transpose(0, 2, 1, 3).reshape(b, s, h * d)


def rope(t, pos):
    half = t.shape[-1] // 2
    inv = ROPE_BASE ** (-jnp.arange(half, dtype=jnp.float32) / half)
    ang = pos.astype(jnp.float32)[:, None] * inv[None, :]
    cos = jnp.cos(ang)[None, :, None, :]
    sin = jnp.sin(ang)[None, :, None, :]
    t1, t2 = t[..., :half], t[..., half:]
    return jnp.concatenate([t1 * cos - t2 * sin, t1 * sin + t2 * cos], axis=-1).astype(t.dtype)


def chunk_linear_attention(q, k, v, log_a):
    b_, h_, s_, _ = q.shape
    dv = v.shape[-1]
    nc = s_ // CHUNK

    def chunks(t):
        return t.reshape(t.shape[0], t.shape[1], nc, CHUNK, t.shape[3])

    qc, kc, vc = chunks(q), chunks(k), chunks(v)
    b = jnp.cumsum(chunks(log_a.astype(jnp.float32)), axis=3)
    b_last = b[:, :, :, -1:, :]
    q_in = qc * jnp.exp(b)
    k_in = kc * jnp.exp(-b)
    k_st = kc * jnp.exp(b_last - b)
    causal = jnp.tril(jnp.ones((CHUNK, CHUNK), dtype=bool))
    scores = jnp.where(causal, jnp.einsum('bhcld,bhcmd->bhclm', q_in, k_in), 0.0)
    o_intra = jnp.einsum('bhclm,bhcme->bhcle', scores, vc)
    decay = jnp.exp(b_last[:, :, :, 0, :])

    def step(state, xs):
        q_c, k_c, v_c, d_c = xs
        o = jnp.einsum('bhld,bhde->bhle', q_c, state)
        state = d_c[..., None] * state + jnp.einsum('bhld,bhle->bhde', k_c, v_c)
        return state, o

    mv = lambda t: jnp.moveaxis(t, 2, 0)
    dt = jnp.result_type(q_in, k_st, vc)
    init = jnp.zeros((b_, h_, q.shape[-1], dv), dt)
    _, o_inter = lax.scan(step, init, (mv(q_in), mv(k_st), mv(vc), mv(decay)))
    o = o_intra + jnp.moveaxis(o_inter, 0, 2)
    return o.reshape(b_, h_, s_, dv)


def retention_mixer(h, w_in, gn_g, w_out):
    b_, s_, _ = h.shape
    dq = RET_HEADS * RET_DK
    dvv = RET_HEADS * RET_DV
    q, k, v, g = jnp.split(h @ w_in, [dq, 2 * dq, 2 * dq + dvv], axis=-1)
    pos = jnp.arange(s_)
    q = rope(q.reshape(b_, s_, RET_HEADS, RET_DK), pos)
    k = rope(k.reshape(b_, s_, RET_HEADS, RET_DK), pos) * RET_DK ** -0.5
    q, k = q.transpose(0, 2, 1, 3), k.transpose(0, 2, 1, 3)
    v = to_heads(v, RET_HEADS)
    log_gamma = jnp.log1p(-jnp.exp2(-5.0 - jnp.arange(RET_HEADS, dtype=jnp.float32)))
    log_a = jnp.broadcast_to(log_gamma[None, :, None, None], (1, RET_HEADS, s_, 1))
    o = head_norm(chunk_linear_attention(q, k, v, log_a), gn_g, center=True)
    return (from_heads(o) * jax.nn.silu(g)) @ w_out


def chunked_attention_mixer(h, w_in, rel_bias, w_out):
    b_, s_, d_ = h.shape
    nc = s_ // CHUNK
    q, k, v = jnp.split(h @ w_in, 3, axis=-1)
    q = q.reshape(b_, s_, ATT_HEADS, ATT_DH) * ATT_DH ** -0.5
    pad = ((0, 0), (ATT_LEFT, 0), (0, 0), (0, 0))
    k = jnp.pad(k.reshape(b_, s_, ATT_HEADS, ATT_DH), pad)
    v = jnp.pad(v.reshape(b_, s_, ATT_HEADS, ATT_DH), pad)
    qi = jnp.arange(CHUNK)[:, None]
    kj = jnp.arange(ATT_BAND)[None, :]
    dist = qi - kj + ATT_LEFT
    idx = jnp.clip(dist, -(CHUNK - 1), ATT_MAX_REL) + (CHUNK - 1)
    bias = rel_bias[:, idx].astype(jnp.float32)
    qc = jnp.moveaxis(q.reshape(b_, nc, CHUNK, ATT_HEADS, ATT_DH), 1, 0)

    def one_chunk(args):
        ci, q_c = args
        start = ci * CHUNK
        k_b = lax.dynamic_slice_in_dim(k, start, ATT_BAND, axis=1)
        v_b = lax.dynamic_slice_in_dim(v, start, ATT_BAND, axis=1)
        s = jnp.einsum('blhd,bmhd->bhlm', q_c, k_b).astype(jnp.float32) + bias
        valid = kj >= ATT_LEFT - start
        p = jax.nn.softmax(jnp.where(valid, s, -jnp.inf), axis=-1).astype(v.dtype)
        return jnp.einsum('bhlm,bmhd->blhd', p, v_b)

    o = lax.map(one_chunk, (jnp.arange(nc), qc))
    o = jnp.moveaxis(o, 0, 1).reshape(b_, s_, d_)
    return o @ w_out


def gla_mixer(h, w_in, w_gate, b_gate, gn_g, w_out):
    dq = GLA_HEADS * GLA_DK
    dvv = GLA_HEADS * GLA_DV
    q, k, v, g_low, r = jnp.split(h @ w_in, [dq, 2 * dq, 2 * dq + dvv, 2 * dq + dvv + GLA_GATE_RANK], axis=-1)
    log_a = jax.nn.log_sigmoid((g_low @ w_gate + b_gate).astype(jnp.float32)) / GLA_GATE_NORM
    q = to_heads(q, GLA_HEADS) * GLA_DK ** -0.5
    o = chunk_linear_attention(q, to_heads(k, GLA_HEADS), to_heads(v, GLA_HEADS), to_heads(log_a, GLA_HEADS))
    o = head_norm(o, gn_g, center=False)
    return (from_heads(o) * jax.nn.silu(r)) @ w_out


def causal_depthwise_conv(x, w, b):
    ch = x.shape[-1]
    y = lax.conv_general_dilated(x, w[:, None, :].astype(x.dtype), window_strides=(1,),
                                 padding=[(MLSTM_CONV - 1, 0)],
                                 dimension_numbers=('NWC', 'WIO', 'NWC'),
                                 feature_group_count=ch)
    return y + b


def mlstm_chunkwise(q, k, v, ig, log_f):
    b_, h_, s_, dh = q.shape
    nc = s_ // CHUNK
    f32 = jnp.float32

    def chunks(t):
        return jnp.moveaxis(t.astype(f32).reshape((b_, h_, nc, CHUNK) + t.shape[3:]), 2, 0)

    causal = jnp.tril(jnp.ones((CHUNK, CHUNK), dtype=bool))

    def step(carry, xs):
        cmat, nvec, m = carry
        q_c, k_c, v_c, i_c, f_c = xs
        b = jnp.cumsum(f_c, axis=-1)
        log_intra = jnp.where(causal, b[..., :, None] - b[..., None, :] + i_c[..., None, :], -jnp.inf)
        log_inter = b + m[..., None]
        m_row = jnp.maximum(log_inter, jnp.max(log_intra, axis=-1))
        w_intra = jnp.exp(log_intra - m_row[..., None])
        w_inter = jnp.exp(log_inter - m_row)
        s = jnp.einsum('bhld,bhmd->bhlm', q_c, k_c) * w_intra
        num = w_inter[..., None] * jnp.einsum('bhld,bhde->bhle', q_c, cmat) + jnp.einsum('bhlm,bhme->bhle', s, v_c)
        den = w_inter * jnp.einsum('bhld,bhd->bhl', q_c, nvec) + jnp.sum(s, axis=-1)
        h_c = num / jnp.maximum(jnp.abs(den), jnp.exp(-m_row))[..., None]
        b_last = b[..., -1]
        log_keep = b_last + m
        log_w = b_last[..., None] - b + i_c
        m_new = jnp.maximum(log_keep, jnp.max(log_w, axis=-1))
        keep = jnp.exp(log_keep - m_new)
        w = jnp.exp(log_w - m_new[..., None])
        cmat = keep[..., None, None] * cmat + jnp.einsum('bhl,bhld,bhle->bhde', w, k_c, v_c)
        nvec = keep[..., None] * nvec + jnp.einsum('bhl,bhld->bhd', w, k_c)
        return (cmat, nvec, m_new), h_c

    init = (jnp.zeros((b_, h_, dh, dh), f32), jnp.zeros((b_, h_, dh), f32), jnp.zeros((b_, h_), f32))
    _, hs = lax.scan(step, init, (chunks(q), chunks(k), chunks(v), chunks(ig), chunks(log_f)))
    return jnp.moveaxis(hs, 0, 2).reshape(b_, h_, s_, dh)


def mlstm_mixer(h, w_in, conv_w, conv_b, w_q, w_k, w_v, w_gates, b_gates, gn_g, skip, w_out):
    b_, s_, _ = h.shape
    xm, z = jnp.split(h @ w_in, 2, axis=-1)
    xc = jax.nn.silu(causal_depthwise_conv(xm, conv_w, conv_b))

    def blockdiag(t, w):
        t = t.reshape(b_, s_, MLSTM_N_BLOCKS, MLSTM_QKV_BLOCK)
        return jnp.einsum('bsnc,ncd->bsnd', t, w).reshape(b_, s_, MLSTM_INNER)

    q = blockdiag(xc, w_q)
    k = blockdiag(xc, w_k) * MLSTM_DH ** -0.5
    v = blockdiag(xm, w_v)
    gates = (jnp.concatenate([q, k, v], axis=-1) @ w_gates + b_gates).astype(jnp.float32)
    ig, fg = jnp.split(gates.transpose(0, 2, 1), 2, axis=1)
    hh = mlstm_chunkwise(to_heads(q, MLSTM_HEADS), to_heads(k, MLSTM_HEADS), to_heads(v, MLSTM_HEADS),
                         ig, jax.nn.log_sigmoid(fg))
    hh = head_norm(hh.astype(h.dtype), gn_g, center=True)
    out = (from_heads(hh) + skip * xc) * jax.nn.silu(z)
    return out @ w_out


def moe_ffn(h, router_w, router_b, w1, b1, w2, b2):
    b_, s_, d_ = h.shape
    t = h.reshape(b_ * s_, d_)
    logits = (t @ router_w + router_b).astype(jnp.float32)
    top_v, top_i = lax.top_k(logits, TOP_K)
    top_w = jax.nn.softmax(top_v, axis=-1)
    combine = jnp.sum(jax.nn.one_hot(top_i, N_EXPERTS, dtype=jnp.float32) * top_w[..., None], axis=1)
    out = jnp.zeros((b_ * s_, d_), jnp.float32)
    for e in range(N_EXPERTS):
        hid = t @ w1[e] + b1[e]
        gate = jnp.minimum(hid[:, :D_EXPERT], SWIGLU_LIMIT)
        lin = jnp.clip(hid[:, D_EXPERT:], -SWIGLU_LIMIT, SWIGLU_LIMIT)
        act = gate * jax.nn.sigmoid(SWIGLU_ALPHA * gate) * (lin + 1.0)
        out = out + combine[:, e:e + 1] * (act @ w2[e] + b2[e])
    return out.reshape(b_, s_, d_).astype(h.dtype)


def setup_inputs(seed: int = 0) -> dict:
    key = jax.random.key(seed)
    keys = jax.random.split(key, 64)
    counter = iter(range(64))

    def nxt():
        return keys[next(counter)]

    def dense(shape, fan_in, scale=1.0):
        return jax.random.normal(nxt(), shape, jnp.float32) * (scale * fan_in ** -0.5)

    def gain(shape):
        return 1.0 + 0.02 * jax.random.normal(nxt(), shape, jnp.float32)

    def small(shape, s=0.02):
        return s * jax.random.normal(nxt(), shape, jnp.float32)

    D = D_MODEL
    nA, nB, nC, nD = (_layers_of(m) for m in range(N_MIXERS))
    ret_cols = 2 * RET_HEADS * RET_DK + 2 * RET_HEADS * RET_DV
    gla_cols = 2 * GLA_HEADS * GLA_DK + 2 * GLA_HEADS * GLA_DV + GLA_GATE_RANK
    f_bias = jnp.linspace(3.0, 6.0, MLSTM_HEADS, dtype=jnp.float32)
    mlstm_b_gates = jnp.concatenate([small((nD, MLSTM_HEADS), 0.1),
                                     f_bias + small((nD, MLSTM_HEADS), 0.1)], axis=-1)
    return {
        'x': jax.random.normal(nxt(), (BATCH, SEQ, D), jnp.float32),
        'c': jax.random.normal(nxt(), (BATCH, D), jnp.float32),
        'ada_w': dense((DEPTH, D, 6 * D), D, 0.5),
        'ada_b': small((DEPTH, 6 * D)),
        'norm_mix_g': gain((DEPTH, D)),
        'norm_ffn_g': gain((DEPTH, D)),
        'norm_final_g': gain((D,)),
        'ret_w_in': dense((nA, D, ret_cols), D),
        'ret_gn_g': gain((nA, RET_HEADS, RET_DV)),
        'ret_w_out': dense((nA, RET_HEADS * RET_DV, D), RET_HEADS * RET_DV),
        'att_w_in': dense((nB, D, 3 * D), D),
        'att_rel_bias': small((nB, ATT_HEADS, ATT_N_REL), 0.5),
        'att_w_out': dense((nB, D, D), D),
        'gla_w_in': dense((nC, D, gla_cols), D),
        'gla_w_gate': dense((nC, GLA_GATE_RANK, GLA_HEADS * GLA_DK), GLA_GATE_RANK),
        'gla_b_gate': small((nC, GLA_HEADS * GLA_DK), 0.1),
        'gla_gn_g': gain((nC, GLA_HEADS, GLA_DV)),
        'gla_w_out': dense((nC, GLA_HEADS * GLA_DV, D), GLA_HEADS * GLA_DV),
        'mlstm_w_in': dense((nD, D, 2 * MLSTM_INNER), D),
        'mlstm_conv_w': dense((nD, MLSTM_CONV, MLSTM_INNER), MLSTM_CONV),
        'mlstm_conv_b': small((nD, MLSTM_INNER)),
        'mlstm_w_q': dense((nD, MLSTM_N_BLOCKS, MLSTM_QKV_BLOCK, MLSTM_QKV_BLOCK), MLSTM_QKV_BLOCK),
        'mlstm_w_k': dense((nD, MLSTM_N_BLOCKS, MLSTM_QKV_BLOCK, MLSTM_QKV_BLOCK), MLSTM_QKV_BLOCK),
        'mlstm_w_v': dense((nD, MLSTM_N_BLOCKS, MLSTM_QKV_BLOCK, MLSTM_QKV_BLOCK), MLSTM_QKV_BLOCK),
        'mlstm_w_gates': dense((nD, 3 * MLSTM_INNER, 2 * MLSTM_HEADS), 3 * MLSTM_INNER, 0.5),
        'mlstm_b_gates': mlstm_b_gates,
        'mlstm_gn_g': gain((nD, MLSTM_HEADS, MLSTM_DH)),
        'mlstm_skip': gain((nD, MLSTM_INNER)),
        'mlstm_w_out': dense((nD, MLSTM_INNER, D), MLSTM_INNER),
        'router_w': dense((DEPTH, D, N_EXPERTS), D),
        'router_b': small((DEPTH, N_EXPERTS), 0.01),
        'moe_w1': dense((DEPTH, N_EXPERTS, D, 2 * D_EXPERT), D),
        'moe_b1': small((DEPTH, N_EXPERTS, 2 * D_EXPERT)),
        'moe_w2': dense((DEPTH, N_EXPERTS, D_EXPERT, D), D_EXPERT),
        'moe_b2': small((DEPTH, N_EXPERTS, D)),
    }


def reference(x, c, ada_w, ada_b, norm_mix_g, norm_ffn_g, norm_final_g,
              ret_w_in, ret_gn_g, ret_w_out,
              att_w_in, att_rel_bias, att_w_out,
              gla_w_in, gla_w_gate, gla_b_gate, gla_gn_g, gla_w_out,
              mlstm_w_in, mlstm_conv_w, mlstm_conv_b, mlstm_w_q, mlstm_w_k, mlstm_w_v,
              mlstm_w_gates, mlstm_b_gates, mlstm_gn_g, mlstm_skip, mlstm_w_out,
              router_w, router_b, moe_w1, moe_b1, moe_w2, moe_b2):
    cond = jax.nn.silu(c)
    for i in range(DEPTH):
        kind, j = i % N_MIXERS, i // N_MIXERS
        mod = (cond @ ada_w[i] + ada_b[i])[:, None, :]
        sh_a, sc_a, g_a, sh_f, sc_f, g_f = jnp.split(mod, 6, axis=-1)
        h = rms_norm(x, norm_mix_g[i]) * (1.0 + sc_a) + sh_a
        if kind == 0:
            y = retention_mixer(h, ret_w_in[j], ret_gn_g[j], ret_w_out[j])
        elif kind == 1:
            y = chunked_attention_mixer(h, att_w_in[j], att_rel_bias[j], att_w_out[j])
        elif kind == 2:
            y = gla_mixer(h, gla_w_in[j], gla_w_gate[j], gla_b_gate[j], gla_gn_g[j], gla_w_out[j])
        else:
            y = mlstm_mixer(h, mlstm_w_in[j], mlstm_conv_w[j], mlstm_conv_b[j], mlstm_w_q[j], mlstm_w_k[j],
                            mlstm_w_v[j], mlstm_w_gates[j], mlstm_b_gates[j], mlstm_gn_g[j], mlstm_skip[j],
                            mlstm_w_out[j])
        x = x + g_a * y
        h = rms_norm(x, norm_ffn_g[i]) * (1.0 + sc_f) + sh_f
        x = x + g_f * moe_ffn(h, router_w[i], router_b[i], moe_w1[i], moe_b1[i], moe_w2[i], moe_b2[i])
    return rms_norm(x, norm_final_g)
```

```python
import functools

import jax
import jax.numpy as jnp
from jax import lax
from jax.experimental import pallas as pl
from jax.experimental.pallas import tpu as pltpu

F32 = jnp.float32
BF16 = jnp.bfloat16

CHUNK = 64
EPS = 1e-6
ROPE_BASE = 10000.0
RET_HEADS = 4
ATT_HEADS = 16
ATT_LEFT_CHUNKS = 8
ATT_MAX_REL = 256
GLA_HEADS = 4
GLA_GATE_RANK = 16
GLA_GATE_NORM = 16.0
MLSTM_HEADS = 4
MLSTM_CONV = 4
N_EXPERTS = 32
TOP_K = 4
SWIGLU_LIMIT = 7.0
SWIGLU_ALPHA = 1.702

VMEM_LIMIT_BYTES = 56 * 1024 * 1024
LANES = 128
MOE_TILE = 512
COMBINE_TILE = 256
GLA_PROJ_COLS = 3200


def _cparams(*sem):
    return pltpu.CompilerParams(dimension_semantics=sem, vmem_limit_bytes=VMEM_LIMIT_BYTES)


def _ada_kernel(c_ref, w_ref, b_ref, o_ref):
    c = c_ref[...]
    cond = c * jax.nn.sigmoid(c)
    o_ref[0] = jnp.dot(cond, w_ref[0], preferred_element_type=F32,
                       precision=lax.Precision.HIGHEST) + b_ref[0]


def _ada_mod(c, ada_w, ada_b):
    depth, d, n = ada_w.shape
    b = c.shape[0]
    tn = 1536
    return pl.pallas_call(
        _ada_kernel,
        grid=(depth, n // tn),
        in_specs=[pl.BlockSpec((b, d), lambda l, j: (0, 0)),
                  pl.BlockSpec((1, d, tn), lambda l, j: (l, 0, j)),
                  pl.BlockSpec((1, 1, tn), lambda l, j: (l, 0, j))],
        out_specs=pl.BlockSpec((1, b, tn), lambda l, j: (l, 0, j)),
        out_shape=jax.ShapeDtypeStruct((depth, b, n), F32),
        compiler_params=_cparams("arbitrary", "arbitrary"),
        name="ada_mod",
    )(c, ada_w, ada_b.reshape(depth, 1, n))


def _modulated_norm(x, g, sc, sh):
    ms = jnp.mean(x * x, axis=-1, keepdims=True)
    return (x * lax.rsqrt(ms + EPS) * g) * (1.0 + sc) + sh


def _norm_mm_kernel(x_ref, g_ref, sc_ref, sh_ref, w_ref, o_ref, h_scr):
    @pl.when(pl.program_id(1) == 0)
    def _():
        h = _modulated_norm(x_ref[...], g_ref[...], sc_ref[0], sh_ref[0])
        h_scr[...] = h.astype(BF16)

    o_ref[...] = jnp.dot(h_scr[...], w_ref[...], preferred_element_type=F32)


def _norm_mm(x, g, sc, sh, w, seq, tm=1024, tn=512):
    t, d = x.shape
    n = w.shape[1]
    per = seq // tm
    bsz = sc.shape[0]
    return pl.pallas_call(
        _norm_mm_kernel,
        grid=(t // tm, n // tn),
        in_specs=[pl.BlockSpec((tm, d), lambda i, j: (i, 0)),
                  pl.BlockSpec((1, d), lambda i, j: (0, 0)),
                  pl.BlockSpec((1, 1, d), lambda i, j: (i // per, 0, 0)),
                  pl.BlockSpec((1, 1, d), lambda i, j: (i // per, 0, 0)),
                  pl.BlockSpec((d, tn), lambda i, j: (0, j))],
        out_specs=pl.BlockSpec((tm, tn), lambda i, j: (i, j)),
        out_shape=jax.ShapeDtypeStruct((t, n), F32),
        scratch_shapes=[pltpu.VMEM((tm, d), BF16)],
        compiler_params=_cparams("arbitrary", "arbitrary"),
        name="norm_mm",
    )(x, g.reshape(1, d), sc.reshape(bsz, 1, d), sh.reshape(bsz, 1, d), w)


def _mm_res_kernel(y_ref, w_ref, x_ref, gate_ref, o_ref):
    acc = jnp.dot(y_ref[...].astype(BF16), w_ref[...], preferred_element_type=F32)
    o_ref[...] = x_ref[...] + gate_ref[0] * acc


def _mm_res(y, w, x, gate, seq, tm=512):
    t, k = y.shape
    d = w.shape[1]
    per = seq // tm
    bsz = gate.shape[0]
    return pl.pallas_call(
        _mm_res_kernel,
        grid=(t // tm,),
        in_specs=[pl.BlockSpec((tm, k), lambda i: (i, 0)),
                  pl.BlockSpec((k, d), lambda i: (0, 0)),
                  pl.BlockSpec((tm, d), lambda i: (i, 0)),
                  pl.BlockSpec((1, 1, d), lambda i: (i // per, 0, 0))],
        out_specs=pl.BlockSpec((tm, d), lambda i: (i, 0)),
        out_shape=jax.ShapeDtypeStruct((t, d), F32),
        compiler_params=_cparams("arbitrary"),
        name="mm_res",
    )(y, w, x, gate.reshape(bsz, 1, d))


def _final_norm_kernel(x_ref, g_ref, o_ref):
    x = x_ref[...]
    ms = jnp.mean(x * x, axis=-1, keepdims=True)
    o_ref[...] = x * lax.rsqrt(ms + EPS) * g_ref[...]


def _final_norm(x, g, tm=1024):
    t, d = x.shape
    return pl.pallas_call(
        _final_norm_kernel,
        grid=(t // tm,),
        in_specs=[pl.BlockSpec((tm, d), lambda i: (i, 0)),
                  pl.BlockSpec((1, d), lambda i: (0, 0))],
        out_specs=pl.BlockSpec((tm, d), lambda i: (i, 0)),
        out_shape=jax.ShapeDtypeStruct((t, d), F32),
        compiler_params=_cparams("arbitrary"),
        name="final_norm",
    )(x, g.reshape(1, d))


def _norm_router_kernel(x_ref, g_ref, sc_ref, sh_ref, rw_ref, rb_ref, h_ref, lg_ref):
    h = _modulated_norm(x_ref[...], g_ref[...], sc_ref[0], sh_ref[0])
    h_ref[...] = h
    lg_ref[...] = jnp.dot(h, rw_ref[...], preferred_element_type=F32,
                          precision=lax.Precision.HIGHEST) + rb_ref[...]


def _norm_router(x, g, sc, sh, rw, rb, seq, tm=512):
    t, d = x.shape
    per = seq // tm
    bsz = sc.shape[0]
    e = rw.shape[1]
    rw_p = jnp.pad(rw, ((0, 0), (0, LANES - e)))
    rb_p = jnp.pad(rb.reshape(1, e), ((0, 0), (0, LANES - e)))
    return pl.pallas_call(
        _norm_router_kernel,
        grid=(t // tm,),
        in_specs=[pl.BlockSpec((tm, d), lambda i: (i, 0)),
                  pl.BlockSpec((1, d), lambda i: (0, 0)),
                  pl.BlockSpec((1, 1, d), lambda i: (i // per, 0, 0)),
                  pl.BlockSpec((1, 1, d), lambda i: (i // per, 0, 0)),
                  pl.BlockSpec((d, LANES), lambda i: (0, 0)),
                  pl.BlockSpec((1, LANES), lambda i: (0, 0))],
        out_specs=[pl.BlockSpec((tm, d), lambda i: (i, 0)),
                   pl.BlockSpec((tm, LANES), lambda i: (i, 0))],
        out_shape=[jax.ShapeDtypeStruct((t, d), F32),
                   jax.ShapeDtypeStruct((t, LANES), F32)],
        compiler_params=_cparams("arbitrary"),
        name="norm_router",
    )(x, g.reshape(1, d), sc.reshape(bsz, 1, d), sh.reshape(bsz, 1, d), rw_p, rb_p)


def _row_gather_copy(src_hbm, row, dst_vmem, dst_row, sem):
    return pltpu.make_async_copy(src_hbm.at[pl.ds(row, 1)], dst_vmem.at[pl.ds(dst_row, 1)], sem)


def _expert_kernel(te_ref, nt_ref, idx_cur_ref, idx_nxt_ref, h_hbm, w1_ref, b1_ref, w2_ref, b2_ref,
                   o_ref, xbuf, sem):
    i = pl.program_id(0)
    n_used = nt_ref[0]
    tm = xbuf.shape[1]
    slot = i % 2

    def issue(idx_ref, s):
        def body(r, carry):
            _row_gather_copy(h_hbm, idx_ref[0, 0, r], xbuf.at[s], r, sem.at[s]).start()
            return carry
        lax.fori_loop(0, tm, body, 0, unroll=8)

    @pl.when(jnp.logical_and(i == 0, n_used > 0))
    def _():
        issue(idx_cur_ref, 0)

    @pl.when(i + 1 < n_used)
    def _():
        issue(idx_nxt_ref, 1 - slot)

    @pl.when(i < n_used)
    def _():
        pltpu.make_async_copy(h_hbm.at[pl.ds(0, tm)], xbuf.at[slot], sem.at[slot]).wait()
        x = xbuf[slot].astype(BF16)
        hid = jnp.dot(x, w1_ref[0], preferred_element_type=F32) + b1_ref[0]
        de = hid.shape[1] // 2
        gate = jnp.minimum(hid[:, :de], SWIGLU_LIMIT)
        lin = jnp.clip(hid[:, de:], -SWIGLU_LIMIT, SWIGLU_LIMIT)
        act = gate * jax.nn.sigmoid(SWIGLU_ALPHA * gate) * (lin + 1.0)
        o_ref[...] = jnp.dot(act.astype(BF16), w2_ref[0], preferred_element_type=F32) + b2_ref[0]

    @pl.when(i >= n_used)
    def _():
        o_ref[...] = jnp.zeros_like(o_ref)


def _expert_ffn(h, src_tok, tile_expert, n_tiles_used, w1, b1, w2, b2):
    t, d = h.shape
    p = src_tok.shape[0]
    tm = MOE_TILE
    nt = p // tm
    e, _, dh = w1.shape
    idx3 = src_tok.reshape(nt, 1, tm)
    grid_spec = pltpu.PrefetchScalarGridSpec(
        num_scalar_prefetch=2,
        grid=(nt,),
        in_specs=[
            pl.BlockSpec((1, 1, tm), lambda i, te, nu: (i, 0, 0), memory_space=pltpu.SMEM),
            pl.BlockSpec((1, 1, tm), lambda i, te, nu: (jnp.minimum(i + 1, nt - 1), 0, 0),
                         memory_space=pltpu.SMEM),
            pl.BlockSpec(memory_space=pl.ANY),
            pl.BlockSpec((1, d, dh), lambda i, te, nu: (te[i], 0, 0)),
            pl.BlockSpec((1, 1, dh), lambda i, te, nu: (te[i], 0, 0)),
            pl.BlockSpec((1, dh // 2, d), lambda i, te, nu: (te[i], 0, 0)),
            pl.BlockSpec((1, 1, d), lambda i, te, nu: (te[i], 0, 0)),
        ],
        out_specs=pl.BlockSpec((tm, d), lambda i, te, nu: (i, 0)),
        scratch_shapes=[pltpu.VMEM((2, tm, d), F32), pltpu.SemaphoreType.DMA((2,))],
    )
    return pl.pallas_call(
        _expert_kernel,
        grid_spec=grid_spec,
        out_shape=jax.ShapeDtypeStruct((p, d), F32),
        compiler_params=_cparams("arbitrary"),
        name="expert_ffn",
    )(tile_expert, n_tiles_used, idx3, idx3, h, w1, b1.reshape(e, 1, dh), w2, b2.reshape(e, 1, d))


def _combine_kernel(idx_cur_ref, idx_nxt_ref, y_hbm, x_ref, w_ref, gate_ref, o_ref, ybuf, sem):
    i = pl.program_id(0)
    n = pl.num_programs(0)
    tc = x_ref.shape[0]
    slot = i % 2

    def issue(idx_ref, s):
        def body(r, carry):
            for k in range(TOP_K):
                _row_gather_copy(y_hbm, idx_ref[0, 0, r * TOP_K + k], ybuf.at[s, k], r, sem.at[s]).start()
            return carry
        lax.fori_loop(0, tc, body, 0, unroll=4)

    @pl.when(i == 0)
    def _():
        issue(idx_cur_ref, 0)

    @pl.when(i + 1 < n)
    def _():
        issue(idx_nxt_ref, 1 - slot)

    for k in range(TOP_K):
        pltpu.make_async_copy(y_hbm.at[pl.ds(0, tc)], ybuf.at[slot, k], sem.at[slot]).wait()
    w = w_ref[...]
    acc = w[:, 0:1] * ybuf[slot, 0]
    for k in range(1, TOP_K):
        acc = acc + w[:, k:k + 1] * ybuf[slot, k]
    o_ref[...] = x_ref[...] + gate_ref[0] * acc


def _moe_combine(y, slot_of_pair, top_w, x, gate, seq):
    t, d = x.shape
    tc = COMBINE_TILE
    nt = t // tc
    per = seq // tc
    bsz = gate.shape[0]
    idx3 = slot_of_pair.reshape(nt, 1, tc * TOP_K)
    return pl.pallas_call(
        _combine_kernel,
        grid=(nt,),
        in_specs=[
            pl.BlockSpec((1, 1, tc * TOP_K), lambda i: (i, 0, 0), memory_space=pltpu.SMEM),
            pl.BlockSpec((1, 1, tc * TOP_K), lambda i: (jnp.minimum(i + 1, nt - 1), 0, 0),
                         memory_space=pltpu.SMEM),
            pl.BlockSpec(memory_space=pl.ANY),
            pl.BlockSpec((tc, d), lambda i: (i, 0)),
            pl.BlockSpec((tc, TOP_K), lambda i: (i, 0)),
            pl.BlockSpec((1, 1, d), lambda i: (i // per, 0, 0)),
        ],
        out_specs=pl.BlockSpec((tc, d), lambda i: (i, 0)),
        out_shape=jax.ShapeDtypeStruct((t, d), F32),
        scratch_shapes=[pltpu.VMEM((2, TOP_K, tc, d), F32), pltpu.SemaphoreType.DMA((2,))],
        compiler_params=_cparams("arbitrary"),
        name="moe_combine",
    )(idx3, idx3, y, x, top_w, gate.reshape(bsz, 1, d))


def _route(logits):
    t = logits.shape[0]
    top_v, top_i = lax.top_k(logits[:, :N_EXPERTS], TOP_K)
    top_w = jax.nn.softmax(top_v, axis=-1)
    flat_e = top_i.reshape(-1).astype(jnp.int32)
    onehot = (flat_e[:, None] == jnp.arange(N_EXPERTS, dtype=jnp.int32)[None, :]).astype(jnp.int32)
    csum = jnp.cumsum(onehot, axis=0)
    rank = jnp.sum(onehot * csum, axis=1) - 1
    counts = csum[-1]
    padded = ((counts + MOE_TILE - 1) // MOE_TILE) * MOE_TILE
    ends = jnp.cumsum(padded)
    starts = ends - padded
    slot_of_pair = starts[flat_e] + rank
    n_slots = t * TOP_K + N_EXPERTS * MOE_TILE
    n_tiles = n_slots // MOE_TILE
    src_tok = jnp.zeros((n_slots,), jnp.int32).at[slot_of_pair].set(
        jnp.arange(t * TOP_K, dtype=jnp.int32) // TOP_K)
    tile_start = jnp.arange(n_tiles, dtype=jnp.int32) * MOE_TILE
    tile_expert = jnp.sum((tile_start[:, None] >= ends[None, :]).astype(jnp.int32), axis=1)
    n_used = (ends[-1] // MOE_TILE).astype(jnp.int32)
    last_e = tile_expert[jnp.maximum(n_used - 1, 0)]
    tile_expert = jnp.where(jnp.arange(n_tiles) < n_used, tile_expert, last_e).astype(jnp.int32)
    return top_w, slot_of_pair.astype(jnp.int32), src_tok, tile_expert, n_used.reshape(1)


def _moe_layer(x, g, sc, sh, gate, rw, rb, w1, b1, w2, b2, seq):
    h, logits = _norm_router(x, g, sc, sh, rw, rb, seq)
    top_w, slot_of_pair, src_tok, tile_expert, n_used = _route(logits)
    y = _expert_ffn(h, src_tok, tile_expert, n_used, w1, b1, w2, b2)
    return _moe_combine(y, slot_of_pair, top_w, x, gate, seq)


def _head_norm(x, g, center):
    if center:
        x = x - jnp.mean(x, axis=-1, keepdims=True)
    y = x * lax.rsqrt(jnp.mean(x * x, axis=-1, keepdims=True) + EPS)
    return y * g[None, :, None, :]


def _to_heads(t, n_heads):
    b, s, _ = t.shape
    return t.reshape(b, s, n_heads, -1).transpose(0, 2, 1, 3)


def _from_heads(t):
    b, h, s, d = t.shape
    return t.transpose(0, 2, 1, 3).reshape(b, s, h * d)


def _rope(t, pos):
    half = t.shape[-1] // 2
    inv = ROPE_BASE ** (-jnp.arange(half, dtype=F32) / half)
    ang = pos.astype(F32)[:, None] * inv[None, :]
    cos = jnp.cos(ang)[None, :, None, :]
    sin = jnp.sin(ang)[None, :, None, :]
    t1, t2 = t[..., :half], t[..., half:]
    return jnp.concatenate([t1 * cos - t2 * sin, t1 * sin + t2 * cos], axis=-1)


def _chunk_linear_attention(q, k, v, log_a):
    b_, h_, s_, _ = q.shape
    dv = v.shape[-1]
    nc = s_ // CHUNK

    def chunks(t):
        return t.reshape(t.shape[0], t.shape[1], nc, CHUNK, t.shape[3])

    qc, kc, vc = chunks(q), chunks(k), chunks(v)
    b = jnp.cumsum(chunks(log_a.astype(F32)), axis=3)
    b_last = b[:, :, :, -1:, :]
    q_in = qc * jnp.exp(b)
    k_in = kc * jnp.exp(-b)
    k_st = kc * jnp.exp(b_last - b)
    causal = jnp.tril(jnp.ones((CHUNK, CHUNK), dtype=bool))
    scores = jnp.where(causal, jnp.einsum('bhcld,bhcmd->bhclm', q_in, k_in), 0.0)
    o_intra = jnp.einsum('bhclm,bhcme->bhcle', scores, vc)
    decay = jnp.exp(b_last[:, :, :, 0, :])

    def step(state, xs):
        q_c, k_c, v_c, d_c = xs
        o = jnp.einsum('bhld,bhde->bhle', q_c, state)
        state = d_c[..., None] * state + jnp.einsum('bhld,bhle->bhde', k_c, v_c)
        return state, o

    mv = lambda t: jnp.moveaxis(t, 2, 0)
    init = jnp.zeros((b_, h_, q.shape[-1], dv), F32)
    _, o_inter = lax.scan(step, init, (mv(q_in), mv(k_st), mv(vc), mv(decay)))
    o = o_intra + jnp.moveaxis(o_inter, 0, 2)
    return o.reshape(b_, h_, s_, dv)


def _retention_core(proj, gn_g, bsz, seq):
    d = gn_g.shape[0] * gn_g.shape[1] // 2
    dk = d // RET_HEADS
    p = proj.reshape(bsz, seq, -1)
    q, k, v, g = jnp.split(p, [d, 2 * d, 4 * d], axis=-1)
    pos = jnp.arange(seq)
    q = _rope(q.reshape(bsz, seq, RET_HEADS, dk), pos)
    k = _rope(k.reshape(bsz, seq, RET_HEADS, dk), pos) * dk ** -0.5
    q, k = q.transpose(0, 2, 1, 3), k.transpose(0, 2, 1, 3)
    v = _to_heads(v, RET_HEADS)
    log_gamma = jnp.log1p(-jnp.exp2(-5.0 - jnp.arange(RET_HEADS, dtype=F32)))
    log_a = jnp.broadcast_to(log_gamma[None, :, None, None], (1, RET_HEADS, seq, 1))
    o = _head_norm(_chunk_linear_attention(q, k, v, log_a), gn_g, center=True)
    return (_from_heads(o) * jax.nn.silu(g)).reshape(bsz * seq, -1)


def _attention_core(proj, rel_bias, bsz, seq):
    d = proj.shape[-1] // 3
    dh = d // ATT_HEADS
    left = ATT_LEFT_CHUNKS * CHUNK
    band = left + CHUNK
    nc = seq // CHUNK
    p = proj.reshape(bsz, seq, -1)
    q, k, v = jnp.split(p, 3, axis=-1)
    q = q.reshape(bsz, seq, ATT_HEADS, dh) * dh ** -0.5
    pad = ((0, 0), (left, 0), (0, 0), (0, 0))
    k = jnp.pad(k.reshape(bsz, seq, ATT_HEADS, dh), pad)
    v = jnp.pad(v.reshape(bsz, seq, ATT_HEADS, dh), pad)
    qi = jnp.arange(CHUNK)[:, None]
    kj = jnp.arange(band)[None, :]
    dist = qi - kj + left
    idx = jnp.clip(dist, -(CHUNK - 1), ATT_MAX_REL) + (CHUNK - 1)
    bias = rel_bias[:, idx].astype(F32)
    qc = jnp.moveaxis(q.reshape(bsz, nc, CHUNK, ATT_HEADS, dh), 1, 0)

    def one_chunk(args):
        ci, q_c = args
        start = ci * CHUNK
        k_b = lax.dynamic_slice_in_dim(k, start, band, axis=1)
        v_b = lax.dynamic_slice_in_dim(v, start, band, axis=1)
        s = jnp.einsum('blhd,bmhd->bhlm', q_c, k_b).astype(F32) + bias
        valid = kj >= left - start
        pr = jax.nn.softmax(jnp.where(valid, s, -jnp.inf), axis=-1)
        return jnp.einsum('bhlm,bmhd->blhd', pr, v_b)

    o = lax.map(one_chunk, (jnp.arange(nc), qc))
    return jnp.moveaxis(o, 0, 1).reshape(bsz * seq, d)


def _gla_core(proj, w_gate, b_gate, gn_g, bsz, seq):
    dvv = gn_g.shape[0] * gn_g.shape[1]
    dq = dvv // 2
    dk = dq // GLA_HEADS
    p = proj.reshape(bsz, seq, -1)
    q, k, v, g_low, r = jnp.split(p, [dq, 2 * dq, 2 * dq + dvv, 2 * dq + dvv + GLA_GATE_RANK], axis=-1)
    log_a = jax.nn.log_sigmoid(
        jnp.dot(g_low, w_gate, precision=lax.Precision.HIGHEST) + b_gate) / GLA_GATE_NORM
    q = _to_heads(q, GLA_HEADS) * dk ** -0.5
    o = _chunk_linear_attention(q, _to_heads(k, GLA_HEADS), _to_heads(v, GLA_HEADS),
                                _to_heads(log_a, GLA_HEADS))
    o = _head_norm(o, gn_g, center=False)
    return (_from_heads(o) * jax.nn.silu(r)).reshape(bsz * seq, -1)


def _causal_depthwise_conv(x, w, b):
    ch = x.shape[-1]
    y = lax.conv_general_dilated(x, w[:, None, :], window_strides=(1,),
                                 padding=[(MLSTM_CONV - 1, 0)],
                                 dimension_numbers=('NWC', 'WIO', 'NWC'),
                                 feature_group_count=ch)
    return y + b


def _mlstm_chunkwise(q, k, v, ig, log_f):
    b_, h_, s_, dh = q.shape
    nc = s_ // CHUNK

    def chunks(t):
        return jnp.moveaxis(t.reshape((b_, h_, nc, CHUNK) + t.shape[3:]), 2, 0)

    causal = jnp.tril(jnp.ones((CHUNK, CHUNK), dtype=bool))

    def step(carry, xs):
        cmat, nvec, m = carry
        q_c, k_c, v_c, i_c, f_c = xs
        b = jnp.cumsum(f_c, axis=-1)
        log_intra = jnp.where(causal, b[..., :, None] - b[..., None, :] + i_c[..., None, :], -jnp.inf)
        log_inter = b + m[..., None]
        m_row = jnp.maximum(log_inter, jnp.max(log_intra, axis=-1))
        w_intra = jnp.exp(log_intra - m_row[..., None])
        w_inter = jnp.exp(log_inter - m_row)
        s = jnp.einsum('bhld,bhmd->bhlm', q_c, k_c) * w_intra
        num = w_inter[..., None] * jnp.einsum('bhld,bhde->bhle', q_c, cmat) + jnp.einsum('bhlm,bhme->bhle', s, v_c)
        den = w_inter * jnp.einsum('bhld,bhd->bhl', q_c, nvec) + jnp.sum(s, axis=-1)
        h_c = num / jnp.maximum(jnp.abs(den), jnp.exp(-m_row))[..., None]
        b_last = b[..., -1]
        log_keep = b_last + m
        log_w = b_last[..., None] - b + i_c
        m_new = jnp.maximum(log_keep, jnp.max(log_w, axis=-1))
        keep = jnp.exp(log_keep - m_new)
        w = jnp.exp(log_w - m_new[..., None])
        cmat = keep[..., None, None] * cmat + jnp.einsum('bhl,bhld,bhle->bhde', w, k_c, v_c)
        nvec = keep[..., None] * nvec + jnp.einsum('bhl,bhld->bhd', w, k_c)
        return (cmat, nvec, m_new), h_c

    init = (jnp.zeros((b_, h_, dh, dh), F32), jnp.zeros((b_, h_, dh), F32), jnp.zeros((b_, h_), F32))
    _, hs = lax.scan(step, init, (chunks(q), chunks(k), chunks(v), chunks(ig), chunks(log_f)))
    return jnp.moveaxis(hs, 0, 2).reshape(b_, h_, s_, dh)


def _mlstm_core(proj, conv_w, conv_b, w_q, w_k, w_v, w_gates, b_gates, gn_g, skip, bsz, seq):
    inner = proj.shape[-1] // 2
    dh = inner // MLSTM_HEADS
    nb, blk = w_q.shape[0], w_q.shape[1]
    p = proj.reshape(bsz, seq, -1)
    xm, z = jnp.split(p, 2, axis=-1)
    xc = jax.nn.silu(_causal_depthwise_conv(xm, conv_w, conv_b))

    def blockdiag(t, w):
        t = t.reshape(bsz, seq, nb, blk)
        return jnp.einsum('bsnc,ncd->bsnd', t, w).reshape(bsz, seq, inner)

    q = blockdiag(xc, w_q)
    k = blockdiag(xc, w_k) * dh ** -0.5
    v = blockdiag(xm, w_v)
    gates = jnp.concatenate([q, k, v], axis=-1) @ w_gates + b_gates
    ig, fg = jnp.split(gates.transpose(0, 2, 1), 2, axis=1)
    hh = _mlstm_chunkwise(_to_heads(q, MLSTM_HEADS), _to_heads(k, MLSTM_HEADS), _to_heads(v, MLSTM_HEADS),
                          ig, jax.nn.log_sigmoid(fg))
    hh = _head_norm(hh, gn_g, center=True)
    out = (_from_heads(hh) + skip * xc) * jax.nn.silu(z)
    return out.reshape(bsz * seq, inner)


def kernel(x, c, ada_w, ada_b, norm_mix_g, norm_ffn_g, norm_final_g, ret_w_in, ret_gn_g, ret_w_out, att_w_in, att_rel_bias, att_w_out, gla_w_in, gla_w_gate, gla_b_gate, gla_gn_g, gla_w_out, mlstm_w_in, mlstm_conv_w, mlstm_conv_b, mlstm_w_q, mlstm_w_k, mlstm_w_v, mlstm_w_gates, mlstm_b_gates, mlstm_gn_g, mlstm_skip, mlstm_w_out, router_w, router_b, moe_w1, moe_b1, moe_w2, moe_b2):
    bsz, seq, d = x.shape
    depth = ada_w.shape[0]
    xt = x.reshape(bsz * seq, d)
    mod = _ada_mod(c, ada_w, ada_b)
    w1_bf = moe_w1.astype(BF16)
    w2_bf = moe_w2.astype(BF16)
    for i in range(depth):
        kind, j = i % 4, i // 4
        sh_a, sc_a, g_a, sh_f, sc_f, g_f = jnp.split(mod[i], 6, axis=-1)
        if kind == 0:
            proj = _norm_mm(xt, norm_mix_g[i], sc_a, sh_a, ret_w_in[j].astype(BF16), seq)
            y = _retention_core(proj, ret_gn_g[j], bsz, seq)
            w_out = ret_w_out[j]
        elif kind == 1:
            proj = _norm_mm(xt, norm_mix_g[i], sc_a, sh_a, att_w_in[j].astype(BF16), seq)
            y = _attention_core(proj, att_rel_bias[j], bsz, seq)
            w_out = att_w_out[j]
        elif kind == 2:
            n_gla = gla_w_in.shape[-1]
            w_gla = jnp.pad(gla_w_in[j], ((0, 0), (0, GLA_PROJ_COLS - n_gla))).astype(BF16)
            proj = _norm_mm(xt, norm_mix_g[i], sc_a, sh_a, w_gla, seq, tn=GLA_PROJ_COLS // 5)[:, :n_gla]
            y = _gla_core(proj, gla_w_gate[j], gla_b_gate[j], gla_gn_g[j], bsz, seq)
            w_out = gla_w_out[j]
        else:
            proj = _norm_mm(xt, norm_mix_g[i], sc_a, sh_a, mlstm_w_in[j].astype(BF16), seq)
            y = _mlstm_core(proj, mlstm_conv_w[j], mlstm_conv_b[j], mlstm_w_q[j], mlstm_w_k[j], mlstm_w_v[j],
                            mlstm_w_gates[j], mlstm_b_gates[j], mlstm_gn_g[j], mlstm_skip[j], bsz, seq)
            w_out = mlstm_w_out[j]
        xt = _mm_res(y, w_out.astype(BF16), xt, g_a, seq)
        xt = _moe_layer(xt, norm_ffn_g[i], sc_f, sh_f, g_f, router_w[i], router_b[i],
                        w1_bf[i], moe_b1[i], w2_bf[i], moe_b2[i], seq)
    return _final_norm(xt, norm_final_g).reshape(bsz, seq, d)
```

```python
import functools

import jax
import jax.numpy as jnp
from jax import lax
from jax.experimental import pallas as pl
from jax.experimental.pallas import tpu as pltpu

F32 = jnp.float32
BF16 = jnp.bfloat16

CHUNK = 64
EPS = 1e-6
ROPE_BASE = 10000.0
RET_HEADS = 4
ATT_HEADS = 16
ATT_LEFT_CHUNKS = 8
ATT_MAX_REL = 256
GLA_HEADS = 4
GLA_GATE_RANK = 16
GLA_GATE_NORM = 16.0
MLSTM_HEADS = 4
MLSTM_CONV = 4
N_EXPERTS = 32
TOP_K = 4
SWIGLU_LIMIT = 7.0
SWIGLU_ALPHA = 1.702

VMEM_LIMIT_BYTES = 56 * 1024 * 1024
LANES = 128
MOE_TILE = 512
COMBINE_TILE = 256
GLA_PROJ_COLS = 3200


def _cparams(*sem):
    return pltpu.CompilerParams(dimension_semantics=sem, vmem_limit_bytes=VMEM_LIMIT_BYTES)


def _ada_kernel(c_ref, w_ref, b_ref, o_ref):
    c = c_ref[...]
    cond = c * jax.nn.sigmoid(c)
    o_ref[0] = jnp.dot(cond, w_ref[0], preferred_element_type=F32,
                       precision=lax.Precision.HIGHEST) + b_ref[0]


def _ada_mod(c, ada_w, ada_b):
    depth, d, n = ada_w.shape
    b = c.shape[0]
    tn = 1536
    return pl.pallas_call(
        _ada_kernel,
        grid=(depth, n // tn),
        in_specs=[pl.BlockSpec((b, d), lambda l, j: (0, 0)),
                  pl.BlockSpec((1, d, tn), lambda l, j: (l, 0, j)),
                  pl.BlockSpec((1, 1, tn), lambda l, j: (l, 0, j))],
        out_specs=pl.BlockSpec((1, b, tn), lambda l, j: (l, 0, j)),
        out_shape=jax.ShapeDtypeStruct((depth, b, n), F32),
        compiler_params=_cparams("arbitrary", "arbitrary"),
        name="ada_mod",
    )(c, ada_w, ada_b.reshape(depth, 1, n))


def _modulated_norm(x, g, sc, sh):
    ms = jnp.mean(x * x, axis=-1, keepdims=True)
    return (x * lax.rsqrt(ms + EPS) * g) * (1.0 + sc) + sh


def _norm_mm_kernel(x_ref, g_ref, sc_ref, sh_ref, w_ref, o_ref, h_scr):
    @pl.when(pl.program_id(1) == 0)
    def _():
        h = _modulated_norm(x_ref[...], g_ref[...], sc_ref[0], sh_ref[0])
        h_scr[...] = h.astype(BF16)

    o_ref[...] = jnp.dot(h_scr[...], w_ref[...], preferred_element_type=F32)


def _norm_mm(x, g, sc, sh, w, seq, tm=1024, tn=512):
    t, d = x.shape
    n = w.shape[1]
    per = seq // tm
    bsz = sc.shape[0]
    return pl.pallas_call(
        _norm_mm_kernel,
        grid=(t // tm, n // tn),
        in_specs=[pl.BlockSpec((tm, d), lambda i, j: (i, 0)),
                  pl.BlockSpec((1, d), lambda i, j: (0, 0)),
                  pl.BlockSpec((1, 1, d), lambda i, j: (i // per, 0, 0)),
                  pl.BlockSpec((1, 1, d), lambda i, j: (i // per, 0, 0)),
                  pl.BlockSpec((d, tn), lambda i, j: (0, j))],
        out_specs=pl.BlockSpec((tm, tn), lambda i, j: (i, j)),
        out_shape=jax.ShapeDtypeStruct((t, n), F32),
        scratch_shapes=[pltpu.VMEM((tm, d), BF16)],
        compiler_params=_cparams("arbitrary", "arbitrary"),
        name="norm_mm",
    )(x, g.reshape(1, d), sc.reshape(bsz, 1, d), sh.reshape(bsz, 1, d), w)


def _mm_res_kernel(y_ref, w_ref, x_ref, gate_ref, o_ref):
    acc = jnp.dot(y_ref[...].astype(BF16), w_ref[...], preferred_element_type=F32)
    o_ref[...] = x_ref[...] + gate_ref[0] * acc


def _mm_res(y, w, x, gate, seq, tm=512):
    t, k = y.shape
    d = w.shape[1]
    per = seq // tm
    bsz = gate.shape[0]
    return pl.pallas_call(
        _mm_res_kernel,
        grid=(t // tm,),
        in_specs=[pl.BlockSpec((tm, k), lambda i: (i, 0)),
                  pl.BlockSpec((k, d), lambda i: (0, 0)),
                  pl.BlockSpec((tm, d), lambda i: (i, 0)),
                  pl.BlockSpec((1, 1, d), lambda i: (i // per, 0, 0))],
        out_specs=pl.BlockSpec((tm, d), lambda i: (i, 0)),
        out_shape=jax.ShapeDtypeStruct((t, d), F32),
        compiler_params=_cparams("arbitrary"),
        name="mm_res",
    )(y, w, x, gate.reshape(bsz, 1, d))


def _final_norm_kernel(x_ref, g_ref, o_ref):
    x = x_ref[...]
    ms = jnp.mean(x * x, axis=-1, keepdims=True)
    o_ref[...] = x * lax.rsqrt(ms + EPS) * g_ref[...]


def _final_norm(x, g, tm=1024):
    t, d = x.shape
    return pl.pallas_call(
        _final_norm_kernel,
        grid=(t // tm,),
        in_specs=[pl.BlockSpec((tm, d), lambda i: (i, 0)),
                  pl.BlockSpec((1, d), lambda i: (0, 0))],
        out_specs=pl.BlockSpec((tm, d), lambda i: (i, 0)),
        out_shape=jax.ShapeDtypeStruct((t, d), F32),
        compiler_params=_cparams("arbitrary"),
        name="final_norm",
    )(x, g.reshape(1, d))


def _norm_router_kernel(x_ref, g_ref, sc_ref, sh_ref, rw_ref, rb_ref, h_ref, lg_ref):
    h = _modulated_norm(x_ref[...], g_ref[...], sc_ref[0], sh_ref[0])
    h_ref[...] = h
    lg_ref[...] = jnp.dot(h, rw_ref[...], preferred_element_type=F32,
                          precision=lax.Precision.HIGHEST) + rb_ref[...]


def _norm_router(x, g, sc, sh, rw, rb, seq, tm=512):
    t, d = x.shape
    per = seq // tm
    bsz = sc.shape[0]
    e = rw.shape[1]
    rw_p = jnp.pad(rw, ((0, 0), (0, LANES - e)))
    rb_p = jnp.pad(rb.reshape(1, e), ((0, 0), (0, LANES - e)))
    return pl.pallas_call(
        _norm_router_kernel,
        grid=(t // tm,),
        in_specs=[pl.BlockSpec((tm, d), lambda i: (i, 0)),
                  pl.BlockSpec((1, d), lambda i: (0, 0)),
                  pl.BlockSpec((1, 1, d), lambda i: (i // per, 0, 0)),
                  pl.BlockSpec((1, 1, d), lambda i: (i // per, 0, 0)),
                  pl.BlockSpec((d, LANES), lambda i: (0, 0)),
                  pl.BlockSpec((1, LANES), lambda i: (0, 0))],
        out_specs=[pl.BlockSpec((tm, d), lambda i: (i, 0)),
                   pl.BlockSpec((tm, LANES), lambda i: (i, 0))],
        out_shape=[jax.ShapeDtypeStruct((t, d), F32),
                   jax.ShapeDtypeStruct((t, LANES), F32)],
        compiler_params=_cparams("arbitrary"),
        name="norm_router",
    )(x, g.reshape(1, d), sc.reshape(bsz, 1, d), sh.reshape(bsz, 1, d), rw_p, rb_p)


def _row_gather_copy(src_hbm, row, dst_vmem, dst_row, sem):
    return pltpu.make_async_copy(src_hbm.at[pl.ds(row, 1)], dst_vmem.at[pl.ds(dst_row, 1)], sem)


def _expert_kernel(te_ref, nt_ref, idx_cur_ref, idx_nxt_ref, h_hbm, w1_ref, b1_ref, w2_ref, b2_ref,
                   o_ref, xbuf, sem):
    i = pl.program_id(0)
    n_used = nt_ref[0]
    tm = xbuf.shape[1]
    slot = i % 2

    def issue(idx_ref, s):
        def body(r, carry):
            _row_gather_copy(h_hbm, idx_ref[0, 0, r], xbuf.at[s], r, sem.at[s]).start()
            return carry
        lax.fori_loop(0, tm, body, 0, unroll=8)

    @pl.when(jnp.logical_and(i == 0, n_used > 0))
    def _():
        issue(idx_cur_ref, 0)

    @pl.when(i + 1 < n_used)
    def _():
        issue(idx_nxt_ref, 1 - slot)

    @pl.when(i < n_used)
    def _():
        pltpu.make_async_copy(h_hbm.at[pl.ds(0, tm)], xbuf.at[slot], sem.at[slot]).wait()
        x = xbuf[slot].astype(BF16)
        hid = jnp.dot(x, w1_ref[0], preferred_element_type=F32) + b1_ref[0]
        de = hid.shape[1] // 2
        gate = jnp.minimum(hid[:, :de], SWIGLU_LIMIT)
        lin = jnp.clip(hid[:, de:], -SWIGLU_LIMIT, SWIGLU_LIMIT)
        act = gate * jax.nn.sigmoid(SWIGLU_ALPHA * gate) * (lin + 1.0)
        o_ref[...] = jnp.dot(act.astype(BF16), w2_ref[0], preferred_element_type=F32) + b2_ref[0]

    @pl.when(i >= n_used)
    def _():
        o_ref[...] = jnp.zeros_like(o_ref)


def _expert_ffn(h, src_tok, tile_expert, n_tiles_used, w1, b1, w2, b2):
    t, d = h.shape
    p = src_tok.shape[0]
    tm = MOE_TILE
    nt = p // tm
    e, _, dh = w1.shape
    idx3 = src_tok.reshape(nt, 1, tm)
    grid_spec = pltpu.PrefetchScalarGridSpec(
        num_scalar_prefetch=2,
        grid=(nt,),
        in_specs=[
            pl.BlockSpec((1, 1, tm), lambda i, te, nu: (i, 0, 0), memory_space=pltpu.SMEM),
            pl.BlockSpec((1, 1, tm), lambda i, te, nu: (jnp.minimum(i + 1, nt - 1), 0, 0),
                         memory_space=pltpu.SMEM),
            pl.BlockSpec(memory_space=pl.ANY),
            pl.BlockSpec((1, d, dh), lambda i, te, nu: (te[i], 0, 0)),
            pl.BlockSpec((1, 1, dh), lambda i, te, nu: (te[i], 0, 0)),
            pl.BlockSpec((1, dh // 2, d), lambda i, te, nu: (te[i], 0, 0)),
            pl.BlockSpec((1, 1, d), lambda i, te, nu: (te[i], 0, 0)),
        ],
        out_specs=pl.BlockSpec((tm, d), lambda i, te, nu: (i, 0)),
        scratch_shapes=[pltpu.VMEM((2, tm, d), F32), pltpu.SemaphoreType.DMA((2,))],
    )
    return pl.pallas_call(
        _expert_kernel,
        grid_spec=grid_spec,
        out_shape=jax.ShapeDtypeStruct((p, d), F32),
        compiler_params=_cparams("arbitrary"),
        name="expert_ffn",
    )(tile_expert, n_tiles_used, idx3, idx3, h, w1, b1.reshape(e, 1, dh), w2, b2.reshape(e, 1, d))


def _combine_kernel(idx_cur_ref, idx_nxt_ref, y_hbm, x_ref, w_ref, gate_ref, o_ref, ybuf, sem):
    i = pl.program_id(0)
    n = pl.num_programs(0)
    tc = x_ref.shape[0]
    slot = i % 2

    def issue(idx_ref, s):
        def body(r, carry):
            for k in range(TOP_K):
                _row_gather_copy(y_hbm, idx_ref[0, 0, r * TOP_K + k], ybuf.at[s, k], r, sem.at[s]).start()
            return carry
        lax.fori_loop(0, tc, body, 0, unroll=4)

    @pl.when(i == 0)
    def _():
        issue(idx_cur_ref, 0)

    @pl.when(i + 1 < n)
    def _():
        issue(idx_nxt_ref, 1 - slot)

    for k in range(TOP_K):
        pltpu.make_async_copy(y_hbm.at[pl.ds(0, tc)], ybuf.at[slot, k], sem.at[slot]).wait()
    w = w_ref[...]
    acc = w[:, 0:1] * ybuf[slot, 0]
    for k in range(1, TOP_K):
        acc = acc + w[:, k:k + 1] * ybuf[slot, k]
    o_ref[...] = x_ref[...] + gate_ref[0] * acc


def _moe_combine(y, slot_of_pair, top_w, x, gate, seq):
    t, d = x.shape
    tc = COMBINE_TILE
    nt = t // tc
    per = seq // tc
    bsz = gate.shape[0]
    idx3 = slot_of_pair.reshape(nt, 1, tc * TOP_K)
    return pl.pallas_call(
        _combine_kernel,
        grid=(nt,),
        in_specs=[
            pl.BlockSpec((1, 1, tc * TOP_K), lambda i: (i, 0, 0), memory_space=pltpu.SMEM),
            pl.BlockSpec((1, 1, tc * TOP_K), lambda i: (jnp.minimum(i + 1, nt - 1), 0, 0),
                         memory_space=pltpu.SMEM),
            pl.BlockSpec(memory_space=pl.ANY),
            pl.BlockSpec((tc, d), lambda i: (i, 0)),
            pl.BlockSpec((tc, TOP_K), lambda i: (i, 0)),
            pl.BlockSpec((1, 1, d), lambda i: (i // per, 0, 0)),
        ],
        out_specs=pl.BlockSpec((tc, d), lambda i: (i, 0)),
        out_shape=jax.ShapeDtypeStruct((t, d), F32),
        scratch_shapes=[pltpu.VMEM((2, TOP_K, tc, d), F32), pltpu.SemaphoreType.DMA((2,))],
        compiler_params=_cparams("arbitrary"),
        name="moe_combine",
    )(idx3, idx3, y, x, top_w, gate.reshape(bsz, 1, d))


def _route(logits):
    t = logits.shape[0]
    top_v, top_i = lax.top_k(logits[:, :N_EXPERTS], TOP_K)
    top_w = jax.nn.softmax(top_v, axis=-1)
    flat_e = top_i.reshape(-1).astype(jnp.int32)
    onehot = (flat_e[:, None] == jnp.arange(N_EXPERTS, dtype=jnp.int32)[None, :]).astype(jnp.int32)
    csum = jnp.cumsum(onehot, axis=0)
    rank = jnp.sum(onehot * csum, axis=1) - 1
    counts = csum[-1]
    padded = ((counts + MOE_TILE - 1) // MOE_TILE) * MOE_TILE
    ends = jnp.cumsum(padded)
    starts = ends - padded
    slot_of_pair = starts[flat_e] + rank
    n_slots = t * TOP_K + N_EXPERTS * MOE_TILE
    n_tiles = n_slots // MOE_TILE
    src_tok = jnp.zeros((n_slots,), jnp.int32).at[slot_of_pair].set(
        jnp.arange(t * TOP_K, dtype=jnp.int32) // TOP_K)
    tile_start = jnp.arange(n_tiles, dtype=jnp.int32) * MOE_TILE
    tile_expert = jnp.sum((tile_start[:, None] >= ends[None, :]).astype(jnp.int32), axis=1)
    n_used = (ends[-1] // MOE_TILE).astype(jnp.int32)
    last_e = tile_expert[jnp.maximum(n_used - 1, 0)]
    tile_expert = jnp.where(jnp.arange(n_tiles) < n_used, tile_expert, last_e).astype(jnp.int32)
    return top_w, slot_of_pair.astype(jnp.int32), src_tok, tile_expert, n_used.reshape(1)


def _moe_layer(x, g, sc, sh, gate, rw, rb, w1, b1, w2, b2, seq):
    h, logits = _norm_router(x, g, sc, sh, rw, rb, seq)
    top_w, slot_of_pair, src_tok, tile_expert, n_used = _route(logits)
    y = _expert_ffn(h, src_tok, tile_expert, n_used, w1, b1, w2, b2)
    return _moe_combine(y, slot_of_pair, top_w, x, gate, seq)


SEQ_BLOCK = 512
HI = lax.Precision.HIGHEST


def _dot_bf16(a, b, dims=(((1,), (0,)), ((), ()))):
    return lax.dot_general(a.astype(BF16), b.astype(BF16), dims, preferred_element_type=F32)


def _tril_ones(n):
    row = lax.broadcasted_iota(jnp.int32, (n, n), 0)
    col = lax.broadcasted_iota(jnp.int32, (n, n), 1)
    return row >= col


def _row_norm(x, center):
    if center:
        x = x - jnp.mean(x, axis=-1, keepdims=True)
    return x * lax.rsqrt(jnp.mean(x * x, axis=-1, keepdims=True) + EPS)


def _linattn_chunk(q, k, v, la, state_ref):
    causal = _tril_ones(CHUNK)
    b = jnp.dot(causal.astype(F32), la, precision=HI, preferred_element_type=F32)
    b_last = b[CHUNK - 1:CHUNK, :]
    q_in = q * jnp.exp(b)
    k_in = k * jnp.exp(-b)
    k_st = k * jnp.exp(b_last - b)
    scores = jnp.where(causal, _dot_bf16(q_in, k_in, (((1,), (1,)), ((), ()))), 0.0)
    state = state_ref[...]
    o = _dot_bf16(scores, v) + _dot_bf16(q_in, state)
    dsum = lax.dot_general(la, jnp.ones((CHUNK, LANES), F32), (((0,), (0,)), ((), ())),
                           precision=HI, preferred_element_type=F32)
    decay = jnp.exp(dsum)
    dv = v.shape[1]
    decay_full = jnp.concatenate([decay] * (dv // LANES), axis=1)
    state_ref[...] = decay_full * state + _dot_bf16(k_st, v, (((0,), (0,)), ((), ())))
    return o


def _ret_kernel(q_ref, k_ref, v_ref, g_ref, cos_ref, sin_ref, lg_ref, gn_ref, o_ref, state_ref, *, k_scale):
    @pl.when(pl.program_id(2) == 0)
    def _():
        state_ref[...] = jnp.zeros_like(state_ref)

    half = q_ref.shape[1] // 2
    la = jnp.broadcast_to(lg_ref[...], (CHUNK, q_ref.shape[1]))

    def rope(t, cos, sin):
        t1, t2 = t[:, :half], t[:, half:]
        return jnp.concatenate([t1 * cos - t2 * sin, t1 * sin + t2 * cos], axis=1)

    def body(c, carry):
        sl = pl.ds(pl.multiple_of(c * CHUNK, CHUNK), CHUNK)
        cos, sin = cos_ref[sl, :], sin_ref[sl, :]
        q = rope(q_ref[sl, :], cos, sin)
        k = rope(k_ref[sl, :], cos, sin) * k_scale
        o = _linattn_chunk(q, k, v_ref[sl, :], la, state_ref)
        g = g_ref[sl, :]
        o_ref[sl, :] = _row_norm(o, True) * gn_ref[...] * (g * jax.nn.sigmoid(g))
        return carry

    lax.fori_loop(0, q_ref.shape[0] // CHUNK, body, 0)


def _retention_mixer(proj, gn_g, bsz, seq):
    n_heads, dv = gn_g.shape
    dk = dv // 2
    lb = SEQ_BLOCK
    p3 = proj.reshape(bsz, seq, proj.shape[-1])
    half = dk // 2
    inv = ROPE_BASE ** (-jnp.arange(half, dtype=F32) / half)
    ang = jnp.arange(seq, dtype=F32)[:, None] * inv[None, :]
    cos, sin = jnp.cos(ang), jnp.sin(ang)
    log_gamma = jnp.log1p(-jnp.exp2(-5.0 - jnp.arange(n_heads, dtype=F32)))
    lg = jnp.broadcast_to(log_gamma[:, None, None], (n_heads, 1, dk))
    out = pl.pallas_call(
        functools.partial(_ret_kernel, k_scale=dk ** -0.5),
        grid=(bsz, n_heads, seq // lb),
        in_specs=[pl.BlockSpec((None, lb, dk), lambda b, h, s: (b, s, h)),
                  pl.BlockSpec((None, lb, dk), lambda b, h, s: (b, s, n_heads + h)),
                  pl.BlockSpec((None, lb, dv), lambda b, h, s: (b, s, n_heads + h)),
                  pl.BlockSpec((None, lb, dv), lambda b, h, s: (b, s, 2 * n_heads + h)),
                  pl.BlockSpec((lb, half), lambda b, h, s: (s, 0)),
                  pl.BlockSpec((lb, half), lambda b, h, s: (s, 0)),
                  pl.BlockSpec((None, 1, dk), lambda b, h, s: (h, 0, 0)),
                  pl.BlockSpec((None, 1, dv), lambda b, h, s: (h, 0, 0))],
        out_specs=pl.BlockSpec((None, lb, dv), lambda b, h, s: (b, s, h)),
        out_shape=jax.ShapeDtypeStruct((bsz, seq, n_heads * dv), F32),
        scratch_shapes=[pltpu.VMEM((dk, dv), F32)],
        compiler_params=_cparams("arbitrary", "arbitrary", "arbitrary"),
        name="retention",
    )(p3, p3, p3, p3, cos, sin, lg, gn_g.reshape(n_heads, 1, dv))
    return out.reshape(bsz * seq, n_heads * dv)


def _gla_kernel(q_ref, k_ref, v_ref, r_ref, glow_ref, wg_ref, bg_ref, gn_ref, o_ref, state_ref, *, q_scale):
    @pl.when(pl.program_id(2) == 0)
    def _():
        state_ref[...] = jnp.zeros_like(state_ref)

    def body(c, carry):
        sl = pl.ds(pl.multiple_of(c * CHUNK, CHUNK), CHUNK)
        z = jnp.dot(glow_ref[sl, :], wg_ref[...], precision=HI, preferred_element_type=F32) + bg_ref[...]
        la = jax.nn.log_sigmoid(z) / GLA_GATE_NORM
        o = _linattn_chunk(q_ref[sl, :] * q_scale, k_ref[sl, :], v_ref[sl, :], la, state_ref)
        r = r_ref[sl, :]
        o_ref[sl, :] = _row_norm(o, False) * gn_ref[...] * (r * jax.nn.sigmoid(r))
        return carry

    lax.fori_loop(0, q_ref.shape[0] // CHUNK, body, 0)


def _gla_proj_weight(w_in, dq, dvv):
    q_k_v = w_in[:, :2 * dq + dvv]
    g_low = w_in[:, 2 * dq + dvv:2 * dq + dvv + GLA_GATE_RANK]
    r = w_in[:, 2 * dq + dvv + GLA_GATE_RANK:]
    pad = jnp.zeros((w_in.shape[0], LANES - GLA_GATE_RANK), w_in.dtype)
    return jnp.concatenate([q_k_v, r, g_low, pad], axis=1)


def _gla_mixer(proj, w_gate, b_gate, gn_g, bsz, seq):
    n_heads, dv = gn_g.shape
    dk = dv // 2
    dq, dvv = n_heads * dk, n_heads * dv
    lb = SEQ_BLOCK
    p3 = proj.reshape(bsz, seq, proj.shape[-1])
    wg = jnp.pad(w_gate, ((0, LANES - GLA_GATE_RANK), (0, 0)))
    out = pl.pallas_call(
        functools.partial(_gla_kernel, q_scale=dk ** -0.5),
        grid=(bsz, n_heads, seq // lb),
        in_specs=[pl.BlockSpec((None, lb, dk), lambda b, h, s: (b, s, h)),
                  pl.BlockSpec((None, lb, dk), lambda b, h, s: (b, s, n_heads + h)),
                  pl.BlockSpec((None, lb, dv), lambda b, h, s: (b, s, n_heads + h)),
                  pl.BlockSpec((None, lb, dv), lambda b, h, s: (b, s, 2 * n_heads + h)),
                  pl.BlockSpec((None, lb, LANES), lambda b, h, s: (b, s, (2 * dq + 2 * dvv) // LANES)),
                  pl.BlockSpec((LANES, dk), lambda b, h, s: (0, h)),
                  pl.BlockSpec((1, dk), lambda b, h, s: (0, h)),
                  pl.BlockSpec((None, 1, dv), lambda b, h, s: (h, 0, 0))],
        out_specs=pl.BlockSpec((None, lb, dv), lambda b, h, s: (b, s, h)),
        out_shape=jax.ShapeDtypeStruct((bsz, seq, dvv), F32),
        scratch_shapes=[pltpu.VMEM((dk, dv), F32)],
        compiler_params=_cparams("arbitrary", "arbitrary", "arbitrary"),
        name="gla",
    )(p3, p3, p3, p3, p3, wg, b_gate.reshape(1, dq), gn_g.reshape(n_heads, 1, dv))
    return out.reshape(bsz * seq, dvv)


ATT_WINDOW = (ATT_LEFT_CHUNKS + 2) * CHUNK


def _att_kernel(q_ref, k_ref, v_ref, bias_ref, o_ref, kpad, vpad, *, scale, dh):
    seq = q_ref.shape[0]
    front = ATT_WINDOW - CHUNK
    kpad[0:front, :] = jnp.zeros((front, kpad.shape[1]), F32)
    vpad[0:front, :] = jnp.zeros((front, vpad.shape[1]), F32)
    kpad[front:front + seq, :] = k_ref[...]
    vpad[front:front + seq, :] = v_ref[...]
    lane = lax.broadcasted_iota(jnp.int32, (CHUNK, q_ref.shape[1]), 1)
    jcol = lax.broadcasted_iota(jnp.int32, (CHUNK, ATT_WINDOW), 1)
    n_pair = q_ref.shape[1] // dh

    def body(c, carry):
        start = pl.multiple_of(c * CHUNK, CHUNK)
        q = q_ref[pl.ds(start, CHUNK), :] * scale
        kb = kpad[pl.ds(start, ATT_WINDOW), :].astype(BF16)
        vb = vpad[pl.ds(start, ATT_WINDOW), :].astype(BF16)
        valid = jnp.logical_and(jcol >= CHUNK, jcol >= front - start)
        out = jnp.zeros((CHUNK, q_ref.shape[1]), F32)
        for h in range(n_pair):
            in_head = jnp.logical_and(lane >= h * dh, lane < (h + 1) * dh)
            qh = jnp.where(in_head, q, 0.0).astype(BF16)
            s = lax.dot_general(qh, kb, (((1,), (1,)), ((), ())), preferred_element_type=F32) + bias_ref[h]
            s = jnp.where(valid, s, -jnp.inf)
            p = jnp.exp(s - jnp.max(s, axis=-1, keepdims=True))
            p = p / jnp.sum(p, axis=-1, keepdims=True)
            oh = jnp.dot(p.astype(BF16), vb, preferred_element_type=F32)
            out = jnp.where(in_head, oh, out)
        o_ref[pl.ds(start, CHUNK), :] = out
        return carry

    lax.fori_loop(0, seq // CHUNK, body, 0)


def _attention_mixer(proj, rel_bias, bsz, seq):
    d = proj.shape[-1] // 3
    n_heads = rel_bias.shape[0]
    dh = d // n_heads
    n_pair = LANES // dh
    groups = n_heads // n_pair
    front = ATT_WINDOW - CHUNK
    qi = jnp.arange(CHUNK)[:, None]
    kj = jnp.arange(ATT_WINDOW)[None, :]
    idx = jnp.clip(qi + front - kj, -(CHUNK - 1), ATT_MAX_REL) + (CHUNK - 1)
    bias = rel_bias[:, idx].astype(F32)
    p3 = proj.reshape(bsz, seq, 3 * d)
    out = pl.pallas_call(
        functools.partial(_att_kernel, scale=dh ** -0.5, dh=dh),
        grid=(bsz, groups),
        in_specs=[pl.BlockSpec((None, seq, LANES), lambda b, g: (b, 0, g)),
                  pl.BlockSpec((None, seq, LANES), lambda b, g: (b, 0, groups + g)),
                  pl.BlockSpec((None, seq, LANES), lambda b, g: (b, 0, 2 * groups + g)),
                  pl.BlockSpec((n_pair, CHUNK, ATT_WINDOW), lambda b, g: (g, 0, 0))],
        out_specs=pl.BlockSpec((None, seq, LANES), lambda b, g: (b, 0, g)),
        out_shape=jax.ShapeDtypeStruct((bsz, seq, d), F32),
        scratch_shapes=[pltpu.VMEM((front + seq, LANES), F32), pltpu.VMEM((front + seq, LANES), F32)],
        compiler_params=_cparams("arbitrary", "arbitrary"),
        name="chunk_attention",
    )(p3, p3, p3, bias)
    return out.reshape(bsz * seq, d)


MLSTM_PRE_TILE = 256
CONV_PAD = 8


def _mlstm_pre_kernel(xm_ref, cw_ref, cb_ref, wq_ref, wk_ref, wv_ref, wg_ref, bg_ref,
                      q_ref, k_ref, v_ref, xc_ref, g_ref, xpad, *, per, k_scale):
    i = pl.program_id(0)
    tm, inner = xm_ref.shape

    @pl.when(i % per == 0)
    def _():
        xpad[0:CONV_PAD, :] = jnp.zeros((CONV_PAD, inner), F32)

    @pl.when(i % per != 0)
    def _():
        xpad[0:CONV_PAD, :] = xpad[tm:tm + CONV_PAD, :]

    xm = xm_ref[...]
    xpad[CONV_PAD:CONV_PAD + tm, :] = xm
    acc = jnp.broadcast_to(cb_ref[...], (tm, inner))
    for j in range(MLSTM_CONV):
        off = CONV_PAD - (MLSTM_CONV - 1) + j
        acc = acc + cw_ref[j:j + 1, :] * xpad[off:off + tm, :]
    xc = acc * jax.nn.sigmoid(acc)
    xc_ref[...] = xc

    def blockdiag(t, w_ref):
        tb = t.astype(BF16)
        return jnp.concatenate(
            [jnp.dot(tb[:, g * LANES:(g + 1) * LANES], w_ref[g], preferred_element_type=F32)
             for g in range(inner // LANES)], axis=1)

    q = blockdiag(xc, wq_ref)
    k = blockdiag(xc, wk_ref) * k_scale
    v = blockdiag(xm, wv_ref)
    q_ref[...] = q
    k_ref[...] = k
    v_ref[...] = v
    g_ref[...] = (_dot_bf16(q, wg_ref[0:inner, :]) + _dot_bf16(k, wg_ref[inner:2 * inner, :])
                  + _dot_bf16(v, wg_ref[2 * inner:3 * inner, :]) + bg_ref[...])


def _blockdiag_tiles(w):
    nb, c, _ = w.shape
    per = LANES // c
    wt = w.reshape(nb // per, per, c, c)
    t = jnp.einsum('gpcd,pq->gpcqd', wt, jnp.eye(per, dtype=w.dtype))
    return t.reshape(nb // per, LANES, LANES).astype(BF16)


def _mlstm_rec_kernel(q_ref, k_ref, v_ref, g_ref, xc_ref, z_ref, gn_ref, skip_ref, o_ref,
                      c_ref, n_ref, m_ref, *, n_heads):
    h = pl.program_id(1)

    @pl.when(pl.program_id(2) == 0)
    def _():
        c_ref[...] = jnp.zeros_like(c_ref)
        n_ref[...] = jnp.zeros_like(n_ref)
        m_ref[...] = jnp.zeros_like(m_ref)

    causal = _tril_ones(CHUNK)
    lane = lax.broadcasted_iota(jnp.int32, (CHUNK, LANES), 1)
    ones = jnp.ones((CHUNK, LANES), F32)

    def col(t, sel):
        return jnp.sum(jnp.where(sel, t, 0.0), axis=1, keepdims=True)

    def row_bcast(t, sel):
        return lax.dot_general(ones, jnp.where(sel, t, 0.0), (((1,), (1,)), ((), ())),
                               precision=HI, preferred_element_type=F32)

    def body(c, carry):
        sl = pl.ds(pl.multiple_of(c * CHUNK, CHUNK), CHUNK)
        gates = g_ref[sl, :]
        bcum = jnp.dot(causal.astype(F32), jax.nn.log_sigmoid(gates), precision=HI,
                       preferred_element_type=F32)
        sel_i, sel_f = lane == h, lane == n_heads + h
        i_col, b_col = col(gates, sel_i), col(bcum, sel_f)
        m_prev = m_ref[:, 0:1]
        log_intra = jnp.where(causal, b_col - row_bcast(bcum, sel_f) + row_bcast(gates, sel_i), -jnp.inf)
        log_inter = b_col + m_prev
        m_row = jnp.maximum(log_inter, jnp.max(log_intra, axis=1, keepdims=True))
        w_intra = jnp.exp(log_intra - m_row)
        w_inter = jnp.exp(log_inter - m_row)
        q, k, v = q_ref[sl, :], k_ref[sl, :], v_ref[sl, :]
        s = _dot_bf16(q, k, (((1,), (1,)), ((), ()))) * w_intra
        cmat = c_ref[...]
        nvec = n_ref[...]
        num = w_inter * _dot_bf16(q, cmat) + _dot_bf16(s, v)
        den = w_inter * jnp.sum(q * nvec, axis=1, keepdims=True) + jnp.sum(s, axis=1, keepdims=True)
        hc = num / jnp.maximum(jnp.abs(den), jnp.exp(-m_row))
        b_last = b_col[CHUNK - 1:CHUNK, :]
        log_keep = b_last + m_prev
        log_w = b_last - b_col + i_col
        m_new = jnp.maximum(log_keep, jnp.max(log_w, axis=0, keepdims=True))
        keep = jnp.exp(log_keep - m_new)
        wk = jnp.exp(log_w - m_new) * k
        c_ref[...] = keep * cmat + _dot_bf16(wk, v, (((0,), (0,)), ((), ())))
        n_ref[...] = keep * nvec + jnp.sum(wk, axis=0, keepdims=True)
        m_ref[...] = jnp.broadcast_to(m_new, m_ref.shape)
        z = z_ref[sl, :]
        o_ref[sl, :] = (_row_norm(hc, True) * gn_ref[...] + skip_ref[...] * xc_ref[sl, :]) * (z * jax.nn.sigmoid(z))
        return carry

    lax.fori_loop(0, q_ref.shape[0] // CHUNK, body, 0)


def _mlstm_mixer(proj, conv_w, conv_b, w_q, w_k, w_v, w_gates, b_gates, gn_g, skip, bsz, seq):
    t = proj.shape[0]
    inner = proj.shape[1] // 2
    n_heads, dh = gn_g.shape
    tm = MLSTM_PRE_TILE
    n_g = w_gates.shape[1]
    wg = jnp.pad(w_gates, ((0, 0), (0, LANES - n_g))).astype(BF16)
    bg = jnp.pad(b_gates.reshape(1, n_g), ((0, 0), (0, LANES - n_g)))
    tile_spec = pl.BlockSpec((tm, inner), lambda i: (i, 0))
    bd_spec = pl.BlockSpec((inner // LANES, LANES, LANES), lambda i: (0, 0, 0))
    q, k, v, xc, gates = pl.pallas_call(
        functools.partial(_mlstm_pre_kernel, per=seq // tm, k_scale=dh ** -0.5),
        grid=(t // tm,),
        in_specs=[tile_spec,
                  pl.BlockSpec((MLSTM_CONV, inner), lambda i: (0, 0)),
                  pl.BlockSpec((1, inner), lambda i: (0, 0)),
                  bd_spec, bd_spec, bd_spec,
                  pl.BlockSpec((3 * inner, LANES), lambda i: (0, 0)),
                  pl.BlockSpec((1, LANES), lambda i: (0, 0))],
        out_specs=[tile_spec, tile_spec, tile_spec, tile_spec, pl.BlockSpec((tm, LANES), lambda i: (i, 0))],
        out_shape=[jax.ShapeDtypeStruct((t, inner), F32)] * 4 + [jax.ShapeDtypeStruct((t, LANES), F32)],
        scratch_shapes=[pltpu.VMEM((tm + CONV_PAD, inner), F32)],
        compiler_params=_cparams("arbitrary"),
        name="mlstm_pre",
    )(proj, conv_w, conv_b.reshape(1, inner), _blockdiag_tiles(w_q), _blockdiag_tiles(w_k),
      _blockdiag_tiles(w_v), wg, bg)

    lb = SEQ_BLOCK
    r3 = lambda a: a.reshape(bsz, seq, a.shape[-1])
    head_spec = pl.BlockSpec((None, lb, dh), lambda b, h, s: (b, s, h))
    out = pl.pallas_call(
        functools.partial(_mlstm_rec_kernel, n_heads=n_heads),
        grid=(bsz, n_heads, seq // lb),
        in_specs=[head_spec, head_spec, head_spec,
                  pl.BlockSpec((None, lb, LANES), lambda b, h, s: (b, s, 0)),
                  head_spec,
                  pl.BlockSpec((None, lb, dh), lambda b, h, s: (b, s, n_heads + h)),
                  pl.BlockSpec((None, 1, dh), lambda b, h, s: (h, 0, 0)),
                  pl.BlockSpec((1, dh), lambda b, h, s: (0, h))],
        out_specs=head_spec,
        out_shape=jax.ShapeDtypeStruct((bsz, seq, inner), F32),
        scratch_shapes=[pltpu.VMEM((dh, dh), F32), pltpu.VMEM((1, dh), F32), pltpu.VMEM((1, LANES), F32)],
        compiler_params=_cparams("arbitrary", "arbitrary", "arbitrary"),
        name="mlstm_rec",
    )(r3(q), r3(k), r3(v), r3(gates), r3(xc), r3(proj), gn_g.reshape(n_heads, 1, dh), skip.reshape(1, inner))
    return out.reshape(t, inner)


def _head_norm(x, g, center):
    if center:
        x = x - jnp.mean(x, axis=-1, keepdims=True)
    y = x * lax.rsqrt(jnp.mean(x * x, axis=-1, keepdims=True) + EPS)
    return y * g[None, :, None, :]


def _to_heads(t, n_heads):
    b, s, _ = t.shape
    return t.reshape(b, s, n_heads, -1).transpose(0, 2, 1, 3)


def _from_heads(t):
    b, h, s, d = t.shape
    return t.transpose(0, 2, 1, 3).reshape(b, s, h * d)


def _rope(t, pos):
    half = t.shape[-1] // 2
    inv = ROPE_BASE ** (-jnp.arange(half, dtype=F32) / half)
    ang = pos.astype(F32)[:, None] * inv[None, :]
    cos = jnp.cos(ang)[None, :, None, :]
    sin = jnp.sin(ang)[None, :, None, :]
    t1, t2 = t[..., :half], t[..., half:]
    return jnp.concatenate([t1 * cos - t2 * sin, t1 * sin + t2 * cos], axis=-1)


def _chunk_linear_attention(q, k, v, log_a):
    b_, h_, s_, _ = q.shape
    dv = v.shape[-1]
    nc = s_ // CHUNK

    def chunks(t):
        return t.reshape(t.shape[0], t.shape[1], nc, CHUNK, t.shape[3])

    qc, kc, vc = chunks(q), chunks(k), chunks(v)
    b = jnp.cumsum(chunks(log_a.astype(F32)), axis=3)
    b_last = b[:, :, :, -1:, :]
    q_in = qc * jnp.exp(b)
    k_in = kc * jnp.exp(-b)
    k_st = kc * jnp.exp(b_last - b)
    causal = jnp.tril(jnp.ones((CHUNK, CHUNK), dtype=bool))
    scores = jnp.where(causal, jnp.einsum('bhcld,bhcmd->bhclm', q_in, k_in), 0.0)
    o_intra = jnp.einsum('bhclm,bhcme->bhcle', scores, vc)
    decay = jnp.exp(b_last[:, :, :, 0, :])

    def step(state, xs):
        q_c, k_c, v_c, d_c = xs
        o = jnp.einsum('bhld,bhde->bhle', q_c, state)
        state = d_c[..., None] * state + jnp.einsum('bhld,bhle->bhde', k_c, v_c)
        return state, o

    mv = lambda t: jnp.moveaxis(t, 2, 0)
    init = jnp.zeros((b_, h_, q.shape[-1], dv), F32)
    _, o_inter = lax.scan(step, init, (mv(q_in), mv(k_st), mv(vc), mv(decay)))
    o = o_intra + jnp.moveaxis(o_inter, 0, 2)
    return o.reshape(b_, h_, s_, dv)


def _retention_core(proj, gn_g, bsz, seq):
    d = gn_g.shape[0] * gn_g.shape[1] // 2
    dk = d // RET_HEADS
    p = proj.reshape(bsz, seq, -1)
    q, k, v, g = jnp.split(p, [d, 2 * d, 4 * d], axis=-1)
    pos = jnp.arange(seq)
    q = _rope(q.reshape(bsz, seq, RET_HEADS, dk), pos)
    k = _rope(k.reshape(bsz, seq, RET_HEADS, dk), pos) * dk ** -0.5
    q, k = q.transpose(0, 2, 1, 3), k.transpose(0, 2, 1, 3)
    v = _to_heads(v, RET_HEADS)
    log_gamma = jnp.log1p(-jnp.exp2(-5.0 - jnp.arange(RET_HEADS, dtype=F32)))
    log_a = jnp.broadcast_to(log_gamma[None, :, None, None], (1, RET_HEADS, seq, 1))
    o = _head_norm(_chunk_linear_attention(q, k, v, log_a), gn_g, center=True)
    return (_from_heads(o) * jax.nn.silu(g)).reshape(bsz * seq, -1)


def _attention_core(proj, rel_bias, bsz, seq):
    d = proj.shape[-1] // 3
    dh = d // ATT_HEADS
    left = ATT_LEFT_CHUNKS * CHUNK
    band = left + CHUNK
    nc = seq // CHUNK
    p = proj.reshape(bsz, seq, -1)
    q, k, v = jnp.split(p, 3, axis=-1)
    q = q.reshape(bsz, seq, ATT_HEADS, dh) * dh ** -0.5
    pad = ((0, 0), (left, 0), (0, 0), (0, 0))
    k = jnp.pad(k.reshape(bsz, seq, ATT_HEADS, dh), pad)
    v = jnp.pad(v.reshape(bsz, seq, ATT_HEADS, dh), pad)
    qi = jnp.arange(CHUNK)[:, None]
    kj = jnp.arange(band)[None, :]
    dist = qi - kj + left
    idx = jnp.clip(dist, -(CHUNK - 1), ATT_MAX_REL) + (CHUNK - 1)
    bias = rel_bias[:, idx].astype(F32)
    qc = jnp.moveaxis(q.reshape(bsz, nc, CHUNK, ATT_HEADS, dh), 1, 0)

    def one_chunk(args):
        ci, q_c = args
        start = ci * CHUNK
        k_b = lax.dynamic_slice_in_dim(k, start, band, axis=1)
        v_b = lax.dynamic_slice_in_dim(v, start, band, axis=1)
        s = jnp.einsum('blhd,bmhd->bhlm', q_c, k_b).astype(F32) + bias
        valid = kj >= left - start
        pr = jax.nn.softmax(jnp.where(valid, s, -jnp.inf), axis=-1)
        return jnp.einsum('bhlm,bmhd->blhd', pr, v_b)

    o = lax.map(one_chunk, (jnp.arange(nc), qc))
    return jnp.moveaxis(o, 0, 1).reshape(bsz * seq, d)


def _gla_core(proj, w_gate, b_gate, gn_g, bsz, seq):
    dvv = gn_g.shape[0] * gn_g.shape[1]
    dq = dvv // 2
    dk = dq // GLA_HEADS
    p = proj.reshape(bsz, seq, -1)
    q, k, v, g_low, r = jnp.split(p, [dq, 2 * dq, 2 * dq + dvv, 2 * dq + dvv + GLA_GATE_RANK], axis=-1)
    log_a = jax.nn.log_sigmoid(
        jnp.dot(g_low, w_gate, precision=lax.Precision.HIGHEST) + b_gate) / GLA_GATE_NORM
    q = _to_heads(q, GLA_HEADS) * dk ** -0.5
    o = _chunk_linear_attention(q, _to_heads(k, GLA_HEADS), _to_heads(v, GLA_HEADS),
                                _to_heads(log_a, GLA_HEADS))
    o = _head_norm(o, gn_g, center=False)
    return (_from_heads(o) * jax.nn.silu(r)).reshape(bsz * seq, -1)


def _causal_depthwise_conv(x, w, b):
    ch = x.shape[-1]
    y = lax.conv_general_dilated(x, w[:, None, :], window_strides=(1,),
                                 padding=[(MLSTM_CONV - 1, 0)],
                                 dimension_numbers=('NWC', 'WIO', 'NWC'),
                                 feature_group_count=ch)
    return y + b


def _mlstm_chunkwise(q, k, v, ig, log_f):
    b_, h_, s_, dh = q.shape
    nc = s_ // CHUNK

    def chunks(t):
        return jnp.moveaxis(t.reshape((b_, h_, nc, CHUNK) + t.shape[3:]), 2, 0)

    causal = jnp.tril(jnp.ones((CHUNK, CHUNK), dtype=bool))

    def step(carry, xs):
        cmat, nvec, m = carry
        q_c, k_c, v_c, i_c, f_c = xs
        b = jnp.cumsum(f_c, axis=-1)
        log_intra = jnp.where(causal, b[..., :, None] - b[..., None, :] + i_c[..., None, :], -jnp.inf)
        log_inter = b + m[..., None]
        m_row = jnp.maximum(log_inter, jnp.max(log_intra, axis=-1))
        w_intra = jnp.exp(log_intra - m_row[..., None])
        w_inter = jnp.exp(log_inter - m_row)
        s = jnp.einsum('bhld,bhmd->bhlm', q_c, k_c) * w_intra
        num = w_inter[..., None] * jnp.einsum('bhld,bhde->bhle', q_c, cmat) + jnp.einsum('bhlm,bhme->bhle', s, v_c)
        den = w_inter * jnp.einsum('bhld,bhd->bhl', q_c, nvec) + jnp.sum(s, axis=-1)
        h_c = num / jnp.maximum(jnp.abs(den), jnp.exp(-m_row))[..., None]
        b_last = b[..., -1]
        log_keep = b_last + m
        log_w = b_last[..., None] - b + i_c
        m_new = jnp.maximum(log_keep, jnp.max(log_w, axis=-1))
        keep = jnp.exp(log_keep - m_new)
        w = jnp.exp(log_w - m_new[..., None])
        cmat = keep[..., None, None] * cmat + jnp.einsum('bhl,bhld,bhle->bhde', w, k_c, v_c)
        nvec = keep[..., None] * nvec + jnp.einsum('bhl,bhld->bhd', w, k_c)
        return (cmat, nvec, m_new), h_c

    init = (jnp.zeros((b_, h_, dh, dh), F32), jnp.zeros((b_, h_, dh), F32), jnp.zeros((b_, h_), F32))
    _, hs = lax.scan(step, init, (chunks(q), chunks(k), chunks(v), chunks(ig), chunks(log_f)))
    return jnp.moveaxis(hs, 0, 2).reshape(b_, h_, s_, dh)


def _mlstm_core(proj, conv_w, conv_b, w_q, w_k, w_v, w_gates, b_gates, gn_g, skip, bsz, seq):
    inner = proj.shape[-1] // 2
    dh = inner // MLSTM_HEADS
    nb, blk = w_q.shape[0], w_q.shape[1]
    p = proj.reshape(bsz, seq, -1)
    xm, z = jnp.split(p, 2, axis=-1)
    xc = jax.nn.silu(_causal_depthwise_conv(xm, conv_w, conv_b))

    def blockdiag(t, w):
        t = t.reshape(bsz, seq, nb, blk)
        return jnp.einsum('bsnc,ncd->bsnd', t, w).reshape(bsz, seq, inner)

    q = blockdiag(xc, w_q)
    k = blockdiag(xc, w_k) * dh ** -0.5
    v = blockdiag(xm, w_v)
    gates = jnp.concatenate([q, k, v], axis=-1) @ w_gates + b_gates
    ig, fg = jnp.split(gates.transpose(0, 2, 1), 2, axis=1)
    hh = _mlstm_chunkwise(_to_heads(q, MLSTM_HEADS), _to_heads(k, MLSTM_HEADS), _to_heads(v, MLSTM_HEADS),
                          ig, jax.nn.log_sigmoid(fg))
    hh = _head_norm(hh, gn_g, center=True)
    out = (_from_heads(hh) + skip * xc) * jax.nn.silu(z)
    return out.reshape(bsz * seq, inner)


def kernel(x, c, ada_w, ada_b, norm_mix_g, norm_ffn_g, norm_final_g, ret_w_in, ret_gn_g, ret_w_out, att_w_in, att_rel_bias, att_w_out, gla_w_in, gla_w_gate, gla_b_gate, gla_gn_g, gla_w_out, mlstm_w_in, mlstm_conv_w, mlstm_conv_b, mlstm_w_q, mlstm_w_k, mlstm_w_v, mlstm_w_gates, mlstm_b_gates, mlstm_gn_g, mlstm_skip, mlstm_w_out, router_w, router_b, moe_w1, moe_b1, moe_w2, moe_b2):
    bsz, seq, d = x.shape
    depth = ada_w.shape[0]
    xt = x.reshape(bsz * seq, d)
    mod = _ada_mod(c, ada_w, ada_b)
    w1_bf = moe_w1.astype(BF16)
    w2_bf = moe_w2.astype(BF16)
    for i in range(depth):
        kind, j = i % 4, i // 4
        sh_a, sc_a, g_a, sh_f, sc_f, g_f = jnp.split(mod[i], 6, axis=-1)
        if kind == 0:
            proj = _norm_mm(xt, norm_mix_g[i], sc_a, sh_a, ret_w_in[j].astype(BF16), seq)
            y = _retention_mixer(proj, ret_gn_g[j], bsz, seq)
            w_out = ret_w_out[j]
        elif kind == 1:
            proj = _norm_mm(xt, norm_mix_g[i], sc_a, sh_a, att_w_in[j].astype(BF16), seq)
            y = _attention_mixer(proj, att_rel_bias[j], bsz, seq)
            w_out = att_w_out[j]
        elif kind == 2:
            n_heads, dv = gla_gn_g[j].shape
            w_gla = _gla_proj_weight(gla_w_in[j], n_heads * dv // 2, n_heads * dv).astype(BF16)
            proj = _norm_mm(xt, norm_mix_g[i], sc_a, sh_a, w_gla, seq, tn=w_gla.shape[1] // 5)
            y = _gla_mixer(proj, gla_w_gate[j], gla_b_gate[j], gla_gn_g[j], bsz, seq)
            w_out = gla_w_out[j]
        else:
            proj = _norm_mm(xt, norm_mix_g[i], sc_a, sh_a, mlstm_w_in[j].astype(BF16), seq)
            y = _mlstm_mixer(proj, mlstm_conv_w[j], mlstm_conv_b[j], mlstm_w_q[j], mlstm_w_k[j], mlstm_w_v[j],
                             mlstm_w_gates[j], mlstm_b_gates[j], mlstm_gn_g[j], mlstm_skip[j], bsz, seq)
            w_out = mlstm_w_out[j]
        xt = _mm_res(y, w_out.astype(BF16), xt, g_a, seq)
        xt = _moe_layer(xt, norm_ffn_g[i], sc_f, sh_f, g_f, router_w[i], router_b[i],
                        w1_bf[i], moe_b1[i], w2_bf[i], moe_b2[i], seq)
    return _final_norm(xt, norm_final_g).reshape(bsz, seq, d)
```

```python
import functools

import jax
import jax.numpy as jnp
from jax import lax
from jax.experimental import pallas as pl
from jax.experimental.pallas import tpu as pltpu

F32 = jnp.float32
BF16 = jnp.bfloat16
HI = lax.Precision.HIGHEST

CHUNK = 64
EPS = 1e-6
ROPE_BASE = 10000.0
ATT_LEFT_CHUNKS = 8
ATT_MAX_REL = 256
GLA_GATE_RANK = 16
GLA_GATE_NORM = 16.0
MLSTM_CONV = 4
N_EXPERTS = 32
TOP_K = 4
SWIGLU_LIMIT = 7.0
SWIGLU_ALPHA = 1.702

VMEM_LIMIT_BYTES = 56 * 1024 * 1024
LANES = 128
SUBLANES = 8
MOE_TILE = 512
COMBINE_TILE = 256
SEQ_BLOCK = 512
MLSTM_PRE_TILE = 256
CONV_PAD = 8
ATT_QBLOCK = 2 * CHUNK
ATT_LEFT = ATT_LEFT_CHUNKS * CHUNK
ATT_WINDOW = ATT_LEFT + ATT_QBLOCK


def _cparams(*sem):
    return pltpu.CompilerParams(dimension_semantics=sem, vmem_limit_bytes=VMEM_LIMIT_BYTES)


def _dot_bf16(a, b, dims=(((1,), (0,)), ((), ()))):
    return lax.dot_general(a.astype(BF16), b.astype(BF16), dims, preferred_element_type=F32)


def _tril_ones(n):
    row = lax.broadcasted_iota(jnp.int32, (n, n), 0)
    col = lax.broadcasted_iota(jnp.int32, (n, n), 1)
    return row >= col


def _row_norm(x, center):
    if center:
        x = x - jnp.mean(x, axis=-1, keepdims=True)
    return x * lax.rsqrt(jnp.mean(x * x, axis=-1, keepdims=True) + EPS)


def _load_row_tiles(ref, rows):
    return jnp.concatenate([ref[pl.ds(j, rows, stride=SUBLANES), :] for j in range(SUBLANES)], axis=1)


def _store_row_tiles(ref, val):
    rows = val.shape[0]
    for j in range(SUBLANES):
        ref[pl.ds(j, rows, stride=SUBLANES), :] = val[:, j * LANES:(j + 1) * LANES]


def _ada_kernel(c_ref, w_ref, b_ref, o_ref):
    c = c_ref[...]
    cond = c * jax.nn.sigmoid(c)
    o_ref[0] = jnp.dot(cond, w_ref[0], preferred_element_type=F32, precision=HI) + b_ref[0]


def _ada_mod(c, ada_w, ada_b):
    depth, d, n = ada_w.shape
    b = c.shape[0]
    tn = 1536
    return pl.pallas_call(
        _ada_kernel,
        grid=(depth, n // tn),
        in_specs=[pl.BlockSpec((b, d), lambda l, j: (0, 0)),
                  pl.BlockSpec((1, d, tn), lambda l, j: (l, 0, j)),
                  pl.BlockSpec((1, 1, tn), lambda l, j: (l, 0, j))],
        out_specs=pl.BlockSpec((1, b, tn), lambda l, j: (l, 0, j)),
        out_shape=jax.ShapeDtypeStruct((depth, b, n), F32),
        compiler_params=_cparams("arbitrary", "arbitrary"),
        name="ada_mod",
    )(c, ada_w, ada_b.reshape(depth, 1, n))


def _modulated_norm(x, g, sc, sh):
    ms = jnp.mean(x * x, axis=-1, keepdims=True)
    return (x * lax.rsqrt(ms + EPS) * g) * (1.0 + sc) + sh


def _norm_mm_kernel(x_ref, g_ref, sc_ref, sh_ref, w_ref, o_ref, h_scr):
    @pl.when(pl.program_id(1) == 0)
    def _():
        h = _modulated_norm(x_ref[...], g_ref[...], sc_ref[0], sh_ref[0])
        h_scr[...] = h.astype(BF16)

    o_ref[...] = jnp.dot(h_scr[...], w_ref[...], preferred_element_type=F32)


def _norm_mm(x, g, sc, sh, w, seq, tm=1024, tn=512):
    t, d = x.shape
    n = w.shape[1]
    per = seq // tm
    bsz = sc.shape[0]
    return pl.pallas_call(
        _norm_mm_kernel,
        grid=(t // tm, n // tn),
        in_specs=[pl.BlockSpec((tm, d), lambda i, j: (i, 0)),
                  pl.BlockSpec((1, d), lambda i, j: (0, 0)),
                  pl.BlockSpec((1, 1, d), lambda i, j: (i // per, 0, 0)),
                  pl.BlockSpec((1, 1, d), lambda i, j: (i // per, 0, 0)),
                  pl.BlockSpec((d, tn), lambda i, j: (0, j))],
        out_specs=pl.BlockSpec((tm, tn), lambda i, j: (i, j)),
        out_shape=jax.ShapeDtypeStruct((t, n), F32),
        scratch_shapes=[pltpu.VMEM((tm, d), BF16)],
        compiler_params=_cparams("arbitrary", "arbitrary"),
        name="norm_mm",
    )(x, g.reshape(1, d), sc.reshape(bsz, 1, d), sh.reshape(bsz, 1, d), w)


def _mm_res_kernel(y_ref, w_ref, x_ref, gate_ref, o_ref):
    acc = jnp.dot(y_ref[...].astype(BF16), w_ref[...], preferred_element_type=F32)
    o_ref[...] = x_ref[...] + gate_ref[0] * acc


def _mm_res(y, w, x, gate, seq, tm=512):
    t, k = y.shape
    d = w.shape[1]
    per = seq // tm
    bsz = gate.shape[0]
    return pl.pallas_call(
        _mm_res_kernel,
        grid=(t // tm,),
        in_specs=[pl.BlockSpec((tm, k), lambda i: (i, 0)),
                  pl.BlockSpec((k, d), lambda i: (0, 0)),
                  pl.BlockSpec((tm, d), lambda i: (i, 0)),
                  pl.BlockSpec((1, 1, d), lambda i: (i // per, 0, 0))],
        out_specs=pl.BlockSpec((tm, d), lambda i: (i, 0)),
        out_shape=jax.ShapeDtypeStruct((t, d), F32),
        compiler_params=_cparams("arbitrary"),
        name="mm_res",
    )(y, w, x, gate.reshape(bsz, 1, d))


def _final_norm_kernel(x_ref, g_ref, o_ref):
    x = x_ref[...]
    ms = jnp.mean(x * x, axis=-1, keepdims=True)
    o_ref[...] = x * lax.rsqrt(ms + EPS) * g_ref[...]


def _final_norm(x, g, tm=1024):
    t, d = x.shape
    return pl.pallas_call(
        _final_norm_kernel,
        grid=(t // tm,),
        in_specs=[pl.BlockSpec((tm, d), lambda i: (i, 0)),
                  pl.BlockSpec((1, d), lambda i: (0, 0))],
        out_specs=pl.BlockSpec((tm, d), lambda i: (i, 0)),
        out_shape=jax.ShapeDtypeStruct((t, d), F32),
        compiler_params=_cparams("arbitrary"),
        name="final_norm",
    )(x, g.reshape(1, d))


def _norm_router_kernel(x_ref, g_ref, sc_ref, sh_ref, rw_ref, rb_ref, h_ref, lg_ref):
    h = _modulated_norm(x_ref[...], g_ref[...], sc_ref[0], sh_ref[0])
    _store_row_tiles(h_ref, h)
    lg_ref[...] = jnp.dot(h, rw_ref[...], preferred_element_type=F32, precision=HI) + rb_ref[...]


def _norm_router(x, g, sc, sh, rw, rb, seq, tm=512):
    t, d = x.shape
    per = seq // tm
    bsz = sc.shape[0]
    e = rw.shape[1]
    rw_p = jnp.pad(rw, ((0, 0), (0, LANES - e)))
    rb_p = jnp.pad(rb.reshape(1, e), ((0, 0), (0, LANES - e)))
    return pl.pallas_call(
        _norm_router_kernel,
        grid=(t // tm,),
        in_specs=[pl.BlockSpec((tm, d), lambda i: (i, 0)),
                  pl.BlockSpec((1, d), lambda i: (0, 0)),
                  pl.BlockSpec((1, 1, d), lambda i: (i // per, 0, 0)),
                  pl.BlockSpec((1, 1, d), lambda i: (i // per, 0, 0)),
                  pl.BlockSpec((d, LANES), lambda i: (0, 0)),
                  pl.BlockSpec((1, LANES), lambda i: (0, 0))],
        out_specs=[pl.BlockSpec((tm * SUBLANES, LANES), lambda i: (i, 0)),
                   pl.BlockSpec((tm, LANES), lambda i: (i, 0))],
        out_shape=[jax.ShapeDtypeStruct((t * SUBLANES, LANES), F32),
                   jax.ShapeDtypeStruct((t, LANES), F32)],
        compiler_params=_cparams("arbitrary"),
        name="norm_router",
    )(x, g.reshape(1, d), sc.reshape(bsz, 1, d), sh.reshape(bsz, 1, d), rw_p, rb_p)


def _row_tile_copy(src_hbm, src_row8, dst_vmem, dst_row, sem):
    return pltpu.make_async_copy(
        src_hbm.at[pl.ds(pl.multiple_of(src_row8, SUBLANES), SUBLANES)],
        dst_vmem.at[pl.ds(pl.multiple_of(dst_row * SUBLANES, SUBLANES), SUBLANES)], sem)


def _expert_kernel(te_ref, nt_ref, idx_cur_ref, idx_nxt_ref, h_hbm, w1_ref, b1_ref, w2_ref, b2_ref,
                   o_ref, xbuf, sem):
    i = pl.program_id(0)
    n_used = nt_ref[0]
    tm = xbuf.shape[1] // SUBLANES
    slot = i % 2

    def issue(idx_ref, s):
        def body(r8, carry):
            for u in range(SUBLANES):
                r = r8 * SUBLANES + u
                _row_tile_copy(h_hbm, idx_ref[0, 0, r], xbuf.at[s], r, sem.at[s]).start(priority=u % 2)
            return carry
        lax.fori_loop(0, tm // SUBLANES, body, 0)

    @pl.when(jnp.logical_and(i == 0, n_used > 0))
    def _():
        issue(idx_cur_ref, 0)

    @pl.when(i + 1 < n_used)
    def _():
        issue(idx_nxt_ref, 1 - slot)

    @pl.when(i < n_used)
    def _():
        pltpu.make_async_copy(h_hbm.at[pl.ds(0, tm * SUBLANES)], xbuf.at[slot], sem.at[slot]).wait()
        x = _load_row_tiles(xbuf.at[slot], tm).astype(BF16)
        hid = jnp.dot(x, w1_ref[0], preferred_element_type=F32) + b1_ref[0]
        de = hid.shape[1] // 2
        gate = jnp.minimum(hid[:, :de], SWIGLU_LIMIT)
        lin = jnp.clip(hid[:, de:], -SWIGLU_LIMIT, SWIGLU_LIMIT)
        act = gate * jax.nn.sigmoid(SWIGLU_ALPHA * gate) * (lin + 1.0)
        y = jnp.dot(act.astype(BF16), w2_ref[0], preferred_element_type=F32) + b2_ref[0]
        _store_row_tiles(o_ref, y)

    @pl.when(i >= n_used)
    def _():
        o_ref[...] = jnp.zeros_like(o_ref)


def _expert_ffn(h_tiles, src_row8, tile_expert, n_tiles_used, w1, b1, w2, b2, layer):
    p = src_row8.shape[0]
    tm = MOE_TILE
    nt = p // tm
    _, d, dh = w1.shape
    idx3 = src_row8.reshape(nt, 1, tm)
    base = layer * N_EXPERTS
    grid_spec = pltpu.PrefetchScalarGridSpec(
        num_scalar_prefetch=2,
        grid=(nt,),
        in_specs=[
            pl.BlockSpec((1, 1, tm), lambda i, te, nu: (i, 0, 0), memory_space=pltpu.SMEM),
            pl.BlockSpec((1, 1, tm), lambda i, te, nu: (jnp.minimum(i + 1, nt - 1), 0, 0),
                         memory_space=pltpu.SMEM),
            pl.BlockSpec(memory_space=pl.ANY),
            pl.BlockSpec((1, d, dh), lambda i, te, nu: (base + te[i], 0, 0)),
            pl.BlockSpec((1, 1, dh), lambda i, te, nu: (base + te[i], 0, 0)),
            pl.BlockSpec((1, dh // 2, d), lambda i, te, nu: (base + te[i], 0, 0)),
            pl.BlockSpec((1, 1, d), lambda i, te, nu: (base + te[i], 0, 0)),
        ],
        out_specs=pl.BlockSpec((tm * SUBLANES, LANES), lambda i, te, nu: (i, 0)),
        scratch_shapes=[pltpu.VMEM((2, tm * SUBLANES, LANES), F32), pltpu.SemaphoreType.DMA((2,))],
    )
    return pl.pallas_call(
        _expert_kernel,
        grid_spec=grid_spec,
        out_shape=jax.ShapeDtypeStruct((p * SUBLANES, LANES), F32),
        compiler_params=_cparams("arbitrary"),
        name="expert_ffn",
    )(tile_expert, n_tiles_used, idx3, idx3, h_tiles, w1, b1, w2, b2)


def _combine_kernel(idx_cur_ref, idx_nxt_ref, y_hbm, x_ref, w_ref, gate_ref, o_ref, ybuf, sem):
    i = pl.program_id(0)
    n = pl.num_programs(0)
    tc = x_ref.shape[0]
    slot = i % 2

    def issue(idx_ref, s):
        def body(r2, carry):
            for u in range(2):
                r = r2 * 2 + u
                for k in range(TOP_K):
                    _row_tile_copy(y_hbm, idx_ref[0, 0, r * TOP_K + k], ybuf.at[s, k], r,
                                   sem.at[s]).start(priority=k % 2)
            return carry
        lax.fori_loop(0, tc // 2, body, 0)

    @pl.when(i == 0)
    def _():
        issue(idx_cur_ref, 0)

    @pl.when(i + 1 < n)
    def _():
        issue(idx_nxt_ref, 1 - slot)

    for k in range(TOP_K):
        pltpu.make_async_copy(y_hbm.at[pl.ds(0, tc * SUBLANES)], ybuf.at[slot, k], sem.at[slot]).wait()
    w = w_ref[...]
    acc = w[:, 0:1] * _load_row_tiles(ybuf.at[slot, 0], tc)
    for k in range(1, TOP_K):
        acc = acc + w[:, k:k + 1] * _load_row_tiles(ybuf.at[slot, k], tc)
    o_ref[...] = x_ref[...] + gate_ref[0] * acc


def _moe_combine(y_tiles, slot8_of_pair, top_w, x, gate, seq):
    t, d = x.shape
    tc = COMBINE_TILE
    nt = t // tc
    per = seq // tc
    bsz = gate.shape[0]
    idx3 = slot8_of_pair.reshape(nt, 1, tc * TOP_K)
    return pl.pallas_call(
        _combine_kernel,
        grid=(nt,),
        in_specs=[
            pl.BlockSpec((1, 1, tc * TOP_K), lambda i: (i, 0, 0), memory_space=pltpu.SMEM),
            pl.BlockSpec((1, 1, tc * TOP_K), lambda i: (jnp.minimum(i + 1, nt - 1), 0, 0),
                         memory_space=pltpu.SMEM),
            pl.BlockSpec(memory_space=pl.ANY),
            pl.BlockSpec((tc, d), lambda i: (i, 0)),
            pl.BlockSpec((tc, TOP_K), lambda i: (i, 0)),
            pl.BlockSpec((1, 1, d), lambda i: (i // per, 0, 0)),
        ],
        out_specs=pl.BlockSpec((tc, d), lambda i: (i, 0)),
        out_shape=jax.ShapeDtypeStruct((t, d), F32),
        scratch_shapes=[pltpu.VMEM((2, TOP_K, tc * SUBLANES, LANES), F32), pltpu.SemaphoreType.DMA((2,))],
        compiler_params=_cparams("arbitrary"),
        name="moe_combine",
    )(idx3, idx3, y_tiles, x, top_w, gate.reshape(bsz, 1, d))


def _route(logits):
    t = logits.shape[0]
    top_v, top_i = lax.top_k(logits[:, :N_EXPERTS], TOP_K)
    top_w = jax.nn.softmax(top_v, axis=-1)
    flat_e = top_i.reshape(-1).astype(jnp.int32)
    onehot = (flat_e[:, None] == jnp.arange(N_EXPERTS, dtype=jnp.int32)[None, :]).astype(jnp.int32)
    csum = jnp.cumsum(onehot, axis=0)
    counts = csum[-1]
    padded = ((counts + MOE_TILE - 1) // MOE_TILE) * MOE_TILE
    ends = jnp.cumsum(padded)
    starts = ends - padded
    slot_of_pair = jnp.sum(onehot * (csum - 1 + starts[None, :]), axis=1)
    n_fill = N_EXPERTS * MOE_TILE
    fill_end = jnp.cumsum(padded - counts)
    fill_key = jnp.sum((jnp.arange(n_fill, dtype=jnp.int32)[:, None] >= fill_end[None, :]).astype(jnp.int32), axis=1)
    keys = jnp.concatenate([flat_e, fill_key])
    toks = jnp.concatenate([jnp.arange(t * TOP_K, dtype=jnp.int32) // TOP_K, jnp.zeros((n_fill,), jnp.int32)])
    _, src_tok = lax.sort((keys, toks), num_keys=1, is_stable=True)
    n_tiles = (t * TOP_K + n_fill) // MOE_TILE
    tile_start = jnp.arange(n_tiles, dtype=jnp.int32) * MOE_TILE
    tile_expert = jnp.sum((tile_start[:, None] >= ends[None, :]).astype(jnp.int32), axis=1)
    n_used = (ends[-1] // MOE_TILE).astype(jnp.int32)
    last_e = jnp.sum(jnp.where(jnp.arange(n_tiles) == n_used - 1, tile_expert, 0))
    tile_expert = jnp.where(jnp.arange(n_tiles) < n_used, tile_expert, last_e).astype(jnp.int32)
    return (top_w, (slot_of_pair * SUBLANES).astype(jnp.int32), (src_tok * SUBLANES).astype(jnp.int32),
            tile_expert, n_used.reshape(1))


def _moe_layer(x, g, sc, sh, gate, rw, rb, w1, b1, w2, b2, layer, seq):
    h_tiles, logits = _norm_router(x, g, sc, sh, rw, rb, seq)
    top_w, slot8_of_pair, src_row8, tile_expert, n_used = _route(logits)
    y_tiles = _expert_ffn(h_tiles, src_row8, tile_expert, n_used, w1, b1, w2, b2, layer)
    return _moe_combine(y_tiles, slot8_of_pair, top_w, x, gate, seq)


def _linattn_chunk(q, k, v, la, state_ref):
    causal = _tril_ones(CHUNK)
    b = jnp.dot(causal.astype(F32), la, precision=HI, preferred_element_type=F32)
    b_last = b[CHUNK - 1:CHUNK, :]
    q_in = q * jnp.exp(b)
    k_in = k * jnp.exp(-b)
    k_st = k * jnp.exp(b_last - b)
    scores = jnp.where(causal, _dot_bf16(q_in, k_in, (((1,), (1,)), ((), ()))), 0.0)
    state = state_ref[...]
    o = _dot_bf16(scores, v) + _dot_bf16(q_in, state)
    dsum = lax.dot_general(la, jnp.ones((CHUNK, LANES), F32), (((0,), (0,)), ((), ())),
                           precision=HI, preferred_element_type=F32)
    decay = jnp.exp(dsum)
    dv = v.shape[1]
    decay_full = jnp.concatenate([decay] * (dv // LANES), axis=1)
    state_ref[...] = decay_full * state + _dot_bf16(k_st, v, (((0,), (0,)), ((), ())))
    return o


def _ret_kernel(q_ref, k_ref, v_ref, g_ref, cos_ref, sin_ref, lg_ref, gn_ref, o_ref, state_ref, *, k_scale):
    @pl.when(pl.program_id(2) == 0)
    def _():
        state_ref[...] = jnp.zeros_like(state_ref)

    half = q_ref.shape[1] // 2
    la = jnp.broadcast_to(lg_ref[...], (CHUNK, q_ref.shape[1]))

    def rope(t, cos, sin):
        t1, t2 = t[:, :half], t[:, half:]
        return jnp.concatenate([t1 * cos - t2 * sin, t1 * sin + t2 * cos], axis=1)

    def body(c, carry):
        sl = pl.ds(pl.multiple_of(c * CHUNK, CHUNK), CHUNK)
        cos, sin = cos_ref[sl, :], sin_ref[sl, :]
        q = rope(q_ref[sl, :], cos, sin)
        k = rope(k_ref[sl, :], cos, sin) * k_scale
        o = _linattn_chunk(q, k, v_ref[sl, :], la, state_ref)
        g = g_ref[sl, :]
        o_ref[sl, :] = _row_norm(o, True) * gn_ref[...] * (g * jax.nn.sigmoid(g))
        return carry

    lax.fori_loop(0, q_ref.shape[0] // CHUNK, body, 0, unroll=2)


def _retention_mixer(proj, gn_g, bsz, seq):
    n_heads, dv = gn_g.shape
    dk = dv // 2
    lb = SEQ_BLOCK
    p3 = proj.reshape(bsz, seq, proj.shape[-1])
    half = dk // 2
    inv = ROPE_BASE ** (-jnp.arange(half, dtype=F32) / half)
    ang = jnp.arange(seq, dtype=F32)[:, None] * inv[None, :]
    cos, sin = jnp.cos(ang), jnp.sin(ang)
    log_gamma = jnp.log1p(-jnp.exp2(-5.0 - jnp.arange(n_heads, dtype=F32)))
    lg = jnp.broadcast_to(log_gamma[:, None, None], (n_heads, 1, dk))
    out = pl.pallas_call(
        functools.partial(_ret_kernel, k_scale=dk ** -0.5),
        grid=(bsz, n_heads, seq // lb),
        in_specs=[pl.BlockSpec((None, lb, dk), lambda b, h, s: (b, s, h)),
                  pl.BlockSpec((None, lb, dk), lambda b, h, s: (b, s, n_heads + h)),
                  pl.BlockSpec((None, lb, dv), lambda b, h, s: (b, s, n_heads + h)),
                  pl.BlockSpec((None, lb, dv), lambda b, h, s: (b, s, 2 * n_heads + h)),
                  pl.BlockSpec((lb, half), lambda b, h, s: (s, 0)),
                  pl.BlockSpec((lb, half), lambda b, h, s: (s, 0)),
                  pl.BlockSpec((None, 1, dk), lambda b, h, s: (h, 0, 0)),
                  pl.BlockSpec((None, 1, dv), lambda b, h, s: (h, 0, 0))],
        out_specs=pl.BlockSpec((None, lb, dv), lambda b, h, s: (b, s, h)),
        out_shape=jax.ShapeDtypeStruct((bsz, seq, n_heads * dv), F32),
        scratch_shapes=[pltpu.VMEM((dk, dv), F32)],
        compiler_params=_cparams("arbitrary", "arbitrary", "arbitrary"),
        name="retention",
    )(p3, p3, p3, p3, cos, sin, lg, gn_g.reshape(n_heads, 1, dv))
    return out.reshape(bsz * seq, n_heads * dv)


def _gla_kernel(q_ref, k_ref, v_ref, r_ref, glow_ref, wg_ref, bg_ref, gn_ref, o_ref, state_ref, *, q_scale):
    @pl.when(pl.program_id(2) == 0)
    def _():
        state_ref[...] = jnp.zeros_like(state_ref)

    def body(c, carry):
        sl = pl.ds(pl.multiple_of(c * CHUNK, CHUNK), CHUNK)
        z = jnp.dot(glow_ref[sl, :], wg_ref[...], precision=HI, preferred_element_type=F32) + bg_ref[...]
        la = jax.nn.log_sigmoid(z) / GLA_GATE_NORM
        o = _linattn_chunk(q_ref[sl, :] * q_scale, k_ref[sl, :], v_ref[sl, :], la, state_ref)
        r = r_ref[sl, :]
        o_ref[sl, :] = _row_norm(o, False) * gn_ref[...] * (r * jax.nn.sigmoid(r))
        return carry

    lax.fori_loop(0, q_ref.shape[0] // CHUNK, body, 0, unroll=2)


def _gla_proj_weight(w_in, dq, dvv):
    q_k_v = w_in[:, :2 * dq + dvv]
    g_low = w_in[:, 2 * dq + dvv:2 * dq + dvv + GLA_GATE_RANK]
    r = w_in[:, 2 * dq + dvv + GLA_GATE_RANK:]
    pad = jnp.zeros((w_in.shape[0], LANES - GLA_GATE_RANK), w_in.dtype)
    return jnp.concatenate([q_k_v, r, g_low, pad], axis=1)


def _gla_mixer(proj, w_gate, b_gate, gn_g, bsz, seq):
    n_heads, dv = gn_g.shape
    dk = dv // 2
    dq, dvv = n_heads * dk, n_heads * dv
    lb = SEQ_BLOCK
    p3 = proj.reshape(bsz, seq, proj.shape[-1])
    wg = jnp.pad(w_gate, ((0, LANES - GLA_GATE_RANK), (0, 0)))
    out = pl.pallas_call(
        functools.partial(_gla_kernel, q_scale=dk ** -0.5),
        grid=(bsz, n_heads, seq // lb),
        in_specs=[pl.BlockSpec((None, lb, dk), lambda b, h, s: (b, s, h)),
                  pl.BlockSpec((None, lb, dk), lambda b, h, s: (b, s, n_heads + h)),
                  pl.BlockSpec((None, lb, dv), lambda b, h, s: (b, s, n_heads + h)),
                  pl.BlockSpec((None, lb, dv), lambda b, h, s: (b, s, 2 * n_heads + h)),
                  pl.BlockSpec((None, lb, LANES), lambda b, h, s: (b, s, (2 * dq + 2 * dvv) // LANES)),
                  pl.BlockSpec((LANES, dk), lambda b, h, s: (0, h)),
                  pl.BlockSpec((1, dk), lambda b, h, s: (0, h)),
                  pl.BlockSpec((None, 1, dv), lambda b, h, s: (h, 0, 0))],
        out_specs=pl.BlockSpec((None, lb, dv), lambda b, h, s: (b, s, h)),
        out_shape=jax.ShapeDtypeStruct((bsz, seq, dvv), F32),
        scratch_shapes=[pltpu.VMEM((dk, dv), F32)],
        compiler_params=_cparams("arbitrary", "arbitrary", "arbitrary"),
        name="gla",
    )(p3, p3, p3, p3, p3, wg, b_gate.reshape(1, dq), gn_g.reshape(n_heads, 1, dv))
    return out.reshape(bsz * seq, dvv)


def _att_kernel(q_ref, k_ref, v_ref, bias_ref, o_ref, kpad, vpad, *, scale, dh):
    seq, width = q_ref.shape
    n_pair = width // dh
    kpad[0:ATT_LEFT, :] = jnp.zeros((ATT_LEFT, width), F32)
    vpad[0:ATT_LEFT, :] = jnp.zeros((ATT_LEFT, width), F32)
    kpad[ATT_LEFT:ATT_LEFT + seq, :] = k_ref[...]
    vpad[ATT_LEFT:ATT_LEFT + seq, :] = v_ref[...]
    lane = lax.broadcasted_iota(jnp.int32, (ATT_QBLOCK, width), 1)
    jcol = lax.broadcasted_iota(jnp.int32, (n_pair * ATT_QBLOCK, ATT_WINDOW), 1)
    in_head = [jnp.logical_and(lane >= h * dh, lane < (h + 1) * dh) for h in range(n_pair)]

    def body(c, carry):
        start = pl.multiple_of(c * ATT_QBLOCK, ATT_QBLOCK)
        q = q_ref[pl.ds(start, ATT_QBLOCK), :] * scale
        kb = kpad[pl.ds(start, ATT_WINDOW), :].astype(BF16)
        vb = vpad[pl.ds(start, ATT_WINDOW), :].astype(BF16)
        q2 = jnp.concatenate([jnp.where(m, q, 0.0) for m in in_head], axis=0).astype(BF16)
        s = lax.dot_general(q2, kb, (((1,), (1,)), ((), ())), preferred_element_type=F32) + bias_ref[...]
        s = jnp.where(jcol >= ATT_LEFT - start, s, -jnp.inf)
        p = jnp.exp(s - jnp.max(s, axis=-1, keepdims=True))
        p = p / jnp.sum(p, axis=-1, keepdims=True)
        o2 = jnp.dot(p.astype(BF16), vb, preferred_element_type=F32)
        out = o2[0:ATT_QBLOCK, :]
        for h in range(1, n_pair):
            out = jnp.where(in_head[h], o2[h * ATT_QBLOCK:(h + 1) * ATT_QBLOCK, :], out)
        o_ref[pl.ds(start, ATT_QBLOCK), :] = out
        return carry

    lax.fori_loop(0, seq // ATT_QBLOCK, body, 0)


def _attention_mixer(proj, rel_bias, bsz, seq):
    d = proj.shape[-1] // 3
    n_heads = rel_bias.shape[0]
    dh = d // n_heads
    n_pair = LANES // dh
    groups = n_heads // n_pair
    qo = jnp.arange(ATT_QBLOCK)[:, None]
    kj = jnp.arange(ATT_WINDOW)[None, :]
    idx = jnp.clip(qo + ATT_LEFT - kj, -(CHUNK - 1), ATT_MAX_REL) + (CHUNK - 1)
    lo = (qo // CHUNK) * CHUNK
    in_window = jnp.logical_and(kj >= lo, kj < lo + ATT_LEFT + CHUNK)
    bias = jnp.where(in_window[None], rel_bias[:, idx].astype(F32), -jnp.inf)
    bias = bias.reshape(groups, n_pair * ATT_QBLOCK, ATT_WINDOW)
    p3 = proj.reshape(bsz, seq, 3 * d)
    out = pl.pallas_call(
        functools.partial(_att_kernel, scale=dh ** -0.5, dh=dh),
        grid=(bsz, groups),
        in_specs=[pl.BlockSpec((None, seq, LANES), lambda b, g: (b, 0, g)),
                  pl.BlockSpec((None, seq, LANES), lambda b, g: (b, 0, groups + g)),
                  pl.BlockSpec((None, seq, LANES), lambda b, g: (b, 0, 2 * groups + g)),
                  pl.BlockSpec((None, n_pair * ATT_QBLOCK, ATT_WINDOW), lambda b, g: (g, 0, 0))],
        out_specs=pl.BlockSpec((None, seq, LANES), lambda b, g: (b, 0, g)),
        out_shape=jax.ShapeDtypeStruct((bsz, seq, d), F32),
        scratch_shapes=[pltpu.VMEM((ATT_LEFT + seq, LANES), F32), pltpu.VMEM((ATT_LEFT + seq, LANES), F32)],
        compiler_params=_cparams("arbitrary", "arbitrary"),
        name="chunk_attention",
    )(p3, p3, p3, bias)
    return out.reshape(bsz * seq, d)


def _mlstm_pre_kernel(xm_ref, cw_ref, cb_ref, wq_ref, wk_ref, wv_ref, wg_ref, bg_ref,
                      q_ref, k_ref, v_ref, xc_ref, g_ref, xpad, *, per, k_scale):
    i = pl.program_id(0)
    tm, inner = xm_ref.shape

    @pl.when(i % per == 0)
    def _():
        xpad[0:CONV_PAD, :] = jnp.zeros((CONV_PAD, inner), F32)

    @pl.when(i % per != 0)
    def _():
        xpad[0:CONV_PAD, :] = xpad[tm:tm + CONV_PAD, :]

    xm = xm_ref[...]
    xpad[CONV_PAD:CONV_PAD + tm, :] = xm
    acc = jnp.broadcast_to(cb_ref[...], (tm, inner))
    for j in range(MLSTM_CONV):
        off = CONV_PAD - (MLSTM_CONV - 1) + j
        acc = acc + cw_ref[j:j + 1, :] * xpad[off:off + tm, :]
    xc = acc * jax.nn.sigmoid(acc)
    xc_ref[...] = xc

    def blockdiag(t, w_ref):
        tb = t.astype(BF16)
        return jnp.concatenate(
            [jnp.dot(tb[:, g * LANES:(g + 1) * LANES], w_ref[g], preferred_element_type=F32)
             for g in range(inner // LANES)], axis=1)

    q = blockdiag(xc, wq_ref)
    k = blockdiag(xc, wk_ref) * k_scale
    v = blockdiag(xm, wv_ref)
    q_ref[...] = q
    k_ref[...] = k
    v_ref[...] = v
    g_ref[...] = (_dot_bf16(q, wg_ref[0:inner, :]) + _dot_bf16(k, wg_ref[inner:2 * inner, :])
                  + _dot_bf16(v, wg_ref[2 * inner:3 * inner, :]) + bg_ref[...])


def _blockdiag_tiles(w):
    nb, c, _ = w.shape
    per = LANES // c
    wt = w.reshape(nb // per, per, c, c)
    t = jnp.einsum('gpcd,pq->gpcqd', wt, jnp.eye(per, dtype=w.dtype))
    return t.reshape(nb // per, LANES, LANES).astype(BF16)


def _mlstm_rec_kernel(q_ref, k_ref, v_ref, g_ref, xc_ref, z_ref, gn_ref, skip_ref, o_ref,
                      c_ref, n_ref, m_ref, *, n_heads):
    h = pl.program_id(1)

    @pl.when(pl.program_id(2) == 0)
    def _():
        c_ref[...] = jnp.zeros_like(c_ref)
        n_ref[...] = jnp.zeros_like(n_ref)
        m_ref[...] = jnp.zeros_like(m_ref)

    causal = _tril_ones(CHUNK)
    lane = lax.broadcasted_iota(jnp.int32, (CHUNK, LANES), 1)
    ones = jnp.ones((CHUNK, LANES), F32)

    def col(t, sel):
        return jnp.sum(jnp.where(sel, t, 0.0), axis=1, keepdims=True)

    def row_bcast(t, sel):
        return lax.dot_general(ones, jnp.where(sel, t, 0.0), (((1,), (1,)), ((), ())),
                               precision=HI, preferred_element_type=F32)

    def body(c, carry):
        sl = pl.ds(pl.multiple_of(c * CHUNK, CHUNK), CHUNK)
        gates = g_ref[sl, :]
        bcum = jnp.dot(causal.astype(F32), jax.nn.log_sigmoid(gates), precision=HI,
                       preferred_element_type=F32)
        sel_i, sel_f = lane == h, lane == n_heads + h
        i_col, b_col = col(gates, sel_i), col(bcum, sel_f)
        m_prev = m_ref[:, 0:1]
        log_intra = jnp.where(causal, b_col - row_bcast(bcum, sel_f) + row_bcast(gates, sel_i), -jnp.inf)
        log_inter = b_col + m_prev
        m_row = jnp.maximum(log_inter, jnp.max(log_intra, axis=1, keepdims=True))
        w_intra = jnp.exp(log_intra - m_row)
        w_inter = jnp.exp(log_inter - m_row)
        q, k, v = q_ref[sl, :], k_ref[sl, :], v_ref[sl, :]
        s = _dot_bf16(q, k, (((1,), (1,)), ((), ()))) * w_intra
        cmat = c_ref[...]
        nvec = n_ref[...]
        num = w_inter * _dot_bf16(q, cmat) + _dot_bf16(s, v)
        den = w_inter * jnp.sum(q * nvec, axis=1, keepdims=True) + jnp.sum(s, axis=1, keepdims=True)
        hc = num / jnp.maximum(jnp.abs(den), jnp.exp(-m_row))
        b_last = b_col[CHUNK - 1:CHUNK, :]
        log_keep = b_last + m_prev
        log_w = b_last - b_col + i_col
        m_new = jnp.maximum(log_keep, jnp.max(log_w, axis=0, keepdims=True))
        keep = jnp.exp(log_keep - m_new)
        wk = jnp.exp(log_w - m_new) * k
        c_ref[...] = keep * cmat + _dot_bf16(wk, v, (((0,), (0,)), ((), ())))
        n_ref[...] = keep * nvec + jnp.sum(wk, axis=0, keepdims=True)
        m_ref[...] = jnp.broadcast_to(m_new, m_ref.shape)
        z = z_ref[sl, :]
        o_ref[sl, :] = (_row_norm(hc, True) * gn_ref[...] + skip_ref[...] * xc_ref[sl, :]) * (z * jax.nn.sigmoid(z))
        return carry

    lax.fori_loop(0, q_ref.shape[0] // CHUNK, body, 0, unroll=2)


def _mlstm_mixer(proj, conv_w, conv_b, w_q, w_k, w_v, w_gates, b_gates, gn_g, skip, bsz, seq):
    t = proj.shape[0]
    inner = proj.shape[1] // 2
    n_heads, dh = gn_g.shape
    tm = MLSTM_PRE_TILE
    n_g = w_gates.shape[1]
    wg = jnp.pad(w_gates, ((0, 0), (0, LANES - n_g))).astype(BF16)
    bg = jnp.pad(b_gates.reshape(1, n_g), ((0, 0), (0, LANES - n_g)))
    tile_spec = pl.BlockSpec((tm, inner), lambda i: (i, 0))
    bd_spec = pl.BlockSpec((inner // LANES, LANES, LANES), lambda i: (0, 0, 0))
    q, k, v, xc, gates = pl.pallas_call(
        functools.partial(_mlstm_pre_kernel, per=seq // tm, k_scale=dh ** -0.5),
        grid=(t // tm,),
        in_specs=[tile_spec,
                  pl.BlockSpec((MLSTM_CONV, inner), lambda i: (0, 0)),
                  pl.BlockSpec((1, inner), lambda i: (0, 0)),
                  bd_spec, bd_spec, bd_spec,
                  pl.BlockSpec((3 * inner, LANES), lambda i: (0, 0)),
                  pl.BlockSpec((1, LANES), lambda i: (0, 0))],
        out_specs=[tile_spec, tile_spec, tile_spec, tile_spec, pl.BlockSpec((tm, LANES), lambda i: (i, 0))],
        out_shape=[jax.ShapeDtypeStruct((t, inner), F32)] * 4 + [jax.ShapeDtypeStruct((t, LANES), F32)],
        scratch_shapes=[pltpu.VMEM((tm + CONV_PAD, inner), F32)],
        compiler_params=_cparams("arbitrary"),
        name="mlstm_pre",
    )(proj, conv_w, conv_b.reshape(1, inner), _blockdiag_tiles(w_q), _blockdiag_tiles(w_k),
      _blockdiag_tiles(w_v), wg, bg)

    lb = SEQ_BLOCK
    r3 = lambda a: a.reshape(bsz, seq, a.shape[-1])
    head_spec = pl.BlockSpec((None, lb, dh), lambda b, h, s: (b, s, h))
    out = pl.pallas_call(
        functools.partial(_mlstm_rec_kernel, n_heads=n_heads),
        grid=(bsz, n_heads, seq // lb),
        in_specs=[head_spec, head_spec, head_spec,
                  pl.BlockSpec((None, lb, LANES), lambda b, h, s: (b, s, 0)),
                  head_spec,
                  pl.BlockSpec((None, lb, dh), lambda b, h, s: (b, s, n_heads + h)),
                  pl.BlockSpec((None, 1, dh), lambda b, h, s: (h, 0, 0)),
                  pl.BlockSpec((1, dh), lambda b, h, s: (0, h))],
        out_specs=head_spec,
        out_shape=jax.ShapeDtypeStruct((bsz, seq, inner), F32),
        scratch_shapes=[pltpu.VMEM((dh, dh), F32), pltpu.VMEM((1, dh), F32), pltpu.VMEM((1, LANES), F32)],
        compiler_params=_cparams("arbitrary", "arbitrary", "arbitrary"),
        name="mlstm_rec",
    )(r3(q), r3(k), r3(v), r3(gates), r3(xc), r3(proj), gn_g.reshape(n_heads, 1, dh), skip.reshape(1, inner))
    return out.reshape(t, inner)


def kernel(x, c, ada_w, ada_b, norm_mix_g, norm_ffn_g, norm_final_g, ret_w_in, ret_gn_g, ret_w_out, att_w_in, att_rel_bias, att_w_out, gla_w_in, gla_w_gate, gla_b_gate, gla_gn_g, gla_w_out, mlstm_w_in, mlstm_conv_w, mlstm_conv_b, mlstm_w_q, mlstm_w_k, mlstm_w_v, mlstm_w_gates, mlstm_b_gates, mlstm_gn_g, mlstm_skip, mlstm_w_out, router_w, router_b, moe_w1, moe_b1, moe_w2, moe_b2):
    bsz, seq, d = x.shape
    depth, n_exp, _, dh2 = moe_w1.shape
    xt = x.reshape(bsz * seq, d)
    mod = _ada_mod(c, ada_w, ada_b)
    w1_bf = moe_w1.astype(BF16).reshape(depth * n_exp, d, dh2)
    w2_bf = moe_w2.astype(BF16).reshape(depth * n_exp, dh2 // 2, d)
    b1_all = moe_b1.reshape(depth * n_exp, 1, dh2)
    b2_all = moe_b2.reshape(depth * n_exp, 1, d)
    for i in range(depth):
        kind, j = i % 4, i // 4
        sh_a, sc_a, g_a, sh_f, sc_f, g_f = jnp.split(mod[i], 6, axis=-1)
        if kind == 0:
            proj = _norm_mm(xt, norm_mix_g[i], sc_a, sh_a, ret_w_in[j].astype(BF16), seq)
            y = _retention_mixer(proj, ret_gn_g[j], bsz, seq)
            w_out = ret_w_out[j]
        elif kind == 1:
            proj = _norm_mm(xt, norm_mix_g[i], sc_a, sh_a, att_w_in[j].astype(BF16), seq)
            y = _attention_mixer(proj, att_rel_bias[j], bsz, seq)
            w_out = att_w_out[j]
        elif kind == 2:
            n_heads, dv = gla_gn_g[j].shape
            w_gla = _gla_proj_weight(gla_w_in[j], n_heads * dv // 2, n_heads * dv).astype(BF16)
            proj = _norm_mm(xt, norm_mix_g[i], sc_a, sh_a, w_gla, seq, tn=w_gla.shape[1] // 5)
            y = _gla_mixer(proj, gla_w_gate[j], gla_b_gate[j], gla_gn_g[j], bsz, seq)
            w_out = gla_w_out[j]
        else:
            proj = _norm_mm(xt, norm_mix_g[i], sc_a, sh_a, mlstm_w_in[j].astype(BF16), seq)
            y = _mlstm_mixer(proj, mlstm_conv_w[j], mlstm_conv_b[j], mlstm_w_q[j], mlstm_w_k[j], mlstm_w_v[j],
                             mlstm_w_gates[j], mlstm_b_gates[j], mlstm_gn_g[j], mlstm_skip[j], bsz, seq)
            w_out = mlstm_w_out[j]
        xt = _mm_res(y, w_out.astype(BF16), xt, g_a, seq)
        xt = _moe_layer(xt, norm_ffn_g[i], sc_f, sh_f, g_f, router_w[i], router_b[i],
                        w1_bf, b1_all, w2_bf, b2_all, i, seq)
    return _final_norm(xt, norm_final_g).reshape(bsz, seq, d)
```

```python
import functools

import jax
import jax.numpy as jnp
from jax import lax
from jax.experimental import pallas as pl
from jax.experimental.pallas import tpu as pltpu

F32 = jnp.float32
BF16 = jnp.bfloat16
HI = lax.Precision.HIGHEST

CHUNK = 64
EPS = 1e-6
ROPE_BASE = 10000.0
ATT_LEFT_CHUNKS = 8
ATT_MAX_REL = 256
GLA_GATE_RANK = 16
GLA_GATE_NORM = 16.0
MLSTM_CONV = 4
N_EXPERTS = 32
TOP_K = 4
SWIGLU_LIMIT = 7.0
SWIGLU_ALPHA = 1.702

VMEM_LIMIT_BYTES = 56 * 1024 * 1024
LANES = 128
SUBLANES = 8
MOE_TILE = 512
COMBINE_TILE = 256
SEQ_BLOCK = 512
MLSTM_PRE_TILE = 256
CONV_PAD = 8
ATT_QBLOCK = 2 * CHUNK
ATT_LEFT = ATT_LEFT_CHUNKS * CHUNK
ATT_WINDOW = ATT_LEFT + ATT_QBLOCK


def _cparams(*sem):
    return pltpu.CompilerParams(dimension_semantics=sem, vmem_limit_bytes=VMEM_LIMIT_BYTES)


def _dot_bf16(a, b, dims=(((1,), (0,)), ((), ()))):
    return lax.dot_general(a.astype(BF16), b.astype(BF16), dims, preferred_element_type=F32)


def _tril_ones(n):
    row = lax.broadcasted_iota(jnp.int32, (n, n), 0)
    col = lax.broadcasted_iota(jnp.int32, (n, n), 1)
    return row >= col


def _row_norm(x, center):
    if center:
        x = x - jnp.mean(x, axis=-1, keepdims=True)
    return x * lax.rsqrt(jnp.mean(x * x, axis=-1, keepdims=True) + EPS)


def _load_row_tiles(ref, rows):
    return jnp.concatenate([ref[pl.ds(j, rows, stride=SUBLANES), :] for j in range(SUBLANES)], axis=1)


def _store_row_tiles(ref, val):
    rows = val.shape[0]
    for j in range(SUBLANES):
        ref[pl.ds(j, rows, stride=SUBLANES), :] = val[:, j * LANES:(j + 1) * LANES]


def _ada_kernel(c_ref, w_ref, b_ref, o_ref):
    c = c_ref[...]
    cond = c * jax.nn.sigmoid(c)
    o_ref[0] = jnp.dot(cond, w_ref[0], preferred_element_type=F32, precision=HI) + b_ref[0]


def _ada_mod(c, ada_w, ada_b):
    depth, d, n = ada_w.shape
    b = c.shape[0]
    tn = 1536
    return pl.pallas_call(
        _ada_kernel,
        grid=(depth, n // tn),
        in_specs=[pl.BlockSpec((b, d), lambda l, j: (0, 0)),
                  pl.BlockSpec((1, d, tn), lambda l, j: (l, 0, j)),
                  pl.BlockSpec((1, 1, tn), lambda l, j: (l, 0, j))],
        out_specs=pl.BlockSpec((1, b, tn), lambda l, j: (l, 0, j)),
        out_shape=jax.ShapeDtypeStruct((depth, b, n), F32),
        compiler_params=_cparams("arbitrary", "arbitrary"),
        name="ada_mod",
    )(c, ada_w, ada_b.reshape(depth, 1, n))


def _modulated_norm(x, g, sc, sh):
    ms = jnp.mean(x * x, axis=-1, keepdims=True)
    return (x * lax.rsqrt(ms + EPS) * g) * (1.0 + sc) + sh


def _norm_mm_kernel(x_ref, g_ref, sc_ref, sh_ref, w_ref, o_ref, h_scr):
    @pl.when(pl.program_id(1) == 0)
    def _():
        h = _modulated_norm(x_ref[...], g_ref[...], sc_ref[0], sh_ref[0])
        h_scr[...] = h.astype(BF16)

    o_ref[...] = jnp.dot(h_scr[...], w_ref[...], preferred_element_type=F32)


def _norm_mm(x, g, sc, sh, w, seq, tm=1024, tn=512):
    t, d = x.shape
    n = w.shape[1]
    per = seq // tm
    bsz = sc.shape[0]
    return pl.pallas_call(
        _norm_mm_kernel,
        grid=(t // tm, n // tn),
        in_specs=[pl.BlockSpec((tm, d), lambda i, j: (i, 0)),
                  pl.BlockSpec((1, d), lambda i, j: (0, 0)),
                  pl.BlockSpec((1, 1, d), lambda i, j: (i // per, 0, 0)),
                  pl.BlockSpec((1, 1, d), lambda i, j: (i // per, 0, 0)),
                  pl.BlockSpec((d, tn), lambda i, j: (0, j))],
        out_specs=pl.BlockSpec((tm, tn), lambda i, j: (i, j)),
        out_shape=jax.ShapeDtypeStruct((t, n), F32),
        scratch_shapes=[pltpu.VMEM((tm, d), BF16)],
        compiler_params=_cparams("arbitrary", "arbitrary"),
        name="norm_mm",
    )(x, g.reshape(1, d), sc.reshape(bsz, 1, d), sh.reshape(bsz, 1, d), w)


def _mm_res_kernel(y_ref, w_ref, x_ref, gate_ref, o_ref):
    acc = jnp.dot(y_ref[...].astype(BF16), w_ref[...], preferred_element_type=F32)
    o_ref[...] = x_ref[...] + gate_ref[0] * acc


def _mm_res(y, w, x, gate, seq, tm=512):
    t, k = y.shape
    d = w.shape[1]
    per = seq // tm
    bsz = gate.shape[0]
    return pl.pallas_call(
        _mm_res_kernel,
        grid=(t // tm,),
        in_specs=[pl.BlockSpec((tm, k), lambda i: (i, 0)),
                  pl.BlockSpec((k, d), lambda i: (0, 0)),
                  pl.BlockSpec((tm, d), lambda i: (i, 0)),
                  pl.BlockSpec((1, 1, d), lambda i: (i // per, 0, 0))],
        out_specs=pl.BlockSpec((tm, d), lambda i: (i, 0)),
        out_shape=jax.ShapeDtypeStruct((t, d), F32),
        compiler_params=_cparams("arbitrary"),
        name="mm_res",
    )(y, w, x, gate.reshape(bsz, 1, d))


def _final_norm_kernel(x_ref, g_ref, o_ref):
    x = x_ref[...]
    ms = jnp.mean(x * x, axis=-1, keepdims=True)
    o_ref[...] = x * lax.rsqrt(ms + EPS) * g_ref[...]


def _final_norm(x, g, tm=1024):
    t, d = x.shape
    return pl.pallas_call(
        _final_norm_kernel,
        grid=(t // tm,),
        in_specs=[pl.BlockSpec((tm, d), lambda i: (i, 0)),
                  pl.BlockSpec((1, d), lambda i: (0, 0))],
        out_specs=pl.BlockSpec((tm, d), lambda i: (i, 0)),
        out_shape=jax.ShapeDtypeStruct((t, d), F32),
        compiler_params=_cparams("arbitrary"),
        name="final_norm",
    )(x, g.reshape(1, d))


def _norm_router_kernel(x_ref, g_ref, sc_ref, sh_ref, rw_ref, rb_ref, h_ref, lg_ref):
    h = _modulated_norm(x_ref[...], g_ref[...], sc_ref[0], sh_ref[0])
    _store_row_tiles(h_ref, h)
    lg_ref[...] = jnp.dot(h, rw_ref[...], preferred_element_type=F32, precision=HI) + rb_ref[...]


def _norm_router(x, g, sc, sh, rw, rb, seq, tm=512):
    t, d = x.shape
    per = seq // tm
    bsz = sc.shape[0]
    e = rw.shape[1]
    rw_p = jnp.pad(rw, ((0, 0), (0, LANES - e)))
    rb_p = jnp.pad(rb.reshape(1, e), ((0, 0), (0, LANES - e)))
    return pl.pallas_call(
        _norm_router_kernel,
        grid=(t // tm,),
        in_specs=[pl.BlockSpec((tm, d), lambda i: (i, 0)),
                  pl.BlockSpec((1, d), lambda i: (0, 0)),
                  pl.BlockSpec((1, 1, d), lambda i: (i // per, 0, 0)),
                  pl.BlockSpec((1, 1, d), lambda i: (i // per, 0, 0)),
                  pl.BlockSpec((d, LANES), lambda i: (0, 0)),
                  pl.BlockSpec((1, LANES), lambda i: (0, 0))],
        out_specs=[pl.BlockSpec((tm * SUBLANES, LANES), lambda i: (i, 0)),
                   pl.BlockSpec((tm, LANES), lambda i: (i, 0))],
        out_shape=[jax.ShapeDtypeStruct((t * SUBLANES, LANES), F32),
                   jax.ShapeDtypeStruct((t, LANES), F32)],
        compiler_params=_cparams("arbitrary"),
        name="norm_router",
    )(x, g.reshape(1, d), sc.reshape(bsz, 1, d), sh.reshape(bsz, 1, d), rw_p, rb_p)


def _row_tile_copy(src_hbm, src_row8, dst_vmem, dst_row, sem):
    return pltpu.make_async_copy(
        src_hbm.at[pl.ds(pl.multiple_of(src_row8, SUBLANES), SUBLANES)],
        dst_vmem.at[pl.ds(pl.multiple_of(dst_row * SUBLANES, SUBLANES), SUBLANES)], sem)


def _expert_kernel(te_ref, nt_ref, idx_cur_ref, idx_nxt_ref, h_hbm, w1_ref, b1_ref, w2_ref, b2_ref,
                   o_ref, xbuf, sem):
    i = pl.program_id(0)
    n_used = nt_ref[0]
    tm = xbuf.shape[1] // SUBLANES
    slot = i % 2

    def issue(idx_ref, s):
        def body(r8, carry):
            for u in range(SUBLANES):
                r = r8 * SUBLANES + u
                _row_tile_copy(h_hbm, idx_ref[0, 0, r], xbuf.at[s], r, sem.at[s]).start(priority=u % 2)
            return carry
        lax.fori_loop(0, tm // SUBLANES, body, 0)

    @pl.when(jnp.logical_and(i == 0, n_used > 0))
    def _():
        issue(idx_cur_ref, 0)

    @pl.when(i + 1 < n_used)
    def _():
        issue(idx_nxt_ref, 1 - slot)

    @pl.when(i < n_used)
    def _():
        pltpu.make_async_copy(h_hbm.at[pl.ds(0, tm * SUBLANES)], xbuf.at[slot], sem.at[slot]).wait()
        x = _load_row_tiles(xbuf.at[slot], tm).astype(BF16)
        hid = jnp.dot(x, w1_ref[0], preferred_element_type=F32) + b1_ref[0]
        de = hid.shape[1] // 2
        gate = jnp.minimum(hid[:, :de], SWIGLU_LIMIT)
        lin = jnp.clip(hid[:, de:], -SWIGLU_LIMIT, SWIGLU_LIMIT)
        act = gate * jax.nn.sigmoid(SWIGLU_ALPHA * gate) * (lin + 1.0)
        y = jnp.dot(act.astype(BF16), w2_ref[0], preferred_element_type=F32) + b2_ref[0]
        _store_row_tiles(o_ref, y)

    @pl.when(i >= n_used)
    def _():
        o_ref[...] = jnp.zeros_like(o_ref)


def _expert_ffn(h_tiles, src_row8, tile_expert, n_tiles_used, w1, b1, w2, b2, layer):
    p = src_row8.shape[0]
    tm = MOE_TILE
    nt = p // tm
    _, d, dh = w1.shape
    idx3 = src_row8.reshape(nt, 1, tm)
    base = layer * N_EXPERTS
    grid_spec = pltpu.PrefetchScalarGridSpec(
        num_scalar_prefetch=2,
        grid=(nt,),
        in_specs=[
            pl.BlockSpec((1, 1, tm), lambda i, te, nu: (i, 0, 0), memory_space=pltpu.SMEM),
            pl.BlockSpec((1, 1, tm), lambda i, te, nu: (jnp.minimum(i + 1, nt - 1), 0, 0),
                         memory_space=pltpu.SMEM),
            pl.BlockSpec(memory_space=pl.ANY),
            pl.BlockSpec((1, d, dh), lambda i, te, nu: (base + te[i], 0, 0)),
            pl.BlockSpec((1, 1, dh), lambda i, te, nu: (base + te[i], 0, 0)),
            pl.BlockSpec((1, dh // 2, d), lambda i, te, nu: (base + te[i], 0, 0)),
            pl.BlockSpec((1, 1, d), lambda i, te, nu: (base + te[i], 0, 0)),
        ],
        out_specs=pl.BlockSpec((tm * SUBLANES, LANES), lambda i, te, nu: (i, 0)),
        scratch_shapes=[pltpu.VMEM((2, tm * SUBLANES, LANES), F32), pltpu.SemaphoreType.DMA((2,))],
    )
    return pl.pallas_call(
        _expert_kernel,
        grid_spec=grid_spec,
        out_shape=jax.ShapeDtypeStruct((p * SUBLANES, LANES), F32),
        compiler_params=_cparams("arbitrary"),
        name="expert_ffn",
    )(tile_expert, n_tiles_used, idx3, idx3, h_tiles, w1, b1, w2, b2)


def _combine_kernel(idx_cur_ref, idx_nxt_ref, y_hbm, x_ref, w_ref, gate_ref, o_ref, ybuf, sem):
    i = pl.program_id(0)
    n = pl.num_programs(0)
    tc = x_ref.shape[0]
    slot = i % 2

    def issue(idx_ref, s):
        def body(r2, carry):
            for u in range(2):
                r = r2 * 2 + u
                for k in range(TOP_K):
                    _row_tile_copy(y_hbm, idx_ref[0, 0, r * TOP_K + k], ybuf.at[s, k], r,
                                   sem.at[s]).start(priority=k % 2)
            return carry
        lax.fori_loop(0, tc // 2, body, 0)

    @pl.when(i == 0)
    def _():
        issue(idx_cur_ref, 0)

    @pl.when(i + 1 < n)
    def _():
        issue(idx_nxt_ref, 1 - slot)

    for k in range(TOP_K):
        pltpu.make_async_copy(y_hbm.at[pl.ds(0, tc * SUBLANES)], ybuf.at[slot, k], sem.at[slot]).wait()
    w = w_ref[...]
    acc = w[:, 0:1] * _load_row_tiles(ybuf.at[slot, 0], tc)
    for k in range(1, TOP_K):
        acc = acc + w[:, k:k + 1] * _load_row_tiles(ybuf.at[slot, k], tc)
    o_ref[...] = x_ref[...] + gate_ref[0] * acc


def _moe_combine(y_tiles, slot8_of_pair, top_w, x, gate, seq):
    t, d = x.shape
    tc = COMBINE_TILE
    nt = t // tc
    per = seq // tc
    bsz = gate.shape[0]
    idx3 = slot8_of_pair.reshape(nt, 1, tc * TOP_K)
    return pl.pallas_call(
        _combine_kernel,
        grid=(nt,),
        in_specs=[
            pl.BlockSpec((1, 1, tc * TOP_K), lambda i: (i, 0, 0), memory_space=pltpu.SMEM),
            pl.BlockSpec((1, 1, tc * TOP_K), lambda i: (jnp.minimum(i + 1, nt - 1), 0, 0),
                         memory_space=pltpu.SMEM),
            pl.BlockSpec(memory_space=pl.ANY),
            pl.BlockSpec((tc, d), lambda i: (i, 0)),
            pl.BlockSpec((tc, TOP_K), lambda i: (i, 0)),
            pl.BlockSpec((1, 1, d), lambda i: (i // per, 0, 0)),
        ],
        out_specs=pl.BlockSpec((tc, d), lambda i: (i, 0)),
        out_shape=jax.ShapeDtypeStruct((t, d), F32),
        scratch_shapes=[pltpu.VMEM((2, TOP_K, tc * SUBLANES, LANES), F32), pltpu.SemaphoreType.DMA((2,))],
        compiler_params=_cparams("arbitrary"),
        name="moe_combine",
    )(idx3, idx3, y_tiles, x, top_w, gate.reshape(bsz, 1, d))


def _route(logits):
    t = logits.shape[0]
    top_v, top_i = lax.top_k(logits[:, :N_EXPERTS], TOP_K)
    top_w = jax.nn.softmax(top_v, axis=-1)
    flat_e = top_i.reshape(-1).astype(jnp.int32)
    onehot = (flat_e[:, None] == jnp.arange(N_EXPERTS, dtype=jnp.int32)[None, :]).astype(jnp.int32)
    csum = jnp.cumsum(onehot, axis=0)
    counts = csum[-1]
    padded = ((counts + MOE_TILE - 1) // MOE_TILE) * MOE_TILE
    ends = jnp.cumsum(padded)
    starts = ends - padded
    slot_of_pair = jnp.sum(onehot * (csum - 1 + starts[None, :]), axis=1)
    n_fill = N_EXPERTS * MOE_TILE
    fill_end = jnp.cumsum(padded - counts)
    fill_key = jnp.sum((jnp.arange(n_fill, dtype=jnp.int32)[:, None] >= fill_end[None, :]).astype(jnp.int32), axis=1)
    keys = jnp.concatenate([flat_e, fill_key])
    toks = jnp.concatenate([jnp.arange(t * TOP_K, dtype=jnp.int32) // TOP_K, jnp.zeros((n_fill,), jnp.int32)])
    _, src_tok = lax.sort((keys, toks), num_keys=1, is_stable=True)
    n_tiles = (t * TOP_K + n_fill) // MOE_TILE
    tile_start = jnp.arange(n_tiles, dtype=jnp.int32) * MOE_TILE
    tile_expert = jnp.sum((tile_start[:, None] >= ends[None, :]).astype(jnp.int32), axis=1)
    n_used = (ends[-1] // MOE_TILE).astype(jnp.int32)
    last_e = jnp.sum(jnp.where(jnp.arange(n_tiles) == n_used - 1, tile_expert, 0))
    tile_expert = jnp.where(jnp.arange(n_tiles) < n_used, tile_expert, last_e).astype(jnp.int32)
    return (top_w, (slot_of_pair * SUBLANES).astype(jnp.int32), (src_tok * SUBLANES).astype(jnp.int32),
            tile_expert, n_used.reshape(1))


def _moe_layer(x, g, sc, sh, gate, rw, rb, w1, b1, w2, b2, layer, seq):
    h_tiles, logits = _norm_router(x, g, sc, sh, rw, rb, seq)
    top_w, slot8_of_pair, src_row8, tile_expert, n_used = _route(logits)
    y_tiles = _expert_ffn(h_tiles, src_row8, tile_expert, n_used, w1, b1, w2, b2, layer)
    return _moe_combine(y_tiles, slot8_of_pair, top_w, x, gate, seq)


def _chunk_pos(shape):
    return lax.broadcasted_iota(jnp.int32, shape, 0) & (CHUNK - 1)


def _inchunk_cumsum(x):
    pos = _chunk_pos(x.shape)
    sh = 1
    while sh < CHUNK:
        x = x + jnp.where(pos >= sh, pltpu.roll(x, sh, axis=0), 0.0)
        sh *= 2
    return x


def _chunk_last_rows(b):
    rows, d = b.shape
    b3 = b.reshape(rows // CHUNK, CHUNK, d)
    return jnp.broadcast_to(b3[:, CHUNK - 1:CHUNK, :], b3.shape).reshape(rows, d)


def _chunk_causal_mask(rows):
    r = lax.broadcasted_iota(jnp.int32, (rows, rows), 0)
    c = lax.broadcasted_iota(jnp.int32, (rows, rows), 1)
    return jnp.logical_and(r >= c, (r // CHUNK) == (c // CHUNK))


def _linattn_block(q, k, v, b, b_rem, chunk_decay, state_ref):
    rows = q.shape[0]
    q_in = (q * jnp.exp(b)).astype(BF16)
    k_in = (k * jnp.exp(-b)).astype(BF16)
    k_st = (k * jnp.exp(b_rem)).astype(BF16)
    vb = v.astype(BF16)
    scores = lax.dot_general(q_in, k_in, (((1,), (1,)), ((), ())), preferred_element_type=F32)
    scores = jnp.where(_chunk_causal_mask(rows), scores, 0.0)
    o_intra = jnp.dot(scores.astype(BF16), vb, preferred_element_type=F32)
    state = state_ref[...]
    o_inter = []
    for c in range(rows // CHUNK):
        sl = slice(c * CHUNK, (c + 1) * CHUNK)
        o_inter.append(jnp.dot(q_in[sl], state.astype(BF16), preferred_element_type=F32))
        upd = lax.dot_general(k_st[sl], vb[sl], (((0,), (0,)), ((), ())), preferred_element_type=F32)
        state = chunk_decay(c) * state + upd
    state_ref[...] = state
    return o_intra + jnp.concatenate(o_inter, axis=0)


HEADS_PER_STEP = 2


def _ret_kernel(q_ref, k_ref, v_ref, g_ref, cos_ref, sin_ref, lg_ref, gn_ref, o_ref, state_ref, *, k_scale):
    @pl.when(pl.program_id(2) == 0)
    def _():
        state_ref[...] = jnp.zeros_like(state_ref)

    rows = q_ref.shape[0]
    dk = q_ref.shape[1] // HEADS_PER_STEP
    dv = v_ref.shape[1] // HEADS_PER_STEP
    half = dk // 2
    cos, sin = cos_ref[...], sin_ref[...]

    def rope(t):
        t1, t2 = t[:, :half], t[:, half:]
        return jnp.concatenate([t1 * cos - t2 * sin, t1 * sin + t2 * cos], axis=1)

    pos = _chunk_pos((rows, dk)).astype(F32)
    for hh in range(HEADS_PER_STEP):
        ks, vs = slice(hh * dk, (hh + 1) * dk), slice(hh * dv, (hh + 1) * dv)
        lg = lg_ref[hh]
        b = (pos + 1.0) * lg
        b_rem = (CHUNK - 1.0 - pos) * lg
        decay = jnp.exp(CHUNK * lg[:, 0:1])
        q = rope(q_ref[:, ks])
        k = rope(k_ref[:, ks]) * k_scale
        o = _linattn_block(q, k, v_ref[:, vs], b, b_rem, lambda c: decay, state_ref.at[hh])
        g = g_ref[:, vs]
        o_ref[:, vs] = _row_norm(o, True) * gn_ref[hh] * (g * jax.nn.sigmoid(g))


def _retention_mixer(proj, gn_g, bsz, seq):
    n_heads, dv = gn_g.shape
    dk = dv // 2
    lb = SEQ_BLOCK
    hps = HEADS_PER_STEP
    n_hp = n_heads // hps
    p3 = proj.reshape(bsz, seq, proj.shape[-1])
    half = dk // 2
    inv = ROPE_BASE ** (-jnp.arange(half, dtype=F32) / half)
    ang = jnp.arange(seq, dtype=F32)[:, None] * inv[None, :]
    cos, sin = jnp.cos(ang), jnp.sin(ang)
    log_gamma = jnp.log1p(-jnp.exp2(-5.0 - jnp.arange(n_heads, dtype=F32)))
    lg = jnp.broadcast_to(log_gamma[:, None, None], (n_heads, 1, dk))
    out = pl.pallas_call(
        functools.partial(_ret_kernel, k_scale=dk ** -0.5),
        grid=(bsz, n_hp, seq // lb),
        in_specs=[pl.BlockSpec((None, lb, hps * dk), lambda b, h, s: (b, s, h)),
                  pl.BlockSpec((None, lb, hps * dk), lambda b, h, s: (b, s, n_hp + h)),
                  pl.BlockSpec((None, lb, hps * dv), lambda b, h, s: (b, s, n_hp + h)),
                  pl.BlockSpec((None, lb, hps * dv), lambda b, h, s: (b, s, 2 * n_hp + h)),
                  pl.BlockSpec((lb, half), lambda b, h, s: (s, 0)),
                  pl.BlockSpec((lb, half), lambda b, h, s: (s, 0)),
                  pl.BlockSpec((hps, 1, dk), lambda b, h, s: (h, 0, 0)),
                  pl.BlockSpec((hps, 1, dv), lambda b, h, s: (h, 0, 0))],
        out_specs=pl.BlockSpec((None, lb, hps * dv), lambda b, h, s: (b, s, h)),
        out_shape=jax.ShapeDtypeStruct((bsz, seq, n_heads * dv), F32),
        scratch_shapes=[pltpu.VMEM((hps, dk, dv), F32)],
        compiler_params=_cparams("arbitrary", "arbitrary", "arbitrary"),
        name="retention",
    )(p3, p3, p3, p3, cos, sin, lg, gn_g.reshape(n_heads, 1, dv))
    return out.reshape(bsz * seq, n_heads * dv)


def _gla_kernel(q_ref, k_ref, v_ref, r_ref, glow_ref, wg_ref, bg_ref, gn_ref, o_ref, state_ref, *, q_scale):
    @pl.when(pl.program_id(2) == 0)
    def _():
        state_ref[...] = jnp.zeros_like(state_ref)

    dk = q_ref.shape[1] // HEADS_PER_STEP
    dv = v_ref.shape[1] // HEADS_PER_STEP

    rows = q_ref.shape[0]
    z = jnp.dot(glow_ref[...], wg_ref[...], precision=HI, preferred_element_type=F32) + bg_ref[...]
    la_all = jax.nn.log_sigmoid(z) / GLA_GATE_NORM
    chunk_sel = (lax.broadcasted_iota(jnp.int32, (rows, LANES), 0) // CHUNK
                 == lax.broadcasted_iota(jnp.int32, (rows, LANES), 1)).astype(F32)
    for hh in range(HEADS_PER_STEP):
        ks, vs = slice(hh * dk, (hh + 1) * dk), slice(hh * dv, (hh + 1) * dv)
        la = la_all[:, ks]
        b = _inchunk_cumsum(la)
        b_rem = _chunk_last_rows(b) - b
        decay = jnp.exp(lax.dot_general(la, chunk_sel, (((0,), (0,)), ((), ())),
                                        precision=HI, preferred_element_type=F32))
        o = _linattn_block(q_ref[:, ks] * q_scale, k_ref[:, ks], v_ref[:, vs], b, b_rem,
                           lambda c: decay[:, c:c + 1], state_ref.at[hh])
        r = r_ref[:, vs]
        o_ref[:, vs] = _row_norm(o, False) * gn_ref[hh] * (r * jax.nn.sigmoid(r))


def _gla_proj_weight(w_in, dq, dvv):
    q_k_v = w_in[:, :2 * dq + dvv]
    g_low = w_in[:, 2 * dq + dvv:2 * dq + dvv + GLA_GATE_RANK]
    r = w_in[:, 2 * dq + dvv + GLA_GATE_RANK:]
    pad = jnp.zeros((w_in.shape[0], LANES - GLA_GATE_RANK), w_in.dtype)
    return jnp.concatenate([q_k_v, r, g_low, pad], axis=1)


def _gla_mixer(proj, w_gate, b_gate, gn_g, bsz, seq):
    n_heads, dv = gn_g.shape
    dk = dv // 2
    dq, dvv = n_heads * dk, n_heads * dv
    lb = SEQ_BLOCK
    hps = HEADS_PER_STEP
    n_hp = n_heads // hps
    p3 = proj.reshape(bsz, seq, proj.shape[-1])
    wg = jnp.pad(w_gate, ((0, LANES - GLA_GATE_RANK), (0, 0)))
    out = pl.pallas_call(
        functools.partial(_gla_kernel, q_scale=dk ** -0.5),
        grid=(bsz, n_hp, seq // lb),
        in_specs=[pl.BlockSpec((None, lb, hps * dk), lambda b, h, s: (b, s, h)),
                  pl.BlockSpec((None, lb, hps * dk), lambda b, h, s: (b, s, n_hp + h)),
                  pl.BlockSpec((None, lb, hps * dv), lambda b, h, s: (b, s, n_hp + h)),
                  pl.BlockSpec((None, lb, hps * dv), lambda b, h, s: (b, s, 2 * n_hp + h)),
                  pl.BlockSpec((None, lb, LANES), lambda b, h, s: (b, s, (2 * dq + 2 * dvv) // LANES)),
                  pl.BlockSpec((LANES, hps * dk), lambda b, h, s: (0, h)),
                  pl.BlockSpec((1, hps * dk), lambda b, h, s: (0, h)),
                  pl.BlockSpec((hps, 1, dv), lambda b, h, s: (h, 0, 0))],
        out_specs=pl.BlockSpec((None, lb, hps * dv), lambda b, h, s: (b, s, h)),
        out_shape=jax.ShapeDtypeStruct((bsz, seq, dvv), F32),
        scratch_shapes=[pltpu.VMEM((hps, dk, dv), F32)],
        compiler_params=_cparams("arbitrary", "arbitrary", "arbitrary"),
        name="gla",
    )(p3, p3, p3, p3, p3, wg, b_gate.reshape(1, dq), gn_g.reshape(n_heads, 1, dv))
    return out.reshape(bsz * seq, dvv)


def _att_kernel(q_ref, k_ref, v_ref, bias_ref, o_ref, kpad, vpad, *, scale, dh):
    seq, width = q_ref.shape
    n_pair = width // dh
    kpad[0:ATT_LEFT, :] = jnp.zeros((ATT_LEFT, width), F32)
    vpad[0:ATT_LEFT, :] = jnp.zeros((ATT_LEFT, width), F32)
    kpad[ATT_LEFT:ATT_LEFT + seq, :] = k_ref[...]
    vpad[ATT_LEFT:ATT_LEFT + seq, :] = v_ref[...]
    lane = lax.broadcasted_iota(jnp.int32, (ATT_QBLOCK, width), 1)
    jcol = lax.broadcasted_iota(jnp.int32, (n_pair * ATT_QBLOCK, ATT_WINDOW), 1)
    in_head = [jnp.logical_and(lane >= h * dh, lane < (h + 1) * dh) for h in range(n_pair)]

    def body(c, carry):
        start = pl.multiple_of(c * ATT_QBLOCK, ATT_QBLOCK)
        q = q_ref[pl.ds(start, ATT_QBLOCK), :] * scale
        kb = kpad[pl.ds(start, ATT_WINDOW), :].astype(BF16)
        vb = vpad[pl.ds(start, ATT_WINDOW), :].astype(BF16)
        q2 = jnp.concatenate([jnp.where(m, q, 0.0) for m in in_head], axis=0).astype(BF16)
        s = lax.dot_general(q2, kb, (((1,), (1,)), ((), ())), preferred_element_type=F32) + bias_ref[...]
        s = jnp.where(jcol >= ATT_LEFT - start, s, -jnp.inf)
        p = jnp.exp(s - jnp.max(s, axis=-1, keepdims=True))
        p = p / jnp.sum(p, axis=-1, keepdims=True)
        o2 = jnp.dot(p.astype(BF16), vb, preferred_element_type=F32)
        out = o2[0:ATT_QBLOCK, :]
        for h in range(1, n_pair):
            out = jnp.where(in_head[h], o2[h * ATT_QBLOCK:(h + 1) * ATT_QBLOCK, :], out)
        o_ref[pl.ds(start, ATT_QBLOCK), :] = out
        return carry

    lax.fori_loop(0, seq // ATT_QBLOCK, body, 0, unroll=2)


def _attention_mixer(proj, rel_bias, bsz, seq):
    d = proj.shape[-1] // 3
    n_heads = rel_bias.shape[0]
    dh = d // n_heads
    n_pair = LANES // dh
    groups = n_heads // n_pair
    qo = jnp.arange(ATT_QBLOCK)[:, None]
    kj = jnp.arange(ATT_WINDOW)[None, :]
    idx = jnp.clip(qo + ATT_LEFT - kj, -(CHUNK - 1), ATT_MAX_REL) + (CHUNK - 1)
    lo = (qo // CHUNK) * CHUNK
    in_window = jnp.logical_and(kj >= lo, kj < lo + ATT_LEFT + CHUNK)
    bias = jnp.where(in_window[None], rel_bias[:, idx].astype(F32), -jnp.inf)
    bias = bias.reshape(groups, n_pair * ATT_QBLOCK, ATT_WINDOW)
    p3 = proj.reshape(bsz, seq, 3 * d)
    out = pl.pallas_call(
        functools.partial(_att_kernel, scale=dh ** -0.5, dh=dh),
        grid=(bsz, groups),
        in_specs=[pl.BlockSpec((None, seq, LANES), lambda b, g: (b, 0, g)),
                  pl.BlockSpec((None, seq, LANES), lambda b, g: (b, 0, groups + g)),
                  pl.BlockSpec((None, seq, LANES), lambda b, g: (b, 0, 2 * groups + g)),
                  pl.BlockSpec((None, n_pair * ATT_QBLOCK, ATT_WINDOW), lambda b, g: (g, 0, 0))],
        out_specs=pl.BlockSpec((None, seq, LANES), lambda b, g: (b, 0, g)),
        out_shape=jax.ShapeDtypeStruct((bsz, seq, d), F32),
        scratch_shapes=[pltpu.VMEM((ATT_LEFT + seq, LANES), F32), pltpu.VMEM((ATT_LEFT + seq, LANES), F32)],
        compiler_params=_cparams("arbitrary", "arbitrary"),
        name="chunk_attention",
    )(p3, p3, p3, bias)
    return out.reshape(bsz * seq, d)


def _mlstm_pre_kernel(xm_ref, cw_ref, cb_ref, wq_ref, wk_ref, wv_ref, wg_ref, bg_ref,
                      q_ref, k_ref, v_ref, xc_ref, g_ref, xpad, *, per, k_scale):
    i = pl.program_id(0)
    tm, inner = xm_ref.shape

    @pl.when(i % per == 0)
    def _():
        xpad[0:CONV_PAD, :] = jnp.zeros((CONV_PAD, inner), F32)

    @pl.when(i % per != 0)
    def _():
        xpad[0:CONV_PAD, :] = xpad[tm:tm + CONV_PAD, :]

    xm = xm_ref[...]
    xpad[CONV_PAD:CONV_PAD + tm, :] = xm
    acc = jnp.broadcast_to(cb_ref[...], (tm, inner))
    for j in range(MLSTM_CONV):
        off = CONV_PAD - (MLSTM_CONV - 1) + j
        acc = acc + cw_ref[j:j + 1, :] * xpad[off:off + tm, :]
    xc = acc * jax.nn.sigmoid(acc)
    xc_ref[...] = xc

    def blockdiag(t, w_ref):
        tb = t.astype(BF16)
        return jnp.concatenate(
            [jnp.dot(tb[:, g * LANES:(g + 1) * LANES], w_ref[g], preferred_element_type=F32)
             for g in range(inner // LANES)], axis=1)

    q = blockdiag(xc, wq_ref)
    k = blockdiag(xc, wk_ref) * k_scale
    v = blockdiag(xm, wv_ref)
    qb, kb, vb = q.astype(BF16), k.astype(BF16), v.astype(BF16)
    q_ref[...] = qb
    k_ref[...] = kb
    v_ref[...] = vb
    g_ref[...] = (jnp.dot(qb, wg_ref[0:inner, :], preferred_element_type=F32)
                  + jnp.dot(kb, wg_ref[inner:2 * inner, :], preferred_element_type=F32)
                  + jnp.dot(vb, wg_ref[2 * inner:3 * inner, :], preferred_element_type=F32) + bg_ref[...])


def _blockdiag_tiles(w):
    nb, c, _ = w.shape
    per = LANES // c
    wt = w.reshape(nb // per, per, c, c)
    t = jnp.einsum('gpcd,pq->gpcqd', wt, jnp.eye(per, dtype=w.dtype))
    return t.reshape(nb // per, LANES, LANES).astype(BF16)


def _mlstm_rec_kernel(q_ref, k_ref, v_ref, g_ref, xc_ref, z_ref, gn_ref, skip_ref, o_ref,
                      c_ref, n_ref, m_ref, *, n_heads):
    head0 = pl.program_id(1) * HEADS_PER_STEP
    dh = q_ref.shape[1] // HEADS_PER_STEP

    @pl.when(pl.program_id(2) == 0)
    def _():
        c_ref[...] = jnp.zeros_like(c_ref)
        n_ref[...] = jnp.zeros_like(n_ref)
        m_ref[...] = jnp.zeros_like(m_ref)

    rows = q_ref.shape[0]
    n_chunks = rows // CHUNK
    mask = _chunk_causal_mask(rows)
    lane = lax.broadcasted_iota(jnp.int32, (rows, LANES), 1)
    sub = lax.broadcasted_iota(jnp.int32, (LANES, rows), 0)

    def col(t, idx):
        return jnp.sum(jnp.where(lane == idx, t, 0.0), axis=1, keepdims=True)

    def row(t_t, idx):
        return jnp.sum(jnp.where(sub == idx, t_t, 0.0), axis=0, keepdims=True)

    def per_chunk(vals):
        return jnp.concatenate([jnp.broadcast_to(x, (CHUNK, 1)) for x in vals], axis=0)

    gates = g_ref[...]
    bcum = _inchunk_cumsum(jax.nn.log_sigmoid(gates))
    gates_t, bcum_t = gates.T, bcum.T
    for hh in range(HEADS_PER_STEP):
        hs = slice(hh * dh, (hh + 1) * dh)
        i_idx, f_idx = head0 + hh, n_heads + head0 + hh
        i_col, b_col = col(gates, i_idx), col(bcum, f_idx)
        b_last_rows = _chunk_last_rows(b_col)
        log_w = b_last_rows - b_col + i_col
        m_start, m_next, keep = [], [], []
        m_c = m_ref[hh, :, 0:1]
        for c in range(n_chunks):
            sl = slice(c * CHUNK, (c + 1) * CHUNK)
            log_keep = b_last_rows[c * CHUNK:c * CHUNK + 1, :] + m_c
            m_n = jnp.maximum(log_keep, jnp.max(log_w[sl], axis=0, keepdims=True))
            m_start.append(m_c)
            m_next.append(m_n)
            keep.append(jnp.exp(log_keep - m_n))
            m_c = m_n
        m_ref[hh] = jnp.broadcast_to(m_c, (1, LANES))
        log_intra = jnp.where(mask, b_col - row(bcum_t, f_idx) + row(gates_t, i_idx), -jnp.inf)
        log_inter = b_col + per_chunk(m_start)
        m_row = jnp.maximum(log_inter, jnp.max(log_intra, axis=1, keepdims=True))
        w_intra = jnp.exp(log_intra - m_row)
        w_inter = jnp.exp(log_inter - m_row)
        qb, kb, vb = q_ref[:, hs], k_ref[:, hs], v_ref[:, hs]
        q, k = qb.astype(F32), kb.astype(F32)
        s = lax.dot_general(qb, kb, (((1,), (1,)), ((), ())), preferred_element_type=F32) * w_intra
        num_intra = jnp.dot(s.astype(BF16), vb, preferred_element_type=F32)
        den_intra = jnp.sum(s, axis=1, keepdims=True)
        wk = jnp.exp(log_w - per_chunk(m_next)) * k
        wkb = wk.astype(BF16)
        cmat, nvec = c_ref[hh], n_ref[hh]
        num_inter, den_inter = [], []
        for c in range(n_chunks):
            sl = slice(c * CHUNK, (c + 1) * CHUNK)
            num_inter.append(jnp.dot(qb[sl], cmat.astype(BF16), preferred_element_type=F32))
            den_inter.append(jnp.sum(q[sl] * nvec, axis=1, keepdims=True))
            upd = lax.dot_general(wkb[sl], vb[sl], (((0,), (0,)), ((), ())), preferred_element_type=F32)
            cmat = keep[c] * cmat + upd
            nvec = keep[c] * nvec + jnp.sum(wk[sl], axis=0, keepdims=True)
        c_ref[hh] = cmat
        n_ref[hh] = nvec
        num = w_inter * jnp.concatenate(num_inter, axis=0) + num_intra
        den = w_inter * jnp.concatenate(den_inter, axis=0) + den_intra
        hc = num / jnp.maximum(jnp.abs(den), jnp.exp(-m_row))
        z = z_ref[:, hs]
        o_ref[:, hs] = ((_row_norm(hc, True) * gn_ref[hh] + skip_ref[:, hs] * xc_ref[:, hs])
                        * (z * jax.nn.sigmoid(z)))


def _mlstm_mixer(proj, conv_w, conv_b, w_q, w_k, w_v, w_gates, b_gates, gn_g, skip, bsz, seq):
    t = proj.shape[0]
    inner = proj.shape[1] // 2
    n_heads, dh = gn_g.shape
    tm = MLSTM_PRE_TILE
    n_g = w_gates.shape[1]
    wg = jnp.pad(w_gates, ((0, 0), (0, LANES - n_g))).astype(BF16)
    bg = jnp.pad(b_gates.reshape(1, n_g), ((0, 0), (0, LANES - n_g)))
    tile_spec = pl.BlockSpec((tm, inner), lambda i: (i, 0))
    bd_spec = pl.BlockSpec((inner // LANES, LANES, LANES), lambda i: (0, 0, 0))
    q, k, v, xc, gates = pl.pallas_call(
        functools.partial(_mlstm_pre_kernel, per=seq // tm, k_scale=dh ** -0.5),
        grid=(t // tm,),
        in_specs=[tile_spec,
                  pl.BlockSpec((MLSTM_CONV, inner), lambda i: (0, 0)),
                  pl.BlockSpec((1, inner), lambda i: (0, 0)),
                  bd_spec, bd_spec, bd_spec,
                  pl.BlockSpec((3 * inner, LANES), lambda i: (0, 0)),
                  pl.BlockSpec((1, LANES), lambda i: (0, 0))],
        out_specs=[tile_spec, tile_spec, tile_spec, tile_spec, pl.BlockSpec((tm, LANES), lambda i: (i, 0))],
        out_shape=[jax.ShapeDtypeStruct((t, inner), BF16)] * 3
        + [jax.ShapeDtypeStruct((t, inner), F32), jax.ShapeDtypeStruct((t, LANES), F32)],
        scratch_shapes=[pltpu.VMEM((tm + CONV_PAD, inner), F32)],
        compiler_params=_cparams("arbitrary"),
        name="mlstm_pre",
    )(proj, conv_w, conv_b.reshape(1, inner), _blockdiag_tiles(w_q), _blockdiag_tiles(w_k),
      _blockdiag_tiles(w_v), wg, bg)

    lb = SEQ_BLOCK
    hps = HEADS_PER_STEP
    n_hp = n_heads // hps
    r3 = lambda a: a.reshape(bsz, seq, a.shape[-1])
    head_spec = pl.BlockSpec((None, lb, hps * dh), lambda b, h, s: (b, s, h))
    out = pl.pallas_call(
        functools.partial(_mlstm_rec_kernel, n_heads=n_heads),
        grid=(bsz, n_hp, seq // lb),
        in_specs=[head_spec, head_spec, head_spec,
                  pl.BlockSpec((None, lb, LANES), lambda b, h, s: (b, s, 0)),
                  head_spec,
                  pl.BlockSpec((None, lb, hps * dh), lambda b, h, s: (b, s, n_hp + h)),
                  pl.BlockSpec((hps, 1, dh), lambda b, h, s: (h, 0, 0)),
                  pl.BlockSpec((1, hps * dh), lambda b, h, s: (0, h))],
        out_specs=head_spec,
        out_shape=jax.ShapeDtypeStruct((bsz, seq, inner), F32),
        scratch_shapes=[pltpu.VMEM((hps, dh, dh), F32), pltpu.VMEM((hps, 1, dh), F32),
                        pltpu.VMEM((hps, 1, LANES), F32)],
        compiler_params=_cparams("arbitrary", "arbitrary", "arbitrary"),
        name="mlstm_rec",
    )(r3(q), r3(k), r3(v), r3(gates), r3(xc), r3(proj), gn_g.reshape(n_heads, 1, dh), skip.reshape(1, inner))
    return out.reshape(t, inner)


def kernel(x, c, ada_w, ada_b, norm_mix_g, norm_ffn_g, norm_final_g, ret_w_in, ret_gn_g, ret_w_out, att_w_in, att_rel_bias, att_w_out, gla_w_in, gla_w_gate, gla_b_gate, gla_gn_g, gla_w_out, mlstm_w_in, mlstm_conv_w, mlstm_conv_b, mlstm_w_q, mlstm_w_k, mlstm_w_v, mlstm_w_gates, mlstm_b_gates, mlstm_gn_g, mlstm_skip, mlstm_w_out, router_w, router_b, moe_w1, moe_b1, moe_w2, moe_b2):
    bsz, seq, d = x.shape
    depth, n_exp, _, dh2 = moe_w1.shape
    xt = x.reshape(bsz * seq, d)
    mod = _ada_mod(c, ada_w, ada_b)
    w1_bf = moe_w1.astype(BF16).reshape(depth * n_exp, d, dh2)
    w2_bf = moe_w2.astype(BF16).reshape(depth * n_exp, dh2 // 2, d)
    b1_all = moe_b1.reshape(depth * n_exp, 1, dh2)
    b2_all = moe_b2.reshape(depth * n_exp, 1, d)
    for i in range(depth):
        kind, j = i % 4, i // 4
        sh_a, sc_a, g_a, sh_f, sc_f, g_f = jnp.split(mod[i], 6, axis=-1)
        if kind == 0:
            proj = _norm_mm(xt, norm_mix_g[i], sc_a, sh_a, ret_w_in[j].astype(BF16), seq)
            y = _retention_mixer(proj, ret_gn_g[j], bsz, seq)
            w_out = ret_w_out[j]
        elif kind == 1:
            proj = _norm_mm(xt, norm_mix_g[i], sc_a, sh_a, att_w_in[j].astype(BF16), seq)
            y = _attention_mixer(proj, att_rel_bias[j], bsz, seq)
            w_out = att_w_out[j]
        elif kind == 2:
            n_heads, dv = gla_gn_g[j].shape
            w_gla = _gla_proj_weight(gla_w_in[j], n_heads * dv // 2, n_heads * dv).astype(BF16)
            proj = _norm_mm(xt, norm_mix_g[i], sc_a, sh_a, w_gla, seq, tn=w_gla.shape[1] // 5)
            y = _gla_mixer(proj, gla_w_gate[j], gla_b_gate[j], gla_gn_g[j], bsz, seq)
            w_out = gla_w_out[j]
        else:
            proj = _norm_mm(xt, norm_mix_g[i], sc_a, sh_a, mlstm_w_in[j].astype(BF16), seq)
            y = _mlstm_mixer(proj, mlstm_conv_w[j], mlstm_conv_b[j], mlstm_w_q[j], mlstm_w_k[j], mlstm_w_v[j],
                             mlstm_w_gates[j], mlstm_b_gates[j], mlstm_gn_g[j], mlstm_skip[j], bsz, seq)
            w_out = mlstm_w_out[j]
        xt = _mm_res(y, w_out.astype(BF16), xt, g_a, seq)
        xt = _moe_layer(xt, norm_ffn_g[i], sc_f, sh_f, g_f, router_w[i], router_b[i],
                        w1_bf, b1_all, w2_bf, b2_all, i, seq)
    return _final_norm(xt, norm_final_g).reshape(bsz, seq, d)
```

```python
import functools

import jax
import jax.numpy as jnp
from jax import lax
from jax.experimental import pallas as pl
from jax.experimental.pallas import tpu as pltpu

F32 = jnp.float32
BF16 = jnp.bfloat16
HI = lax.Precision.HIGHEST

CHUNK = 64
EPS = 1e-6
ROPE_BASE = 10000.0
ATT_LEFT_CHUNKS = 8
ATT_MAX_REL = 256
GLA_GATE_RANK = 16
GLA_GATE_NORM = 16.0
MLSTM_CONV = 4
N_EXPERTS = 32
TOP_K = 4
SWIGLU_LIMIT = 7.0
SWIGLU_ALPHA = 1.702

VMEM_LIMIT_BYTES = 56 * 1024 * 1024
LANES = 128
SUBLANES = 8
MOE_TILE = 512
COMBINE_TILE = 256
SEQ_BLOCK = 512
MLSTM_PRE_TILE = 256
CONV_PAD = 8
ATT_QBLOCK = 2 * CHUNK
ATT_LEFT = ATT_LEFT_CHUNKS * CHUNK
ATT_WINDOW = ATT_LEFT + ATT_QBLOCK


def _cparams(*sem):
    return pltpu.CompilerParams(dimension_semantics=sem, vmem_limit_bytes=VMEM_LIMIT_BYTES)


def _dot_bf16(a, b, dims=(((1,), (0,)), ((), ()))):
    return lax.dot_general(a.astype(BF16), b.astype(BF16), dims, preferred_element_type=F32)


def _tril_ones(n):
    row = lax.broadcasted_iota(jnp.int32, (n, n), 0)
    col = lax.broadcasted_iota(jnp.int32, (n, n), 1)
    return row >= col


def _row_norm(x, center):
    if center:
        x = x - jnp.mean(x, axis=-1, keepdims=True)
    return x * lax.rsqrt(jnp.mean(x * x, axis=-1, keepdims=True) + EPS)


def _load_row_tiles(ref, rows):
    return jnp.concatenate([ref[pl.ds(j, rows, stride=SUBLANES), :] for j in range(SUBLANES)], axis=1)


def _store_row_tiles(ref, val):
    rows = val.shape[0]
    for j in range(SUBLANES):
        ref[pl.ds(j, rows, stride=SUBLANES), :] = val[:, j * LANES:(j + 1) * LANES]


def _ada_kernel(c_ref, w_ref, b_ref, o_ref):
    c = c_ref[...]
    cond = c * jax.nn.sigmoid(c)
    o_ref[0] = jnp.dot(cond, w_ref[0], preferred_element_type=F32, precision=HI) + b_ref[0]


def _ada_mod(c, ada_w, ada_b):
    depth, d, n = ada_w.shape
    b = c.shape[0]
    tn = 1536
    return pl.pallas_call(
        _ada_kernel,
        grid=(depth, n // tn),
        in_specs=[pl.BlockSpec((b, d), lambda l, j: (0, 0)),
                  pl.BlockSpec((1, d, tn), lambda l, j: (l, 0, j)),
                  pl.BlockSpec((1, 1, tn), lambda l, j: (l, 0, j))],
        out_specs=pl.BlockSpec((1, b, tn), lambda l, j: (l, 0, j)),
        out_shape=jax.ShapeDtypeStruct((depth, b, n), F32),
        compiler_params=_cparams("arbitrary", "arbitrary"),
        name="ada_mod",
    )(c, ada_w, ada_b.reshape(depth, 1, n))


def _modulated_norm(x, g, sc, sh):
    ms = jnp.mean(x * x, axis=-1, keepdims=True)
    return (x * lax.rsqrt(ms + EPS) * g) * (1.0 + sc) + sh


def _norm_mm_kernel(x_ref, g_ref, sc_ref, sh_ref, w_ref, o_ref, h_scr):
    @pl.when(pl.program_id(1) == 0)
    def _():
        h = _modulated_norm(x_ref[...], g_ref[...], sc_ref[0], sh_ref[0])
        h_scr[...] = h.astype(BF16)

    o_ref[...] = jnp.dot(h_scr[...], w_ref[...], preferred_element_type=F32)


def _norm_mm(x, g, sc, sh, w, seq, tm=1024, tn=512):
    t, d = x.shape
    n = w.shape[1]
    per = seq // tm
    bsz = sc.shape[0]
    return pl.pallas_call(
        _norm_mm_kernel,
        grid=(t // tm, n // tn),
        in_specs=[pl.BlockSpec((tm, d), lambda i, j: (i, 0)),
                  pl.BlockSpec((1, d), lambda i, j: (0, 0)),
                  pl.BlockSpec((1, 1, d), lambda i, j: (i // per, 0, 0)),
                  pl.BlockSpec((1, 1, d), lambda i, j: (i // per, 0, 0)),
                  pl.BlockSpec((d, tn), lambda i, j: (0, j))],
        out_specs=pl.BlockSpec((tm, tn), lambda i, j: (i, j)),
        out_shape=jax.ShapeDtypeStruct((t, n), F32),
        scratch_shapes=[pltpu.VMEM((tm, d), BF16)],
        compiler_params=_cparams("arbitrary", "arbitrary"),
        name="norm_mm",
    )(x, g.reshape(1, d), sc.reshape(bsz, 1, d), sh.reshape(bsz, 1, d), w)


def _mm_res_kernel(y_ref, w_ref, x_ref, gate_ref, o_ref):
    acc = jnp.dot(y_ref[...].astype(BF16), w_ref[...], preferred_element_type=F32)
    o_ref[...] = x_ref[...] + gate_ref[0] * acc


def _mm_res(y, w, x, gate, seq, tm=512):
    t, k = y.shape
    d = w.shape[1]
    per = seq // tm
    bsz = gate.shape[0]
    return pl.pallas_call(
        _mm_res_kernel,
        grid=(t // tm,),
        in_specs=[pl.BlockSpec((tm, k), lambda i: (i, 0)),
                  pl.BlockSpec((k, d), lambda i: (0, 0)),
                  pl.BlockSpec((tm, d), lambda i: (i, 0)),
                  pl.BlockSpec((1, 1, d), lambda i: (i // per, 0, 0))],
        out_specs=pl.BlockSpec((tm, d), lambda i: (i, 0)),
        out_shape=jax.ShapeDtypeStruct((t, d), F32),
        compiler_params=_cparams("arbitrary"),
        name="mm_res",
    )(y, w, x, gate.reshape(bsz, 1, d))


def _final_norm_kernel(x_ref, g_ref, o_ref):
    x = x_ref[...]
    ms = jnp.mean(x * x, axis=-1, keepdims=True)
    o_ref[...] = x * lax.rsqrt(ms + EPS) * g_ref[...]


def _final_norm(x, g, tm=1024):
    t, d = x.shape
    return pl.pallas_call(
        _final_norm_kernel,
        grid=(t // tm,),
        in_specs=[pl.BlockSpec((tm, d), lambda i: (i, 0)),
                  pl.BlockSpec((1, d), lambda i: (0, 0))],
        out_specs=pl.BlockSpec((tm, d), lambda i: (i, 0)),
        out_shape=jax.ShapeDtypeStruct((t, d), F32),
        compiler_params=_cparams("arbitrary"),
        name="final_norm",
    )(x, g.reshape(1, d))


def _norm_router_kernel(x_ref, g_ref, sc_ref, sh_ref, rw_ref, rb_ref, h_ref, lg_ref):
    h = _modulated_norm(x_ref[...], g_ref[...], sc_ref[0], sh_ref[0])
    _store_row_tiles(h_ref, h)
    lg_ref[...] = jnp.dot(h, rw_ref[...], preferred_element_type=F32, precision=HI) + rb_ref[...]


def _norm_router(x, g, sc, sh, rw, rb, seq, tm=512):
    t, d = x.shape
    per = seq // tm
    bsz = sc.shape[0]
    e = rw.shape[1]
    rw_p = jnp.pad(rw, ((0, 0), (0, LANES - e)))
    rb_p = jnp.pad(rb.reshape(1, e), ((0, 0), (0, LANES - e)))
    return pl.pallas_call(
        _norm_router_kernel,
        grid=(t // tm,),
        in_specs=[pl.BlockSpec((tm, d), lambda i: (i, 0)),
                  pl.BlockSpec((1, d), lambda i: (0, 0)),
                  pl.BlockSpec((1, 1, d), lambda i: (i // per, 0, 0)),
                  pl.BlockSpec((1, 1, d), lambda i: (i // per, 0, 0)),
                  pl.BlockSpec((d, LANES), lambda i: (0, 0)),
                  pl.BlockSpec((1, LANES), lambda i: (0, 0))],
        out_specs=[pl.BlockSpec((tm * SUBLANES, LANES), lambda i: (i, 0)),
                   pl.BlockSpec((tm, LANES), lambda i: (i, 0))],
        out_shape=[jax.ShapeDtypeStruct((t * SUBLANES, LANES), F32),
                   jax.ShapeDtypeStruct((t, LANES), F32)],
        compiler_params=_cparams("arbitrary"),
        name="norm_router",
    )(x, g.reshape(1, d), sc.reshape(bsz, 1, d), sh.reshape(bsz, 1, d), rw_p, rb_p)


def _row_tile_copy(src_hbm, src_row8, dst_vmem, dst_row, sem):
    return pltpu.make_async_copy(
        src_hbm.at[pl.ds(pl.multiple_of(src_row8, SUBLANES), SUBLANES)],
        dst_vmem.at[pl.ds(pl.multiple_of(dst_row * SUBLANES, SUBLANES), SUBLANES)], sem)


def _expert_kernel(te_ref, nt_ref, idx_cur_ref, idx_nxt_ref, h_hbm, w1_ref, b1_ref, w2_ref, b2_ref,
                   o_ref, xbuf, w1_bf, w2_bf, sem):
    i = pl.program_id(0)
    n_used = nt_ref[0]
    tm = xbuf.shape[1] // SUBLANES
    slot = i % 2

    @pl.when(jnp.logical_or(i == 0, te_ref[i] != te_ref[jnp.maximum(i - 1, 0)]))
    def _():
        w1_bf[...] = w1_ref[0].astype(BF16)
        w2_bf[...] = w2_ref[0].astype(BF16)

    def issue(idx_ref, s):
        def body(r8, carry):
            for u in range(SUBLANES):
                r = r8 * SUBLANES + u
                _row_tile_copy(h_hbm, idx_ref[0, 0, r], xbuf.at[s], r, sem.at[s]).start(priority=u % 2)
            return carry
        lax.fori_loop(0, tm // SUBLANES, body, 0)

    @pl.when(jnp.logical_and(i == 0, n_used > 0))
    def _():
        issue(idx_cur_ref, 0)

    @pl.when(i + 1 < n_used)
    def _():
        issue(idx_nxt_ref, 1 - slot)

    @pl.when(i < n_used)
    def _():
        pltpu.make_async_copy(h_hbm.at[pl.ds(0, tm * SUBLANES)], xbuf.at[slot], sem.at[slot]).wait()
        x = _load_row_tiles(xbuf.at[slot], tm).astype(BF16)
        hid = jnp.dot(x, w1_bf[...], preferred_element_type=F32) + b1_ref[0]
        de = hid.shape[1] // 2
        gate = jnp.minimum(hid[:, :de], SWIGLU_LIMIT)
        lin = jnp.clip(hid[:, de:], -SWIGLU_LIMIT, SWIGLU_LIMIT)
        act = gate * jax.nn.sigmoid(SWIGLU_ALPHA * gate) * (lin + 1.0)
        y = jnp.dot(act.astype(BF16), w2_bf[...], preferred_element_type=F32) + b2_ref[0]
        _store_row_tiles(o_ref, y)

    @pl.when(i >= n_used)
    def _():
        o_ref[...] = jnp.zeros_like(o_ref)


def _expert_ffn(h_tiles, src_row8, tile_expert, n_tiles_used, w1, b1, w2, b2, layer):
    p = src_row8.shape[0]
    tm = MOE_TILE
    nt = p // tm
    _, d, dh = w1.shape
    idx3 = src_row8.reshape(nt, 1, tm)
    base = layer * N_EXPERTS
    grid_spec = pltpu.PrefetchScalarGridSpec(
        num_scalar_prefetch=2,
        grid=(nt,),
        in_specs=[
            pl.BlockSpec((1, 1, tm), lambda i, te, nu: (i, 0, 0), memory_space=pltpu.SMEM),
            pl.BlockSpec((1, 1, tm), lambda i, te, nu: (jnp.minimum(i + 1, nt - 1), 0, 0),
                         memory_space=pltpu.SMEM),
            pl.BlockSpec(memory_space=pl.ANY),
            pl.BlockSpec((1, d, dh), lambda i, te, nu: (base + te[i], 0, 0)),
            pl.BlockSpec((1, 1, dh), lambda i, te, nu: (base + te[i], 0, 0)),
            pl.BlockSpec((1, dh // 2, d), lambda i, te, nu: (base + te[i], 0, 0)),
            pl.BlockSpec((1, 1, d), lambda i, te, nu: (base + te[i], 0, 0)),
        ],
        out_specs=pl.BlockSpec((tm * SUBLANES, LANES), lambda i, te, nu: (i, 0)),
        scratch_shapes=[pltpu.VMEM((2, tm * SUBLANES, LANES), F32), pltpu.VMEM((d, dh), BF16),
                        pltpu.VMEM((dh // 2, d), BF16), pltpu.SemaphoreType.DMA((2,))],
    )
    return pl.pallas_call(
        _expert_kernel,
        grid_spec=grid_spec,
        out_shape=jax.ShapeDtypeStruct((p * SUBLANES, LANES), F32),
        compiler_params=_cparams("arbitrary"),
        name="expert_ffn",
    )(tile_expert, n_tiles_used, idx3, idx3, h_tiles, w1, b1, w2, b2)


def _combine_kernel(idx_cur_ref, idx_nxt_ref, y_hbm, x_ref, w_ref, gate_ref, o_ref, ybuf, sem):
    i = pl.program_id(0)
    n = pl.num_programs(0)
    tc = x_ref.shape[0]
    slot = i % 2

    def issue(idx_ref, s):
        def body(r2, carry):
            for u in range(2):
                r = r2 * 2 + u
                for k in range(TOP_K):
                    _row_tile_copy(y_hbm, idx_ref[0, 0, r * TOP_K + k], ybuf.at[s, k], r,
                                   sem.at[s]).start(priority=k % 2)
            return carry
        lax.fori_loop(0, tc // 2, body, 0)

    @pl.when(i == 0)
    def _():
        issue(idx_cur_ref, 0)

    @pl.when(i + 1 < n)
    def _():
        issue(idx_nxt_ref, 1 - slot)

    for k in range(TOP_K):
        pltpu.make_async_copy(y_hbm.at[pl.ds(0, tc * SUBLANES)], ybuf.at[slot, k], sem.at[slot]).wait()
    w = w_ref[...]
    acc = w[:, 0:1] * _load_row_tiles(ybuf.at[slot, 0], tc)
    for k in range(1, TOP_K):
        acc = acc + w[:, k:k + 1] * _load_row_tiles(ybuf.at[slot, k], tc)
    o_ref[...] = x_ref[...] + gate_ref[0] * acc


def _moe_combine(y_tiles, slot8_of_pair, top_w, x, gate, seq):
    t, d = x.shape
    tc = COMBINE_TILE
    nt = t // tc
    per = seq // tc
    bsz = gate.shape[0]
    idx3 = slot8_of_pair.reshape(nt, 1, tc * TOP_K)
    return pl.pallas_call(
        _combine_kernel,
        grid=(nt,),
        in_specs=[
            pl.BlockSpec((1, 1, tc * TOP_K), lambda i: (i, 0, 0), memory_space=pltpu.SMEM),
            pl.BlockSpec((1, 1, tc * TOP_K), lambda i: (jnp.minimum(i + 1, nt - 1), 0, 0),
                         memory_space=pltpu.SMEM),
            pl.BlockSpec(memory_space=pl.ANY),
            pl.BlockSpec((tc, d), lambda i: (i, 0)),
            pl.BlockSpec((tc, TOP_K), lambda i: (i, 0)),
            pl.BlockSpec((1, 1, d), lambda i: (i // per, 0, 0)),
        ],
        out_specs=pl.BlockSpec((tc, d), lambda i: (i, 0)),
        out_shape=jax.ShapeDtypeStruct((t, d), F32),
        scratch_shapes=[pltpu.VMEM((2, TOP_K, tc * SUBLANES, LANES), F32), pltpu.SemaphoreType.DMA((2,))],
        compiler_params=_cparams("arbitrary"),
        name="moe_combine",
    )(idx3, idx3, y_tiles, x, top_w, gate.reshape(bsz, 1, d))


def _route(logits):
    t = logits.shape[0]
    top_v, top_i = lax.top_k(logits[:, :N_EXPERTS], TOP_K)
    top_w = jax.nn.softmax(top_v, axis=-1)
    flat_e = top_i.reshape(-1).astype(jnp.int32)
    onehot = (flat_e[:, None] == jnp.arange(N_EXPERTS, dtype=jnp.int32)[None, :]).astype(jnp.int32)
    csum = jnp.cumsum(onehot, axis=0)
    counts = csum[-1]
    padded = ((counts + MOE_TILE - 1) // MOE_TILE) * MOE_TILE
    ends = jnp.cumsum(padded)
    starts = ends - padded
    slot_of_pair = jnp.sum(onehot * (csum - 1 + starts[None, :]), axis=1)
    n_fill = N_EXPERTS * MOE_TILE
    fill_end = jnp.cumsum(padded - counts)
    fill_key = jnp.sum((jnp.arange(n_fill, dtype=jnp.int32)[:, None] >= fill_end[None, :]).astype(jnp.int32), axis=1)
    keys = jnp.concatenate([flat_e, fill_key])
    toks = jnp.concatenate([jnp.arange(t * TOP_K, dtype=jnp.int32) // TOP_K, jnp.zeros((n_fill,), jnp.int32)])
    _, src_tok = lax.sort((keys, toks), num_keys=1, is_stable=True)
    n_tiles = (t * TOP_K + n_fill) // MOE_TILE
    tile_start = jnp.arange(n_tiles, dtype=jnp.int32) * MOE_TILE
    tile_expert = jnp.sum((tile_start[:, None] >= ends[None, :]).astype(jnp.int32), axis=1)
    n_used = (ends[-1] // MOE_TILE).astype(jnp.int32)
    last_e = jnp.sum(jnp.where(jnp.arange(n_tiles) == n_used - 1, tile_expert, 0))
    tile_expert = jnp.where(jnp.arange(n_tiles) < n_used, tile_expert, last_e).astype(jnp.int32)
    return (top_w, (slot_of_pair * SUBLANES).astype(jnp.int32), (src_tok * SUBLANES).astype(jnp.int32),
            tile_expert, n_used.reshape(1))


def _moe_layer(x, g, sc, sh, gate, rw, rb, w1, b1, w2, b2, layer, seq):
    h_tiles, logits = _norm_router(x, g, sc, sh, rw, rb, seq)
    top_w, slot8_of_pair, src_row8, tile_expert, n_used = _route(logits)
    y_tiles = _expert_ffn(h_tiles, src_row8, tile_expert, n_used, w1, b1, w2, b2, layer)
    return _moe_combine(y_tiles, slot8_of_pair, top_w, x, gate, seq)


def _chunk_pos(shape):
    return lax.broadcasted_iota(jnp.int32, shape, 0) & (CHUNK - 1)


def _inchunk_cumsum(x):
    pos = _chunk_pos(x.shape)
    sh = 1
    while sh < CHUNK:
        x = x + jnp.where(pos >= sh, pltpu.roll(x, sh, axis=0), 0.0)
        sh *= 2
    return x


def _chunk_last_rows(b):
    rows, d = b.shape
    b3 = b.reshape(rows // CHUNK, CHUNK, d)
    return jnp.broadcast_to(b3[:, CHUNK - 1:CHUNK, :], b3.shape).reshape(rows, d)


def _chunk_causal_mask(rows):
    r = lax.broadcasted_iota(jnp.int32, (rows, rows), 0)
    c = lax.broadcasted_iota(jnp.int32, (rows, rows), 1)
    return jnp.logical_and(r >= c, (r // CHUNK) == (c // CHUNK))


def _linattn_block(q, k, v, b, b_rem, chunk_decay, state_ref):
    rows = q.shape[0]
    q_in = (q * jnp.exp(b)).astype(BF16)
    k_in = (k * jnp.exp(-b)).astype(BF16)
    k_st = (k * jnp.exp(b_rem)).astype(BF16)
    vb = v.astype(BF16)
    scores = lax.dot_general(q_in, k_in, (((1,), (1,)), ((), ())), preferred_element_type=F32)
    scores = jnp.where(_chunk_causal_mask(rows), scores, 0.0)
    o_intra = jnp.dot(scores.astype(BF16), vb, preferred_element_type=F32)
    state = state_ref[...]
    o_inter = []
    for c in range(rows // CHUNK):
        sl = slice(c * CHUNK, (c + 1) * CHUNK)
        o_inter.append(jnp.dot(q_in[sl], state.astype(BF16), preferred_element_type=F32))
        upd = lax.dot_general(k_st[sl], vb[sl], (((0,), (0,)), ((), ())), preferred_element_type=F32)
        state = chunk_decay(c) * state + upd
    state_ref[...] = state
    return o_intra + jnp.concatenate(o_inter, axis=0)


HEADS_PER_STEP = 2


def _ret_kernel(q_ref, k_ref, v_ref, g_ref, cos_ref, sin_ref, lg_ref, gn_ref, o_ref, state_ref, *, k_scale):
    @pl.when(pl.program_id(2) == 0)
    def _():
        state_ref[...] = jnp.zeros_like(state_ref)

    rows = q_ref.shape[0]
    dk = q_ref.shape[1] // HEADS_PER_STEP
    dv = v_ref.shape[1] // HEADS_PER_STEP
    half = dk // 2
    cos, sin = cos_ref[...], sin_ref[...]

    def rope(t):
        t1, t2 = t[:, :half], t[:, half:]
        return jnp.concatenate([t1 * cos - t2 * sin, t1 * sin + t2 * cos], axis=1)

    pos = _chunk_pos((rows, dk)).astype(F32)
    for hh in range(HEADS_PER_STEP):
        ks, vs = slice(hh * dk, (hh + 1) * dk), slice(hh * dv, (hh + 1) * dv)
        lg = lg_ref[hh]
        b = (pos + 1.0) * lg
        b_rem = (CHUNK - 1.0 - pos) * lg
        decay = jnp.exp(CHUNK * lg[:, 0:1])
        q = rope(q_ref[:, ks])
        k = rope(k_ref[:, ks]) * k_scale
        o = _linattn_block(q, k, v_ref[:, vs], b, b_rem, lambda c: decay, state_ref.at[hh])
        g = g_ref[:, vs]
        o_ref[:, vs] = _row_norm(o, True) * gn_ref[hh] * (g * jax.nn.sigmoid(g))


def _retention_mixer(proj, gn_g, bsz, seq):
    n_heads, dv = gn_g.shape
    dk = dv // 2
    lb = SEQ_BLOCK
    hps = HEADS_PER_STEP
    n_hp = n_heads // hps
    p3 = proj.reshape(bsz, seq, proj.shape[-1])
    half = dk // 2
    inv = ROPE_BASE ** (-jnp.arange(half, dtype=F32) / half)
    ang = jnp.arange(seq, dtype=F32)[:, None] * inv[None, :]
    cos, sin = jnp.cos(ang), jnp.sin(ang)
    log_gamma = jnp.log1p(-jnp.exp2(-5.0 - jnp.arange(n_heads, dtype=F32)))
    lg = jnp.broadcast_to(log_gamma[:, None, None], (n_heads, 1, dk))
    out = pl.pallas_call(
        functools.partial(_ret_kernel, k_scale=dk ** -0.5),
        grid=(bsz, n_hp, seq // lb),
        in_specs=[pl.BlockSpec((None, lb, hps * dk), lambda b, h, s: (b, s, h)),
                  pl.BlockSpec((None, lb, hps * dk), lambda b, h, s: (b, s, n_hp + h)),
                  pl.BlockSpec((None, lb, hps * dv), lambda b, h, s: (b, s, n_hp + h)),
                  pl.BlockSpec((None, lb, hps * dv), lambda b, h, s: (b, s, 2 * n_hp + h)),
                  pl.BlockSpec((lb, half), lambda b, h, s: (s, 0)),
                  pl.BlockSpec((lb, half), lambda b, h, s: (s, 0)),
                  pl.BlockSpec((hps, 1, dk), lambda b, h, s: (h, 0, 0)),
                  pl.BlockSpec((hps, 1, dv), lambda b, h, s: (h, 0, 0))],
        out_specs=pl.BlockSpec((None, lb, hps * dv), lambda b, h, s: (b, s, h)),
        out_shape=jax.ShapeDtypeStruct((bsz, seq, n_heads * dv), F32),
        scratch_shapes=[pltpu.VMEM((hps, dk, dv), F32)],
        compiler_params=_cparams("arbitrary", "arbitrary", "arbitrary"),
        name="retention",
    )(p3, p3, p3, p3, cos, sin, lg, gn_g.reshape(n_heads, 1, dv))
    return out.reshape(bsz * seq, n_heads * dv)


def _gla_kernel(q_ref, k_ref, v_ref, r_ref, glow_ref, wg_ref, bg_ref, gn_ref, o_ref, state_ref, *, q_scale):
    @pl.when(pl.program_id(2) == 0)
    def _():
        state_ref[...] = jnp.zeros_like(state_ref)

    dk = q_ref.shape[1] // HEADS_PER_STEP
    dv = v_ref.shape[1] // HEADS_PER_STEP

    rows = q_ref.shape[0]
    z = jnp.dot(glow_ref[...], wg_ref[...], precision=HI, preferred_element_type=F32) + bg_ref[...]
    la_all = jax.nn.log_sigmoid(z) / GLA_GATE_NORM
    chunk_sel = (lax.broadcasted_iota(jnp.int32, (rows, LANES), 0) // CHUNK
                 == lax.broadcasted_iota(jnp.int32, (rows, LANES), 1)).astype(F32)
    for hh in range(HEADS_PER_STEP):
        ks, vs = slice(hh * dk, (hh + 1) * dk), slice(hh * dv, (hh + 1) * dv)
        la = la_all[:, ks]
        b = _inchunk_cumsum(la)
        b_rem = _chunk_last_rows(b) - b
        decay = jnp.exp(lax.dot_general(la, chunk_sel, (((0,), (0,)), ((), ())),
                                        precision=HI, preferred_element_type=F32))
        o = _linattn_block(q_ref[:, ks] * q_scale, k_ref[:, ks], v_ref[:, vs], b, b_rem,
                           lambda c: decay[:, c:c + 1], state_ref.at[hh])
        r = r_ref[:, vs]
        o_ref[:, vs] = _row_norm(o, False) * gn_ref[hh] * (r * jax.nn.sigmoid(r))


def _gla_proj_weight(w_in, dq, dvv):
    q_k_v = w_in[:, :2 * dq + dvv]
    g_low = w_in[:, 2 * dq + dvv:2 * dq + dvv + GLA_GATE_RANK]
    r = w_in[:, 2 * dq + dvv + GLA_GATE_RANK:]
    pad = jnp.zeros((w_in.shape[0], LANES - GLA_GATE_RANK), w_in.dtype)
    return jnp.concatenate([q_k_v, r, g_low, pad], axis=1)


def _gla_mixer(proj, w_gate, b_gate, gn_g, bsz, seq):
    n_heads, dv = gn_g.shape
    dk = dv // 2
    dq, dvv = n_heads * dk, n_heads * dv
    lb = SEQ_BLOCK
    hps = HEADS_PER_STEP
    n_hp = n_heads // hps
    p3 = proj.reshape(bsz, seq, proj.shape[-1])
    wg = jnp.pad(w_gate, ((0, LANES - GLA_GATE_RANK), (0, 0)))
    out = pl.pallas_call(
        functools.partial(_gla_kernel, q_scale=dk ** -0.5),
        grid=(bsz, n_hp, seq // lb),
        in_specs=[pl.BlockSpec((None, lb, hps * dk), lambda b, h, s: (b, s, h)),
                  pl.BlockSpec((None, lb, hps * dk), lambda b, h, s: (b, s, n_hp + h)),
                  pl.BlockSpec((None, lb, hps * dv), lambda b, h, s: (b, s, n_hp + h)),
                  pl.BlockSpec((None, lb, hps * dv), lambda b, h, s: (b, s, 2 * n_hp + h)),
                  pl.BlockSpec((None, lb, LANES), lambda b, h, s: (b, s, (2 * dq + 2 * dvv) // LANES)),
                  pl.BlockSpec((LANES, hps * dk), lambda b, h, s: (0, h)),
                  pl.BlockSpec((1, hps * dk), lambda b, h, s: (0, h)),
                  pl.BlockSpec((hps, 1, dv), lambda b, h, s: (h, 0, 0))],
        out_specs=pl.BlockSpec((None, lb, hps * dv), lambda b, h, s: (b, s, h)),
        out_shape=jax.ShapeDtypeStruct((bsz, seq, dvv), F32),
        scratch_shapes=[pltpu.VMEM((hps, dk, dv), F32)],
        compiler_params=_cparams("arbitrary", "arbitrary", "arbitrary"),
        name="gla",
    )(p3, p3, p3, p3, p3, wg, b_gate.reshape(1, dq), gn_g.reshape(n_heads, 1, dv))
    return out.reshape(bsz * seq, dvv)


def _att_kernel(q_ref, k_ref, v_ref, bias_ref, o_ref, kpad, vpad, *, scale, dh):
    seq, width = q_ref.shape
    n_pair = width // dh
    kpad[0:ATT_LEFT, :] = jnp.zeros((ATT_LEFT, width), F32)
    vpad[0:ATT_LEFT, :] = jnp.zeros((ATT_LEFT, width), F32)
    kpad[ATT_LEFT:ATT_LEFT + seq, :] = k_ref[...]
    vpad[ATT_LEFT:ATT_LEFT + seq, :] = v_ref[...]
    lane = lax.broadcasted_iota(jnp.int32, (ATT_QBLOCK, width), 1)
    jcol = lax.broadcasted_iota(jnp.int32, (n_pair * ATT_QBLOCK, ATT_WINDOW), 1)
    in_head = [jnp.logical_and(lane >= h * dh, lane < (h + 1) * dh) for h in range(n_pair)]

    def block(start, left_edge):
        q = q_ref[pl.ds(start, ATT_QBLOCK), :] * scale
        kb = kpad[pl.ds(start, ATT_WINDOW), :].astype(BF16)
        vb = vpad[pl.ds(start, ATT_WINDOW), :].astype(BF16)
        q2 = jnp.concatenate([jnp.where(m, q, 0.0) for m in in_head], axis=0).astype(BF16)
        s = lax.dot_general(q2, kb, (((1,), (1,)), ((), ())), preferred_element_type=F32) + bias_ref[...]
        if left_edge:
            s = jnp.where(jcol >= ATT_LEFT - start, s, -jnp.inf)
        p = jnp.exp(s - jnp.max(s, axis=-1, keepdims=True))
        o2 = jnp.dot(p.astype(BF16), vb, preferred_element_type=F32) / jnp.sum(p, axis=-1, keepdims=True)
        out = o2[0:ATT_QBLOCK, :]
        for h in range(1, n_pair):
            out = jnp.where(in_head[h], o2[h * ATT_QBLOCK:(h + 1) * ATT_QBLOCK, :], out)
        o_ref[pl.ds(start, ATT_QBLOCK), :] = out

    n_edge = ATT_LEFT // ATT_QBLOCK
    for c in range(n_edge):
        block(c * ATT_QBLOCK, True)

    def body(c, carry):
        block(pl.multiple_of(c * ATT_QBLOCK, ATT_QBLOCK), False)
        return carry

    lax.fori_loop(n_edge, seq // ATT_QBLOCK, body, 0, unroll=4)


def _attention_mixer(proj, rel_bias, bsz, seq):
    d = proj.shape[-1] // 3
    n_heads = rel_bias.shape[0]
    dh = d // n_heads
    n_pair = LANES // dh
    groups = n_heads // n_pair
    qo = jnp.arange(ATT_QBLOCK)[:, None]
    kj = jnp.arange(ATT_WINDOW)[None, :]
    lo = (qo // CHUNK) * CHUNK
    in_window = jnp.logical_and(kj >= lo, kj < lo + ATT_LEFT + CHUNK)
    d_max = ATT_QBLOCK - 1 + ATT_LEFT
    d_min = -(ATT_WINDOW - 1 - ATT_LEFT)
    n_rel = rel_bias.shape[1]
    ext = jnp.concatenate(
        [jnp.broadcast_to(rel_bias[:, :1], (n_heads, -(CHUNK - 1) - d_min)), rel_bias,
         jnp.broadcast_to(rel_bias[:, n_rel - 1:], (n_heads, d_max - ATT_MAX_REL))], axis=1)
    rows = [ext[:, q_ + ATT_LEFT - (ATT_WINDOW - 1) - d_min:q_ + ATT_LEFT - d_min + 1] for q_ in range(ATT_QBLOCK)]
    table = jnp.flip(jnp.stack(rows, axis=1), axis=2).astype(F32)
    bias = jnp.where(in_window[None], table, -jnp.inf)
    bias = bias.reshape(groups, n_pair * ATT_QBLOCK, ATT_WINDOW)
    p3 = proj.reshape(bsz, seq, 3 * d)
    out = pl.pallas_call(
        functools.partial(_att_kernel, scale=dh ** -0.5, dh=dh),
        grid=(bsz, groups),
        in_specs=[pl.BlockSpec((None, seq, LANES), lambda b, g: (b, 0, g)),
                  pl.BlockSpec((None, seq, LANES), lambda b, g: (b, 0, groups + g)),
                  pl.BlockSpec((None, seq, LANES), lambda b, g: (b, 0, 2 * groups + g)),
                  pl.BlockSpec((None, n_pair * ATT_QBLOCK, ATT_WINDOW), lambda b, g: (g, 0, 0))],
        out_specs=pl.BlockSpec((None, seq, LANES), lambda b, g: (b, 0, g)),
        out_shape=jax.ShapeDtypeStruct((bsz, seq, d), F32),
        scratch_shapes=[pltpu.VMEM((ATT_LEFT + seq, LANES), F32), pltpu.VMEM((ATT_LEFT + seq, LANES), F32)],
        compiler_params=_cparams("arbitrary", "arbitrary"),
        name="chunk_attention",
    )(p3, p3, p3, bias)
    return out.reshape(bsz * seq, d)


def _mlstm_pre_kernel(xm_ref, cw_ref, cb_ref, wq_ref, wk_ref, wv_ref, wg_ref, bg_ref,
                      q_ref, k_ref, v_ref, xc_ref, g_ref, xpad, *, per, k_scale):
    i = pl.program_id(0)
    tm, inner = xm_ref.shape

    @pl.when(i % per == 0)
    def _():
        xpad[0:CONV_PAD, :] = jnp.zeros((CONV_PAD, inner), F32)

    @pl.when(i % per != 0)
    def _():
        xpad[0:CONV_PAD, :] = xpad[tm:tm + CONV_PAD, :]

    xm = xm_ref[...]
    xpad[CONV_PAD:CONV_PAD + tm, :] = xm
    acc = jnp.broadcast_to(cb_ref[...], (tm, inner))
    for j in range(MLSTM_CONV):
        off = CONV_PAD - (MLSTM_CONV - 1) + j
        acc = acc + cw_ref[j:j + 1, :] * xpad[off:off + tm, :]
    xc = acc * jax.nn.sigmoid(acc)
    xc_ref[...] = xc

    def blockdiag(t, w_ref):
        tb = t.astype(BF16)
        return jnp.concatenate(
            [jnp.dot(tb[:, g * LANES:(g + 1) * LANES], w_ref[g], preferred_element_type=F32)
             for g in range(inner // LANES)], axis=1)

    q = blockdiag(xc, wq_ref)
    k = blockdiag(xc, wk_ref) * k_scale
    v = blockdiag(xm, wv_ref)
    qb, kb, vb = q.astype(BF16), k.astype(BF16), v.astype(BF16)
    q_ref[...] = qb
    k_ref[...] = kb
    v_ref[...] = vb
    g_ref[...] = (jnp.dot(qb, wg_ref[0:inner, :], preferred_element_type=F32)
                  + jnp.dot(kb, wg_ref[inner:2 * inner, :], preferred_element_type=F32)
                  + jnp.dot(vb, wg_ref[2 * inner:3 * inner, :], preferred_element_type=F32) + bg_ref[...])


def _blockdiag_tiles(w):
    nb, c, _ = w.shape
    per = LANES // c
    wt = w.reshape(nb // per, per, c, c)
    t = jnp.einsum('gpcd,pq->gpcqd', wt, jnp.eye(per, dtype=w.dtype))
    return t.reshape(nb // per, LANES, LANES).astype(BF16)


def _mlstm_rec_kernel(q_ref, k_ref, v_ref, g_ref, xc_ref, z_ref, gn_ref, skip_ref, o_ref,
                      c_ref, n_ref, m_ref, *, n_heads):
    head0 = pl.program_id(1) * HEADS_PER_STEP
    dh = q_ref.shape[1] // HEADS_PER_STEP

    @pl.when(pl.program_id(2) == 0)
    def _():
        c_ref[...] = jnp.zeros_like(c_ref)
        n_ref[...] = jnp.zeros_like(n_ref)
        m_ref[...] = jnp.zeros_like(m_ref)

    rows = q_ref.shape[0]
    n_chunks = rows // CHUNK
    mask = _chunk_causal_mask(rows)
    lane = lax.broadcasted_iota(jnp.int32, (rows, LANES), 1)
    sub = lax.broadcasted_iota(jnp.int32, (LANES, rows), 0)

    def col(t, idx):
        return jnp.sum(jnp.where(lane == idx, t, 0.0), axis=1, keepdims=True)

    def row(t_t, idx):
        return jnp.sum(jnp.where(sub == idx, t_t, 0.0), axis=0, keepdims=True)

    def per_chunk(vals):
        return jnp.concatenate([jnp.broadcast_to(x, (CHUNK, 1)) for x in vals], axis=0)

    gates = g_ref[...]
    bcum = _inchunk_cumsum(jax.nn.log_sigmoid(gates))
    gates_t, bcum_t = gates.T, bcum.T
    for hh in range(HEADS_PER_STEP):
        hs = slice(hh * dh, (hh + 1) * dh)
        i_idx, f_idx = head0 + hh, n_heads + head0 + hh
        i_col, b_col = col(gates, i_idx), col(bcum, f_idx)
        b_last_rows = _chunk_last_rows(b_col)
        log_w = b_last_rows - b_col + i_col
        m_start, m_next, keep = [], [], []
        m_c = m_ref[hh, :, 0:1]
        for c in range(n_chunks):
            sl = slice(c * CHUNK, (c + 1) * CHUNK)
            log_keep = b_last_rows[c * CHUNK:c * CHUNK + 1, :] + m_c
            m_n = jnp.maximum(log_keep, jnp.max(log_w[sl], axis=0, keepdims=True))
            m_start.append(m_c)
            m_next.append(m_n)
            keep.append(jnp.exp(log_keep - m_n))
            m_c = m_n
        m_ref[hh] = jnp.broadcast_to(m_c, (1, LANES))
        log_intra = jnp.where(mask, b_col - row(bcum_t, f_idx) + row(gates_t, i_idx), -jnp.inf)
        log_inter = b_col + per_chunk(m_start)
        m_row = jnp.maximum(log_inter, jnp.max(log_intra, axis=1, keepdims=True))
        w_intra = jnp.exp(log_intra - m_row)
        w_inter = jnp.exp(log_inter - m_row)
        qb, kb, vb = q_ref[:, hs], k_ref[:, hs], v_ref[:, hs]
        q, k = qb.astype(F32), kb.astype(F32)
        s = lax.dot_general(qb, kb, (((1,), (1,)), ((), ())), preferred_element_type=F32) * w_intra
        num_intra = jnp.dot(s.astype(BF16), vb, preferred_element_type=F32)
        den_intra = jnp.sum(s, axis=1, keepdims=True)
        wk = jnp.exp(log_w - per_chunk(m_next)) * k
        wkb = wk.astype(BF16)
        cmat, nvec = c_ref[hh], n_ref[hh]
        num_inter, den_inter = [], []
        for c in range(n_chunks):
            sl = slice(c * CHUNK, (c + 1) * CHUNK)
            num_inter.append(jnp.dot(qb[sl], cmat.astype(BF16), preferred_element_type=F32))
            den_inter.append(jnp.sum(q[sl] * nvec, axis=1, keepdims=True))
            upd = lax.dot_general(wkb[sl], vb[sl], (((0,), (0,)), ((), ())), preferred_element_type=F32)
            cmat = keep[c] * cmat + upd
            nvec = keep[c] * nvec + jnp.sum(wk[sl], axis=0, keepdims=True)
        c_ref[hh] = cmat
        n_ref[hh] = nvec
        num = w_inter * jnp.concatenate(num_inter, axis=0) + num_intra
        den = w_inter * jnp.concatenate(den_inter, axis=0) + den_intra
        hc = num / jnp.maximum(jnp.abs(den), jnp.exp(-m_row))
        z = z_ref[:, hs]
        o_ref[:, hs] = ((_row_norm(hc, True) * gn_ref[hh] + skip_ref[:, hs] * xc_ref[:, hs])
                        * (z * jax.nn.sigmoid(z)))


def _mlstm_mixer(proj, conv_w, conv_b, w_q, w_k, w_v, w_gates, b_gates, gn_g, skip, bsz, seq):
    t = proj.shape[0]
    inner = proj.shape[1] // 2
    n_heads, dh = gn_g.shape
    tm = MLSTM_PRE_TILE
    n_g = w_gates.shape[1]
    wg = jnp.pad(w_gates, ((0, 0), (0, LANES - n_g))).astype(BF16)
    bg = jnp.pad(b_gates.reshape(1, n_g), ((0, 0), (0, LANES - n_g)))
    tile_spec = pl.BlockSpec((tm, inner), lambda i: (i, 0))
    bd_spec = pl.BlockSpec((inner // LANES, LANES, LANES), lambda i: (0, 0, 0))
    q, k, v, xc, gates = pl.pallas_call(
        functools.partial(_mlstm_pre_kernel, per=seq // tm, k_scale=dh ** -0.5),
        grid=(t // tm,),
        in_specs=[tile_spec,
                  pl.BlockSpec((MLSTM_CONV, inner), lambda i: (0, 0)),
                  pl.BlockSpec((1, inner), lambda i: (0, 0)),
                  bd_spec, bd_spec, bd_spec,
                  pl.BlockSpec((3 * inner, LANES), lambda i: (0, 0)),
                  pl.BlockSpec((1, LANES), lambda i: (0, 0))],
        out_specs=[tile_spec, tile_spec, tile_spec, tile_spec, pl.BlockSpec((tm, LANES), lambda i: (i, 0))],
        out_shape=[jax.ShapeDtypeStruct((t, inner), BF16)] * 3
        + [jax.ShapeDtypeStruct((t, inner), F32), jax.ShapeDtypeStruct((t, LANES), F32)],
        scratch_shapes=[pltpu.VMEM((tm + CONV_PAD, inner), F32)],
        compiler_params=_cparams("arbitrary"),
        name="mlstm_pre",
    )(proj, conv_w, conv_b.reshape(1, inner), _blockdiag_tiles(w_q), _blockdiag_tiles(w_k),
      _blockdiag_tiles(w_v), wg, bg)

    lb = SEQ_BLOCK
    hps = HEADS_PER_STEP
    n_hp = n_heads // hps
    r3 = lambda a: a.reshape(bsz, seq, a.shape[-1])
    head_spec = pl.BlockSpec((None, lb, hps * dh), lambda b, h, s: (b, s, h))
    out = pl.pallas_call(
        functools.partial(_mlstm_rec_kernel, n_heads=n_heads),
        grid=(bsz, n_hp, seq // lb),
        in_specs=[head_spec, head_spec, head_spec,
                  pl.BlockSpec((None, lb, LANES), lambda b, h, s: (b, s, 0)),
                  head_spec,
                  pl.BlockSpec((None, lb, hps * dh), lambda b, h, s: (b, s, n_hp + h)),
                  pl.BlockSpec((hps, 1, dh), lambda b, h, s: (h, 0, 0)),
                  pl.BlockSpec((1, hps * dh), lambda b, h, s: (0, h))],
        out_specs=head_spec,
        out_shape=jax.ShapeDtypeStruct((bsz, seq, inner), F32),
        scratch_shapes=[pltpu.VMEM((hps, dh, dh), F32), pltpu.VMEM((hps, 1, dh), F32),
                        pltpu.VMEM((hps, 1, LANES), F32)],
        compiler_params=_cparams("arbitrary", "arbitrary", "arbitrary"),
        name="mlstm_rec",
    )(r3(q), r3(k), r3(v), r3(gates), r3(xc), r3(proj), gn_g.reshape(n_heads, 1, dh), skip.reshape(1, inner))
    return out.reshape(t, inner)


def kernel(x, c, ada_w, ada_b, norm_mix_g, norm_ffn_g, norm_final_g, ret_w_in, ret_gn_g, ret_w_out, att_w_in, att_rel_bias, att_w_out, gla_w_in, gla_w_gate, gla_b_gate, gla_gn_g, gla_w_out, mlstm_w_in, mlstm_conv_w, mlstm_conv_b, mlstm_w_q, mlstm_w_k, mlstm_w_v, mlstm_w_gates, mlstm_b_gates, mlstm_gn_g, mlstm_skip, mlstm_w_out, router_w, router_b, moe_w1, moe_b1, moe_w2, moe_b2):
    bsz, seq, d = x.shape
    depth, n_exp, _, dh2 = moe_w1.shape
    xt = x.reshape(bsz * seq, d)
    mod = _ada_mod(c, ada_w, ada_b)
    w1_all = moe_w1.reshape(depth * n_exp, d, dh2)
    w2_all = moe_w2.reshape(depth * n_exp, dh2 // 2, d)
    b1_all = moe_b1.reshape(depth * n_exp, 1, dh2)
    b2_all = moe_b2.reshape(depth * n_exp, 1, d)
    for i in range(depth):
        kind, j = i % 4, i // 4
        sh_a, sc_a, g_a, sh_f, sc_f, g_f = jnp.split(mod[i], 6, axis=-1)
        if kind == 0:
            proj = _norm_mm(xt, norm_mix_g[i], sc_a, sh_a, ret_w_in[j].astype(BF16), seq)
            y = _retention_mixer(proj, ret_gn_g[j], bsz, seq)
            w_out = ret_w_out[j]
        elif kind == 1:
            proj = _norm_mm(xt, norm_mix_g[i], sc_a, sh_a, att_w_in[j].astype(BF16), seq)
            y = _attention_mixer(proj, att_rel_bias[j], bsz, seq)
            w_out = att_w_out[j]
        elif kind == 2:
            n_heads, dv = gla_gn_g[j].shape
            w_gla = _gla_proj_weight(gla_w_in[j], n_heads * dv // 2, n_heads * dv).astype(BF16)
            proj = _norm_mm(xt, norm_mix_g[i], sc_a, sh_a, w_gla, seq, tn=w_gla.shape[1] // 5)
            y = _gla_mixer(proj, gla_w_gate[j], gla_b_gate[j], gla_gn_g[j], bsz, seq)
            w_out = gla_w_out[j]
        else:
            proj = _norm_mm(xt, norm_mix_g[i], sc_a, sh_a, mlstm_w_in[j].astype(BF16), seq)
            y = _mlstm_mixer(proj, mlstm_conv_w[j], mlstm_conv_b[j], mlstm_w_q[j], mlstm_w_k[j], mlstm_w_v[j],
                             mlstm_w_gates[j], mlstm_b_gates[j], mlstm_gn_g[j], mlstm_skip[j], bsz, seq)
            w_out = mlstm_w_out[j]
        xt = _mm_res(y, w_out.astype(BF16), xt, g_a, seq)
        xt = _moe_layer(xt, norm_ffn_g[i], sc_f, sh_f, g_f, router_w[i], router_b[i],
                        w1_all, b1_all, w2_all, b2_all, i, seq)
    return _final_norm(xt, norm_final_g).reshape(bsz, seq, d)
```

```python
import functools

import jax
import jax.numpy as jnp
from jax import lax
from jax.experimental import pallas as pl
from jax.experimental.pallas import tpu as pltpu

F32 = jnp.float32
BF16 = jnp.bfloat16
HI = lax.Precision.HIGHEST

CHUNK = 64
EPS = 1e-6
ROPE_BASE = 10000.0
ATT_LEFT_CHUNKS = 8
ATT_MAX_REL = 256
GLA_GATE_RANK = 16
GLA_GATE_NORM = 16.0
MLSTM_CONV = 4
N_EXPERTS = 32
TOP_K = 4
SWIGLU_LIMIT = 7.0
SWIGLU_ALPHA = 1.702

VMEM_LIMIT_BYTES = 56 * 1024 * 1024
LANES = 128
SUBLANES = 8
MOE_TILE = 512
COMBINE_TILE = 256
COPIES_PER_ITER = 32
SEQ_BLOCK = 512
MLSTM_PRE_TILE = 256
CONV_PAD = 8
ATT_QBLOCK = 2 * CHUNK
ATT_LEFT = ATT_LEFT_CHUNKS * CHUNK
ATT_WINDOW = ATT_LEFT + ATT_QBLOCK


def _cparams(*sem):
    return pltpu.CompilerParams(dimension_semantics=sem, vmem_limit_bytes=VMEM_LIMIT_BYTES)


def _dot_bf16(a, b, dims=(((1,), (0,)), ((), ()))):
    return lax.dot_general(a.astype(BF16), b.astype(BF16), dims, preferred_element_type=F32)


def _tril_ones(n):
    row = lax.broadcasted_iota(jnp.int32, (n, n), 0)
    col = lax.broadcasted_iota(jnp.int32, (n, n), 1)
    return row >= col


def _row_norm(x, center):
    if center:
        x = x - jnp.mean(x, axis=-1, keepdims=True)
    return x * lax.rsqrt(jnp.mean(x * x, axis=-1, keepdims=True) + EPS)


def _load_row_tiles(ref, rows):
    return jnp.concatenate([ref[pl.ds(j, rows, stride=SUBLANES), :] for j in range(SUBLANES)], axis=1)


def _store_row_tiles(ref, val):
    rows = val.shape[0]
    for j in range(SUBLANES):
        ref[pl.ds(j, rows, stride=SUBLANES), :] = val[:, j * LANES:(j + 1) * LANES]


def _ada_kernel(c_ref, w_ref, b_ref, o_ref):
    c = c_ref[...]
    cond = c * jax.nn.sigmoid(c)
    o_ref[0] = jnp.dot(cond, w_ref[0], preferred_element_type=F32, precision=HI) + b_ref[0]


def _ada_mod(c, ada_w, ada_b):
    depth, d, n = ada_w.shape
    b = c.shape[0]
    tn = 1536
    return pl.pallas_call(
        _ada_kernel,
        grid=(depth, n // tn),
        in_specs=[pl.BlockSpec((b, d), lambda l, j: (0, 0)),
                  pl.BlockSpec((1, d, tn), lambda l, j: (l, 0, j)),
                  pl.BlockSpec((1, 1, tn), lambda l, j: (l, 0, j))],
        out_specs=pl.BlockSpec((1, b, tn), lambda l, j: (l, 0, j)),
        out_shape=jax.ShapeDtypeStruct((depth, b, n), F32),
        compiler_params=_cparams("arbitrary", "arbitrary"),
        name="ada_mod",
    )(c, ada_w, ada_b.reshape(depth, 1, n))


def _modulated_norm(x, g, sc, sh):
    ms = jnp.mean(x * x, axis=-1, keepdims=True)
    return (x * lax.rsqrt(ms + EPS) * g) * (1.0 + sc) + sh


def _norm_mm_kernel(x_ref, g_ref, sc_ref, sh_ref, w_ref, o_ref, h_scr):
    @pl.when(pl.program_id(1) == 0)
    def _():
        h = _modulated_norm(x_ref[...], g_ref[...], sc_ref[0], sh_ref[0])
        h_scr[...] = h.astype(BF16)

    o_ref[...] = jnp.dot(h_scr[...], w_ref[...], preferred_element_type=F32)


def _norm_mm(x, g, sc, sh, w, seq, tm=1024, tn=512):
    t, d = x.shape
    n = w.shape[1]
    per = seq // tm
    bsz = sc.shape[0]
    return pl.pallas_call(
        _norm_mm_kernel,
        grid=(t // tm, n // tn),
        in_specs=[pl.BlockSpec((tm, d), lambda i, j: (i, 0)),
                  pl.BlockSpec((1, d), lambda i, j: (0, 0)),
                  pl.BlockSpec((1, 1, d), lambda i, j: (i // per, 0, 0)),
                  pl.BlockSpec((1, 1, d), lambda i, j: (i // per, 0, 0)),
                  pl.BlockSpec((d, tn), lambda i, j: (0, j))],
        out_specs=pl.BlockSpec((tm, tn), lambda i, j: (i, j)),
        out_shape=jax.ShapeDtypeStruct((t, n), F32),
        scratch_shapes=[pltpu.VMEM((tm, d), BF16)],
        compiler_params=_cparams("arbitrary", "arbitrary"),
        name="norm_mm",
    )(x, g.reshape(1, d), sc.reshape(bsz, 1, d), sh.reshape(bsz, 1, d), w)


def _mm_res_kernel(y_ref, w_ref, x_ref, gate_ref, o_ref):
    acc = jnp.dot(y_ref[...].astype(BF16), w_ref[...], preferred_element_type=F32)
    o_ref[...] = x_ref[...] + gate_ref[0] * acc


def _mm_res(y, w, x, gate, seq, tm=512):
    t, k = y.shape
    d = w.shape[1]
    per = seq // tm
    bsz = gate.shape[0]
    return pl.pallas_call(
        _mm_res_kernel,
        grid=(t // tm,),
        in_specs=[pl.BlockSpec((tm, k), lambda i: (i, 0)),
                  pl.BlockSpec((k, d), lambda i: (0, 0)),
                  pl.BlockSpec((tm, d), lambda i: (i, 0)),
                  pl.BlockSpec((1, 1, d), lambda i: (i // per, 0, 0))],
        out_specs=pl.BlockSpec((tm, d), lambda i: (i, 0)),
        out_shape=jax.ShapeDtypeStruct((t, d), F32),
        compiler_params=_cparams("arbitrary"),
        name="mm_res",
    )(y, w, x, gate.reshape(bsz, 1, d))


def _final_norm_kernel(x_ref, g_ref, o_ref):
    x = x_ref[...]
    ms = jnp.mean(x * x, axis=-1, keepdims=True)
    o_ref[...] = x * lax.rsqrt(ms + EPS) * g_ref[...]


def _final_norm(x, g, tm=1024):
    t, d = x.shape
    return pl.pallas_call(
        _final_norm_kernel,
        grid=(t // tm,),
        in_specs=[pl.BlockSpec((tm, d), lambda i: (i, 0)),
                  pl.BlockSpec((1, d), lambda i: (0, 0))],
        out_specs=pl.BlockSpec((tm, d), lambda i: (i, 0)),
        out_shape=jax.ShapeDtypeStruct((t, d), F32),
        compiler_params=_cparams("arbitrary"),
        name="final_norm",
    )(x, g.reshape(1, d))


def _norm_router_kernel(x_ref, g_ref, sc_ref, sh_ref, rw_ref, rb_ref, h_ref, route_ref, *, n_experts):
    h = _modulated_norm(x_ref[...], g_ref[...], sc_ref[0], sh_ref[0])
    _store_row_tiles(h_ref, h)
    logits = jnp.dot(h, rw_ref[...], preferred_element_type=F32, precision=HI) + rb_ref[...]
    lane = lax.broadcasted_iota(jnp.int32, logits.shape, 1)
    rem = jnp.where(lane < n_experts, logits, -jnp.inf)
    vals, idxs = [], []
    for _ in range(TOP_K):
        m = jnp.max(rem, axis=1, keepdims=True)
        idx = jnp.min(jnp.where(rem == m, lane, LANES), axis=1, keepdims=True)
        vals.append(m)
        idxs.append(idx)
        rem = jnp.where(lane == idx, -jnp.inf, rem)
    ex = [jnp.exp(v - vals[0]) for v in vals]
    inv = 1.0 / sum(ex[1:], ex[0])
    out = jnp.zeros(logits.shape, F32)
    for k in range(TOP_K):
        out = jnp.where(lane == k, ex[k] * inv, out)
        out = jnp.where(lane == TOP_K + k, idxs[k].astype(F32), out)
    route_ref[...] = out


def _norm_router(x, g, sc, sh, rw, rb, seq, tm=512):
    t, d = x.shape
    per = seq // tm
    bsz = sc.shape[0]
    e = rw.shape[1]
    rw_p = jnp.pad(rw, ((0, 0), (0, LANES - e)))
    rb_p = jnp.pad(rb.reshape(1, e), ((0, 0), (0, LANES - e)))
    return pl.pallas_call(
        functools.partial(_norm_router_kernel, n_experts=e),
        grid=(t // tm,),
        in_specs=[pl.BlockSpec((tm, d), lambda i: (i, 0)),
                  pl.BlockSpec((1, d), lambda i: (0, 0)),
                  pl.BlockSpec((1, 1, d), lambda i: (i // per, 0, 0)),
                  pl.BlockSpec((1, 1, d), lambda i: (i // per, 0, 0)),
                  pl.BlockSpec((d, LANES), lambda i: (0, 0)),
                  pl.BlockSpec((1, LANES), lambda i: (0, 0))],
        out_specs=[pl.BlockSpec((tm * SUBLANES, LANES), lambda i: (i, 0)),
                   pl.BlockSpec((tm, LANES), lambda i: (i, 0))],
        out_shape=[jax.ShapeDtypeStruct((t * SUBLANES, LANES), F32),
                   jax.ShapeDtypeStruct((t, LANES), F32)],
        compiler_params=_cparams("arbitrary"),
        name="norm_router",
    )(x, g.reshape(1, d), sc.reshape(bsz, 1, d), sh.reshape(bsz, 1, d), rw_p, rb_p)


def _row_tile_copy(src_hbm, src_row8, dst_vmem, dst_row, sem):
    return pltpu.make_async_copy(
        src_hbm.at[pl.ds(pl.multiple_of(src_row8, SUBLANES), SUBLANES)],
        dst_vmem.at[pl.ds(pl.multiple_of(dst_row * SUBLANES, SUBLANES), SUBLANES)], sem)


def _expert_kernel(te_ref, nt_ref, idx_cur_ref, idx_nxt_ref, h_hbm, w1_ref, b1_ref, w2_ref, b2_ref,
                   o_ref, xbuf, w1_bf, w2_bf, sem):
    i = pl.program_id(0)
    n_used = nt_ref[0]
    tm = xbuf.shape[1] // SUBLANES
    slot = i % 2

    @pl.when(jnp.logical_or(i == 0, te_ref[i] != te_ref[jnp.maximum(i - 1, 0)]))
    def _():
        w1_bf[...] = w1_ref[0].astype(BF16)
        w2_bf[...] = w2_ref[0].astype(BF16)

    def issue(idx_ref, s):
        def body(g, carry):
            for u in range(COPIES_PER_ITER):
                r = g * COPIES_PER_ITER + u
                _row_tile_copy(h_hbm, idx_ref[0, 0, r], xbuf.at[s], r, sem.at[s]).start(priority=u % 2)
            return carry
        lax.fori_loop(0, tm // COPIES_PER_ITER, body, 0)

    @pl.when(jnp.logical_and(i == 0, n_used > 0))
    def _():
        issue(idx_cur_ref, 0)

    @pl.when(i + 1 < n_used)
    def _():
        issue(idx_nxt_ref, 1 - slot)

    @pl.when(i < n_used)
    def _():
        pltpu.make_async_copy(h_hbm.at[pl.ds(0, tm * SUBLANES)], xbuf.at[slot], sem.at[slot]).wait()
        x = _load_row_tiles(xbuf.at[slot], tm).astype(BF16)
        hid = jnp.dot(x, w1_bf[...], preferred_element_type=F32) + b1_ref[0]
        de = hid.shape[1] // 2
        gate = jnp.minimum(hid[:, :de], SWIGLU_LIMIT)
        lin = jnp.clip(hid[:, de:], -SWIGLU_LIMIT, SWIGLU_LIMIT)
        act = gate * jax.nn.sigmoid(SWIGLU_ALPHA * gate) * (lin + 1.0)
        y = jnp.dot(act.astype(BF16), w2_bf[...], preferred_element_type=F32) + b2_ref[0]
        _store_row_tiles(o_ref, y)

    @pl.when(i >= n_used)
    def _():
        o_ref[...] = jnp.zeros_like(o_ref)


def _expert_ffn(h_tiles, src_row8, tile_expert, n_tiles_used, w1, b1, w2, b2, layer):
    p = src_row8.shape[0]
    tm = MOE_TILE
    nt = p // tm
    _, d, dh = w1.shape
    idx3 = src_row8.reshape(nt, 1, tm)
    base = layer * N_EXPERTS
    grid_spec = pltpu.PrefetchScalarGridSpec(
        num_scalar_prefetch=2,
        grid=(nt,),
        in_specs=[
            pl.BlockSpec((1, 1, tm), lambda i, te, nu: (i, 0, 0), memory_space=pltpu.SMEM),
            pl.BlockSpec((1, 1, tm), lambda i, te, nu: (jnp.minimum(i + 1, nt - 1), 0, 0),
                         memory_space=pltpu.SMEM),
            pl.BlockSpec(memory_space=pl.ANY),
            pl.BlockSpec((1, d, dh), lambda i, te, nu: (base + te[i], 0, 0)),
            pl.BlockSpec((1, 1, dh), lambda i, te, nu: (base + te[i], 0, 0)),
            pl.BlockSpec((1, dh // 2, d), lambda i, te, nu: (base + te[i], 0, 0)),
            pl.BlockSpec((1, 1, d), lambda i, te, nu: (base + te[i], 0, 0)),
        ],
        out_specs=pl.BlockSpec((tm * SUBLANES, LANES), lambda i, te, nu: (i, 0)),
        scratch_shapes=[pltpu.VMEM((2, tm * SUBLANES, LANES), F32), pltpu.VMEM((d, dh), BF16),
                        pltpu.VMEM((dh // 2, d), BF16), pltpu.SemaphoreType.DMA((2,))],
    )
    return pl.pallas_call(
        _expert_kernel,
        grid_spec=grid_spec,
        out_shape=jax.ShapeDtypeStruct((p * SUBLANES, LANES), F32),
        compiler_params=_cparams("arbitrary"),
        name="expert_ffn",
    )(tile_expert, n_tiles_used, idx3, idx3, h_tiles, w1, b1, w2, b2)


def _combine_kernel(idx_cur_ref, idx_nxt_ref, y_hbm, x_ref, w_ref, gate_ref, o_ref, ybuf, sem):
    i = pl.program_id(0)
    n = pl.num_programs(0)
    tc = x_ref.shape[0]
    slot = i % 2

    def issue(idx_ref, s):
        rows_per_iter = COPIES_PER_ITER // TOP_K

        def body(g, carry):
            for u in range(rows_per_iter):
                r = g * rows_per_iter + u
                for k in range(TOP_K):
                    _row_tile_copy(y_hbm, idx_ref[0, 0, r * TOP_K + k], ybuf.at[s, k], r,
                                   sem.at[s]).start(priority=k % 2)
            return carry
        lax.fori_loop(0, tc // rows_per_iter, body, 0)

    @pl.when(i == 0)
    def _():
        issue(idx_cur_ref, 0)

    @pl.when(i + 1 < n)
    def _():
        issue(idx_nxt_ref, 1 - slot)

    for k in range(TOP_K):
        pltpu.make_async_copy(y_hbm.at[pl.ds(0, tc * SUBLANES)], ybuf.at[slot, k], sem.at[slot]).wait()
    w = w_ref[...]
    acc = w[:, 0:1] * _load_row_tiles(ybuf.at[slot, 0], tc)
    for k in range(1, TOP_K):
        acc = acc + w[:, k:k + 1] * _load_row_tiles(ybuf.at[slot, k], tc)
    o_ref[...] = x_ref[...] + gate_ref[0] * acc


def _moe_combine(y_tiles, slot8_of_pair, top_w, x, gate, seq):
    t, d = x.shape
    tc = COMBINE_TILE
    nt = t // tc
    per = seq // tc
    bsz = gate.shape[0]
    idx3 = slot8_of_pair.reshape(nt, 1, tc * TOP_K)
    return pl.pallas_call(
        _combine_kernel,
        grid=(nt,),
        in_specs=[
            pl.BlockSpec((1, 1, tc * TOP_K), lambda i: (i, 0, 0), memory_space=pltpu.SMEM),
            pl.BlockSpec((1, 1, tc * TOP_K), lambda i: (jnp.minimum(i + 1, nt - 1), 0, 0),
                         memory_space=pltpu.SMEM),
            pl.BlockSpec(memory_space=pl.ANY),
            pl.BlockSpec((tc, d), lambda i: (i, 0)),
            pl.BlockSpec((tc, TOP_K), lambda i: (i, 0)),
            pl.BlockSpec((1, 1, d), lambda i: (i // per, 0, 0)),
        ],
        out_specs=pl.BlockSpec((tc, d), lambda i: (i, 0)),
        out_shape=jax.ShapeDtypeStruct((t, d), F32),
        scratch_shapes=[pltpu.VMEM((2, TOP_K, tc * SUBLANES, LANES), F32), pltpu.SemaphoreType.DMA((2,))],
        compiler_params=_cparams("arbitrary"),
        name="moe_combine",
    )(idx3, idx3, y_tiles, x, top_w, gate.reshape(bsz, 1, d))


def _route(route):
    t = route.shape[0]
    top_w = route[:, :TOP_K]
    flat_e = route[:, TOP_K:2 * TOP_K].astype(jnp.int32).reshape(-1)
    onehot = (flat_e[:, None] == jnp.arange(N_EXPERTS, dtype=jnp.int32)[None, :]).astype(jnp.int32)
    csum = jnp.cumsum(onehot, axis=0)
    counts = csum[-1]
    padded = ((counts + MOE_TILE - 1) // MOE_TILE) * MOE_TILE
    ends = jnp.cumsum(padded)
    starts = ends - padded
    slot_of_pair = jnp.sum(onehot * (csum - 1 + starts[None, :]), axis=1)
    n_fill = N_EXPERTS * MOE_TILE
    fill_end = jnp.cumsum(padded - counts)
    fill_key = jnp.sum((jnp.arange(n_fill, dtype=jnp.int32)[:, None] >= fill_end[None, :]).astype(jnp.int32), axis=1)
    n_pairs = t * TOP_K
    pos_bits = (n_pairs + n_fill - 1).bit_length()
    assert (N_EXPERTS + 1) << pos_bits < 2 ** 31
    keys = (jnp.concatenate([flat_e, fill_key]) << pos_bits) + jnp.arange(n_pairs + n_fill, dtype=jnp.int32)
    entry = lax.sort(keys) & ((1 << pos_bits) - 1)
    src_tok = jnp.where(entry < n_pairs, entry // TOP_K, 0)
    n_tiles = (n_pairs + n_fill) // MOE_TILE
    tile_start = jnp.arange(n_tiles, dtype=jnp.int32) * MOE_TILE
    tile_expert = jnp.sum((tile_start[:, None] >= ends[None, :]).astype(jnp.int32), axis=1)
    n_used = (ends[-1] // MOE_TILE).astype(jnp.int32)
    last_e = jnp.sum(jnp.where(jnp.arange(n_tiles) == n_used - 1, tile_expert, 0))
    tile_expert = jnp.where(jnp.arange(n_tiles) < n_used, tile_expert, last_e).astype(jnp.int32)
    return (top_w, (slot_of_pair * SUBLANES).astype(jnp.int32), (src_tok * SUBLANES).astype(jnp.int32),
            tile_expert, n_used.reshape(1))


def _moe_layer(x, g, sc, sh, gate, rw, rb, w1, b1, w2, b2, layer, seq):
    h_tiles, route = _norm_router(x, g, sc, sh, rw, rb, seq)
    top_w, slot8_of_pair, src_row8, tile_expert, n_used = _route(route)
    y_tiles = _expert_ffn(h_tiles, src_row8, tile_expert, n_used, w1, b1, w2, b2, layer)
    return _moe_combine(y_tiles, slot8_of_pair, top_w, x, gate, seq)


def _chunk_pos(shape):
    return lax.broadcasted_iota(jnp.int32, shape, 0) & (CHUNK - 1)


def _inchunk_cumsum(x):
    pos = _chunk_pos(x.shape)
    sh = 1
    while sh < CHUNK:
        x = x + jnp.where(pos >= sh, pltpu.roll(x, sh, axis=0), 0.0)
        sh *= 2
    return x


def _chunk_last_rows(b):
    rows, d = b.shape
    b3 = b.reshape(rows // CHUNK, CHUNK, d)
    return jnp.broadcast_to(b3[:, CHUNK - 1:CHUNK, :], b3.shape).reshape(rows, d)


def _chunk_causal_mask(rows):
    r = lax.broadcasted_iota(jnp.int32, (rows, rows), 0)
    c = lax.broadcasted_iota(jnp.int32, (rows, rows), 1)
    return jnp.logical_and(r >= c, (r // CHUNK) == (c // CHUNK))


def _linattn_block(q, k, v, b, b_rem, chunk_decay, state_ref):
    rows = q.shape[0]
    q_in = (q * jnp.exp(b)).astype(BF16)
    k_in = (k * jnp.exp(-b)).astype(BF16)
    k_st = (k * jnp.exp(b_rem)).astype(BF16)
    vb = v.astype(BF16)
    scores = lax.dot_general(q_in, k_in, (((1,), (1,)), ((), ())), preferred_element_type=F32)
    scores = jnp.where(_chunk_causal_mask(rows), scores, 0.0)
    o_intra = jnp.dot(scores.astype(BF16), vb, preferred_element_type=F32)
    state = state_ref[...]
    o_inter = []
    for c in range(rows // CHUNK):
        sl = slice(c * CHUNK, (c + 1) * CHUNK)
        o_inter.append(jnp.dot(q_in[sl], state.astype(BF16), preferred_element_type=F32))
        upd = lax.dot_general(k_st[sl], vb[sl], (((0,), (0,)), ((), ())), preferred_element_type=F32)
        state = chunk_decay(c) * state + upd
    state_ref[...] = state
    return o_intra + jnp.concatenate(o_inter, axis=0)


HEADS_PER_STEP = 2


def _ret_kernel(q_ref, k_ref, v_ref, g_ref, cos_ref, sin_ref, lg_ref, gn_ref, o_ref, state_ref, *, k_scale):
    @pl.when(pl.program_id(2) == 0)
    def _():
        state_ref[...] = jnp.zeros_like(state_ref)

    rows = q_ref.shape[0]
    dk = q_ref.shape[1] // HEADS_PER_STEP
    dv = v_ref.shape[1] // HEADS_PER_STEP
    half = dk // 2
    cos, sin = cos_ref[...], sin_ref[...]

    def rope(t):
        t1, t2 = t[:, :half], t[:, half:]
        return jnp.concatenate([t1 * cos - t2 * sin, t1 * sin + t2 * cos], axis=1)

    pos = _chunk_pos((rows, dk)).astype(F32)
    for hh in range(HEADS_PER_STEP):
        ks, vs = slice(hh * dk, (hh + 1) * dk), slice(hh * dv, (hh + 1) * dv)
        lg = lg_ref[hh]
        b = (pos + 1.0) * lg
        b_rem = (CHUNK - 1.0 - pos) * lg
        decay = jnp.exp(CHUNK * lg[:, 0:1])
        q = rope(q_ref[:, ks])
        k = rope(k_ref[:, ks]) * k_scale
        o = _linattn_block(q, k, v_ref[:, vs], b, b_rem, lambda c: decay, state_ref.at[hh])
        g = g_ref[:, vs]
        o_ref[:, vs] = _row_norm(o, True) * gn_ref[hh] * (g * jax.nn.sigmoid(g))


def _retention_mixer(proj, gn_g, bsz, seq):
    n_heads, dv = gn_g.shape
    dk = dv // 2
    lb = SEQ_BLOCK
    hps = HEADS_PER_STEP
    n_hp = n_heads // hps
    p3 = proj.reshape(bsz, seq, proj.shape[-1])
    half = dk // 2
    inv = ROPE_BASE ** (-jnp.arange(half, dtype=F32) / half)
    ang = jnp.arange(seq, dtype=F32)[:, None] * inv[None, :]
    cos, sin = jnp.cos(ang), jnp.sin(ang)
    log_gamma = jnp.log1p(-jnp.exp2(-5.0 - jnp.arange(n_heads, dtype=F32)))
    lg = jnp.broadcast_to(log_gamma[:, None, None], (n_heads, 1, dk))
    out = pl.pallas_call(
        functools.partial(_ret_kernel, k_scale=dk ** -0.5),
        grid=(bsz, n_hp, seq // lb),
        in_specs=[pl.BlockSpec((None, lb, hps * dk), lambda b, h, s: (b, s, h)),
                  pl.BlockSpec((None, lb, hps * dk), lambda b, h, s: (b, s, n_hp + h)),
                  pl.BlockSpec((None, lb, hps * dv), lambda b, h, s: (b, s, n_hp + h)),
                  pl.BlockSpec((None, lb, hps * dv), lambda b, h, s: (b, s, 2 * n_hp + h)),
                  pl.BlockSpec((lb, half), lambda b, h, s: (s, 0)),
                  pl.BlockSpec((lb, half), lambda b, h, s: (s, 0)),
                  pl.BlockSpec((hps, 1, dk), lambda b, h, s: (h, 0, 0)),
                  pl.BlockSpec((hps, 1, dv), lambda b, h, s: (h, 0, 0))],
        out_specs=pl.BlockSpec((None, lb, hps * dv), lambda b, h, s: (b, s, h)),
        out_shape=jax.ShapeDtypeStruct((bsz, seq, n_heads * dv), F32),
        scratch_shapes=[pltpu.VMEM((hps, dk, dv), F32)],
        compiler_params=_cparams("arbitrary", "arbitrary", "arbitrary"),
        name="retention",
    )(p3, p3, p3, p3, cos, sin, lg, gn_g.reshape(n_heads, 1, dv))
    return out.reshape(bsz * seq, n_heads * dv)


def _gla_kernel(q_ref, k_ref, v_ref, r_ref, glow_ref, wg_ref, bg_ref, gn_ref, o_ref, state_ref, *, q_scale):
    @pl.when(pl.program_id(2) == 0)
    def _():
        state_ref[...] = jnp.zeros_like(state_ref)

    dk = q_ref.shape[1] // HEADS_PER_STEP
    dv = v_ref.shape[1] // HEADS_PER_STEP

    rows = q_ref.shape[0]
    z = jnp.dot(glow_ref[...], wg_ref[...], precision=HI, preferred_element_type=F32) + bg_ref[...]
    la_all = jax.nn.log_sigmoid(z) / GLA_GATE_NORM
    chunk_sel = (lax.broadcasted_iota(jnp.int32, (rows, LANES), 0) // CHUNK
                 == lax.broadcasted_iota(jnp.int32, (rows, LANES), 1)).astype(F32)
    for hh in range(HEADS_PER_STEP):
        ks, vs = slice(hh * dk, (hh + 1) * dk), slice(hh * dv, (hh + 1) * dv)
        la = la_all[:, ks]
        b = _inchunk_cumsum(la)
        b_rem = _chunk_last_rows(b) - b
        decay = jnp.exp(lax.dot_general(la, chunk_sel, (((0,), (0,)), ((), ())),
                                        precision=HI, preferred_element_type=F32))
        o = _linattn_block(q_ref[:, ks] * q_scale, k_ref[:, ks], v_ref[:, vs], b, b_rem,
                           lambda c: decay[:, c:c + 1], state_ref.at[hh])
        r = r_ref[:, vs]
        o_ref[:, vs] = _row_norm(o, False) * gn_ref[hh] * (r * jax.nn.sigmoid(r))


def _gla_proj_weight(w_in, dq, dvv):
    q_k_v = w_in[:, :2 * dq + dvv]
    g_low = w_in[:, 2 * dq + dvv:2 * dq + dvv + GLA_GATE_RANK]
    r = w_in[:, 2 * dq + dvv + GLA_GATE_RANK:]
    pad = jnp.zeros((w_in.shape[0], LANES - GLA_GATE_RANK), w_in.dtype)
    return jnp.concatenate([q_k_v, r, g_low, pad], axis=1)


def _gla_mixer(proj, w_gate, b_gate, gn_g, bsz, seq):
    n_heads, dv = gn_g.shape
    dk = dv // 2
    dq, dvv = n_heads * dk, n_heads * dv
    lb = SEQ_BLOCK
    hps = HEADS_PER_STEP
    n_hp = n_heads // hps
    p3 = proj.reshape(bsz, seq, proj.shape[-1])
    wg = jnp.pad(w_gate, ((0, LANES - GLA_GATE_RANK), (0, 0)))
    out = pl.pallas_call(
        functools.partial(_gla_kernel, q_scale=dk ** -0.5),
        grid=(bsz, n_hp, seq // lb),
        in_specs=[pl.BlockSpec((None, lb, hps * dk), lambda b, h, s: (b, s, h)),
                  pl.BlockSpec((None, lb, hps * dk), lambda b, h, s: (b, s, n_hp + h)),
                  pl.BlockSpec((None, lb, hps * dv), lambda b, h, s: (b, s, n_hp + h)),
                  pl.BlockSpec((None, lb, hps * dv), lambda b, h, s: (b, s, 2 * n_hp + h)),
                  pl.BlockSpec((None, lb, LANES), lambda b, h, s: (b, s, (2 * dq + 2 * dvv) // LANES)),
                  pl.BlockSpec((LANES, hps * dk), lambda b, h, s: (0, h)),
                  pl.BlockSpec((1, hps * dk), lambda b, h, s: (0, h)),
                  pl.BlockSpec((hps, 1, dv), lambda b, h, s: (h, 0, 0))],
        out_specs=pl.BlockSpec((None, lb, hps * dv), lambda b, h, s: (b, s, h)),
        out_shape=jax.ShapeDtypeStruct((bsz, seq, dvv), F32),
        scratch_shapes=[pltpu.VMEM((hps, dk, dv), F32)],
        compiler_params=_cparams("arbitrary", "arbitrary", "arbitrary"),
        name="gla",
    )(p3, p3, p3, p3, p3, wg, b_gate.reshape(1, dq), gn_g.reshape(n_heads, 1, dv))
    return out.reshape(bsz * seq, dvv)


def _att_kernel(q_ref, k_ref, v_ref, bias_ref, o_ref, kpad, vpad, *, scale, dh):
    seq, width = q_ref.shape
    n_pair = width // dh
    kpad[0:ATT_LEFT, :] = jnp.zeros((ATT_LEFT, width), F32)
    vpad[0:ATT_LEFT, :] = jnp.zeros((ATT_LEFT, width), F32)
    kpad[ATT_LEFT:ATT_LEFT + seq, :] = k_ref[...]
    vpad[ATT_LEFT:ATT_LEFT + seq, :] = v_ref[...]
    lane = lax.broadcasted_iota(jnp.int32, (ATT_QBLOCK, width), 1)
    jcol = lax.broadcasted_iota(jnp.int32, (n_pair * ATT_QBLOCK, ATT_WINDOW), 1)
    in_head = [jnp.logical_and(lane >= h * dh, lane < (h + 1) * dh) for h in range(n_pair)]

    def block(start, left_edge):
        q = q_ref[pl.ds(start, ATT_QBLOCK), :] * scale
        kb = kpad[pl.ds(start, ATT_WINDOW), :].astype(BF16)
        vb = vpad[pl.ds(start, ATT_WINDOW), :].astype(BF16)
        q2 = jnp.concatenate([jnp.where(m, q, 0.0) for m in in_head], axis=0).astype(BF16)
        s = lax.dot_general(q2, kb, (((1,), (1,)), ((), ())), preferred_element_type=F32) + bias_ref[...]
        if left_edge:
            s = jnp.where(jcol >= ATT_LEFT - start, s, -jnp.inf)
        p = jnp.exp(s - jnp.max(s, axis=-1, keepdims=True))
        o2 = jnp.dot(p.astype(BF16), vb, preferred_element_type=F32) / jnp.sum(p, axis=-1, keepdims=True)
        out = o2[0:ATT_QBLOCK, :]
        for h in range(1, n_pair):
            out = jnp.where(in_head[h], o2[h * ATT_QBLOCK:(h + 1) * ATT_QBLOCK, :], out)
        o_ref[pl.ds(start, ATT_QBLOCK), :] = out

    n_edge = ATT_LEFT // ATT_QBLOCK
    for c in range(n_edge):
        block(c * ATT_QBLOCK, True)

    def body(c, carry):
        block(pl.multiple_of(c * ATT_QBLOCK, ATT_QBLOCK), False)
        return carry

    lax.fori_loop(n_edge, seq // ATT_QBLOCK, body, 0, unroll=4)


def _attention_mixer(proj, rel_bias, bsz, seq):
    d = proj.shape[-1] // 3
    n_heads = rel_bias.shape[0]
    dh = d // n_heads
    n_pair = LANES // dh
    groups = n_heads // n_pair
    qo = jnp.arange(ATT_QBLOCK)[:, None]
    kj = jnp.arange(ATT_WINDOW)[None, :]
    lo = (qo // CHUNK) * CHUNK
    in_window = jnp.logical_and(kj >= lo, kj < lo + ATT_LEFT + CHUNK)
    d_max = ATT_QBLOCK - 1 + ATT_LEFT
    d_min = -(ATT_WINDOW - 1 - ATT_LEFT)
    n_rel = rel_bias.shape[1]
    ext = jnp.concatenate(
        [jnp.broadcast_to(rel_bias[:, :1], (n_heads, -(CHUNK - 1) - d_min)), rel_bias,
         jnp.broadcast_to(rel_bias[:, n_rel - 1:], (n_heads, d_max - ATT_MAX_REL))], axis=1)
    rows = [ext[:, q_ + ATT_LEFT - (ATT_WINDOW - 1) - d_min:q_ + ATT_LEFT - d_min + 1] for q_ in range(ATT_QBLOCK)]
    table = jnp.flip(jnp.stack(rows, axis=1), axis=2).astype(F32)
    bias = jnp.where(in_window[None], table, -jnp.inf)
    bias = bias.reshape(groups, n_pair * ATT_QBLOCK, ATT_WINDOW)
    p3 = proj.reshape(bsz, seq, 3 * d)
    out = pl.pallas_call(
        functools.partial(_att_kernel, scale=dh ** -0.5, dh=dh),
        grid=(bsz, groups),
        in_specs=[pl.BlockSpec((None, seq, LANES), lambda b, g: (b, 0, g)),
                  pl.BlockSpec((None, seq, LANES), lambda b, g: (b, 0, groups + g)),
                  pl.BlockSpec((None, seq, LANES), lambda b, g: (b, 0, 2 * groups + g)),
                  pl.BlockSpec((None, n_pair * ATT_QBLOCK, ATT_WINDOW), lambda b, g: (g, 0, 0))],
        out_specs=pl.BlockSpec((None, seq, LANES), lambda b, g: (b, 0, g)),
        out_shape=jax.ShapeDtypeStruct((bsz, seq, d), F32),
        scratch_shapes=[pltpu.VMEM((ATT_LEFT + seq, LANES), F32), pltpu.VMEM((ATT_LEFT + seq, LANES), F32)],
        compiler_params=_cparams("arbitrary", "arbitrary"),
        name="chunk_attention",
    )(p3, p3, p3, bias)
    return out.reshape(bsz * seq, d)


def _mlstm_pre_kernel(xm_ref, cw_ref, cb_ref, wq_ref, wk_ref, wv_ref, wg_ref, bg_ref,
                      q_ref, k_ref, v_ref, xc_ref, g_ref, xpad, *, per, k_scale):
    i = pl.program_id(0)
    tm, inner = xm_ref.shape

    @pl.when(i % per == 0)
    def _():
        xpad[0:CONV_PAD, :] = jnp.zeros((CONV_PAD, inner), F32)

    @pl.when(i % per != 0)
    def _():
        xpad[0:CONV_PAD, :] = xpad[tm:tm + CONV_PAD, :]

    xm = xm_ref[...]
    xpad[CONV_PAD:CONV_PAD + tm, :] = xm
    acc = jnp.broadcast_to(cb_ref[...], (tm, inner))
    for j in range(MLSTM_CONV):
        off = CONV_PAD - (MLSTM_CONV - 1) + j
        acc = acc + cw_ref[j:j + 1, :] * xpad[off:off + tm, :]
    xc = acc * jax.nn.sigmoid(acc)
    xc_ref[...] = xc

    def blockdiag(t, w_ref):
        tb = t.astype(BF16)
        return jnp.concatenate(
            [jnp.dot(tb[:, g * LANES:(g + 1) * LANES], w_ref[g], preferred_element_type=F32)
             for g in range(inner // LANES)], axis=1)

    q = blockdiag(xc, wq_ref)
    k = blockdiag(xc, wk_ref) * k_scale
    v = blockdiag(xm, wv_ref)
    qb, kb, vb = q.astype(BF16), k.astype(BF16), v.astype(BF16)
    q_ref[...] = qb
    k_ref[...] = kb
    v_ref[...] = vb
    g_ref[...] = (jnp.dot(qb, wg_ref[0:inner, :], preferred_element_type=F32)
                  + jnp.dot(kb, wg_ref[inner:2 * inner, :], preferred_element_type=F32)
                  + jnp.dot(vb, wg_ref[2 * inner:3 * inner, :], preferred_element_type=F32) + bg_ref[...])


def _blockdiag_tiles(w):
    nb, c, _ = w.shape
    per = LANES // c
    wt = w.reshape(nb // per, per, c, c)
    t = jnp.einsum('gpcd,pq->gpcqd', wt, jnp.eye(per, dtype=w.dtype))
    return t.reshape(nb // per, LANES, LANES).astype(BF16)


def _mlstm_rec_kernel(q_ref, k_ref, v_ref, g_ref, xc_ref, z_ref, gn_ref, skip_ref, o_ref,
                      c_ref, n_ref, m_ref, *, n_heads):
    head0 = pl.program_id(1) * HEADS_PER_STEP
    dh = q_ref.shape[1] // HEADS_PER_STEP

    @pl.when(pl.program_id(2) == 0)
    def _():
        c_ref[...] = jnp.zeros_like(c_ref)
        n_ref[...] = jnp.zeros_like(n_ref)
        m_ref[...] = jnp.zeros_like(m_ref)

    rows = q_ref.shape[0]
    n_chunks = rows // CHUNK
    mask = _chunk_causal_mask(rows)
    lane = lax.broadcasted_iota(jnp.int32, (rows, LANES), 1)
    sub = lax.broadcasted_iota(jnp.int32, (LANES, rows), 0)

    def col(t, idx):
        return jnp.sum(jnp.where(lane == idx, t, 0.0), axis=1, keepdims=True)

    def row(t_t, idx):
        return jnp.sum(jnp.where(sub == idx, t_t, 0.0), axis=0, keepdims=True)

    def per_chunk(vals):
        return jnp.concatenate([jnp.broadcast_to(x, (CHUNK, 1)) for x in vals], axis=0)

    gates = g_ref[...]
    bcum = _inchunk_cumsum(jax.nn.log_sigmoid(gates))
    gates_t, bcum_t = gates.T, bcum.T
    for hh in range(HEADS_PER_STEP):
        hs = slice(hh * dh, (hh + 1) * dh)
        i_idx, f_idx = head0 + hh, n_heads + head0 + hh
        i_col, b_col = col(gates, i_idx), col(bcum, f_idx)
        b_last_rows = _chunk_last_rows(b_col)
        log_w = b_last_rows - b_col + i_col
        m_start, m_next, keep = [], [], []
        m_c = m_ref[hh, :, 0:1]
        for c in range(n_chunks):
            sl = slice(c * CHUNK, (c + 1) * CHUNK)
            log_keep = b_last_rows[c * CHUNK:c * CHUNK + 1, :] + m_c
            m_n = jnp.maximum(log_keep, jnp.max(log_w[sl], axis=0, keepdims=True))
            m_start.append(m_c)
            m_next.append(m_n)
            keep.append(jnp.exp(log_keep - m_n))
            m_c = m_n
        m_ref[hh] = jnp.broadcast_to(m_c, (1, LANES))
        log_intra = jnp.where(mask, b_col - row(bcum_t, f_idx) + row(gates_t, i_idx), -jnp.inf)
        log_inter = b_col + per_chunk(m_start)
        m_row = jnp.maximum(log_inter, jnp.max(log_intra, axis=1, keepdims=True))
        w_intra = jnp.exp(log_intra - m_row)
        w_inter = jnp.exp(log_inter - m_row)
        qb, kb, vb = q_ref[:, hs], k_ref[:, hs], v_ref[:, hs]
        q, k = qb.astype(F32), kb.astype(F32)
        s = lax.dot_general(qb, kb, (((1,), (1,)), ((), ())), preferred_element_type=F32) * w_intra
        num_intra = jnp.dot(s.astype(BF16), vb, preferred_element_type=F32)
        den_intra = jnp.sum(s, axis=1, keepdims=True)
        wk = jnp.exp(log_w - per_chunk(m_next)) * k
        wkb = wk.astype(BF16)
        cmat, nvec = c_ref[hh], n_ref[hh]
        num_inter, den_inter = [], []
        for c in range(n_chunks):
            sl = slice(c * CHUNK, (c + 1) * CHUNK)
            num_inter.append(jnp.dot(qb[sl], cmat.astype(BF16), preferred_element_type=F32))
            den_inter.append(jnp.sum(q[sl] * nvec, axis=1, keepdims=True))
            upd = lax.dot_general(wkb[sl], vb[sl], (((0,), (0,)), ((), ())), preferred_element_type=F32)
            cmat = keep[c] * cmat + upd
            nvec = keep[c] * nvec + jnp.sum(wk[sl], axis=0, keepdims=True)
        c_ref[hh] = cmat
        n_ref[hh] = nvec
        num = w_inter * jnp.concatenate(num_inter, axis=0) + num_intra
        den = w_inter * jnp.concatenate(den_inter, axis=0) + den_intra
        hc = num / jnp.maximum(jnp.abs(den), jnp.exp(-m_row))
        z = z_ref[:, hs]
        o_ref[:, hs] = ((_row_norm(hc, True) * gn_ref[hh] + skip_ref[:, hs] * xc_ref[:, hs])
                        * (z * jax.nn.sigmoid(z)))


def _mlstm_mixer(proj, conv_w, conv_b, w_q, w_k, w_v, w_gates, b_gates, gn_g, skip, bsz, seq):
    t = proj.shape[0]
    inner = proj.shape[1] // 2
    n_heads, dh = gn_g.shape
    tm = MLSTM_PRE_TILE
    n_g = w_gates.shape[1]
    wg = jnp.pad(w_gates, ((0, 0), (0, LANES - n_g))).astype(BF16)
    bg = jnp.pad(b_gates.reshape(1, n_g), ((0, 0), (0, LANES - n_g)))
    tile_spec = pl.BlockSpec((tm, inner), lambda i: (i, 0))
    bd_spec = pl.BlockSpec((inner // LANES, LANES, LANES), lambda i: (0, 0, 0))
    q, k, v, xc, gates = pl.pallas_call(
        functools.partial(_mlstm_pre_kernel, per=seq // tm, k_scale=dh ** -0.5),
        grid=(t // tm,),
        in_specs=[tile_spec,
                  pl.BlockSpec((MLSTM_CONV, inner), lambda i: (0, 0)),
                  pl.BlockSpec((1, inner), lambda i: (0, 0)),
                  bd_spec, bd_spec, bd_spec,
                  pl.BlockSpec((3 * inner, LANES), lambda i: (0, 0)),
                  pl.BlockSpec((1, LANES), lambda i: (0, 0))],
        out_specs=[tile_spec, tile_spec, tile_spec, tile_spec, pl.BlockSpec((tm, LANES), lambda i: (i, 0))],
        out_shape=[jax.ShapeDtypeStruct((t, inner), BF16)] * 3
        + [jax.ShapeDtypeStruct((t, inner), F32), jax.ShapeDtypeStruct((t, LANES), F32)],
        scratch_shapes=[pltpu.VMEM((tm + CONV_PAD, inner), F32)],
        compiler_params=_cparams("arbitrary"),
        name="mlstm_pre",
    )(proj, conv_w, conv_b.reshape(1, inner), _blockdiag_tiles(w_q), _blockdiag_tiles(w_k),
      _blockdiag_tiles(w_v), wg, bg)

    lb = SEQ_BLOCK
    hps = HEADS_PER_STEP
    n_hp = n_heads // hps
    r3 = lambda a: a.reshape(bsz, seq, a.shape[-1])
    head_spec = pl.BlockSpec((None, lb, hps * dh), lambda b, h, s: (b, s, h))
    out = pl.pallas_call(
        functools.partial(_mlstm_rec_kernel, n_heads=n_heads),
        grid=(bsz, n_hp, seq // lb),
        in_specs=[head_spec, head_spec, head_spec,
                  pl.BlockSpec((None, lb, LANES), lambda b, h, s: (b, s, 0)),
                  head_spec,
                  pl.BlockSpec((None, lb, hps * dh), lambda b, h, s: (b, s, n_hp + h)),
                  pl.BlockSpec((hps, 1, dh), lambda b, h, s: (h, 0, 0)),
                  pl.BlockSpec((1, hps * dh), lambda b, h, s: (0, h))],
        out_specs=head_spec,
        out_shape=jax.ShapeDtypeStruct((bsz, seq, inner), F32),
        scratch_shapes=[pltpu.VMEM((hps, dh, dh), F32), pltpu.VMEM((hps, 1, dh), F32),
                        pltpu.VMEM((hps, 1, LANES), F32)],
        compiler_params=_cparams("arbitrary", "arbitrary", "arbitrary"),
        name="mlstm_rec",
    )(r3(q), r3(k), r3(v), r3(gates), r3(xc), r3(proj), gn_g.reshape(n_heads, 1, dh), skip.reshape(1, inner))
    return out.reshape(t, inner)


def kernel(x, c, ada_w, ada_b, norm_mix_g, norm_ffn_g, norm_final_g, ret_w_in, ret_gn_g, ret_w_out, att_w_in, att_rel_bias, att_w_out, gla_w_in, gla_w_gate, gla_b_gate, gla_gn_g, gla_w_out, mlstm_w_in, mlstm_conv_w, mlstm_conv_b, mlstm_w_q, mlstm_w_k, mlstm_w_v, mlstm_w_gates, mlstm_b_gates, mlstm_gn_g, mlstm_skip, mlstm_w_out, router_w, router_b, moe_w1, moe_b1, moe_w2, moe_b2):
    bsz, seq, d = x.shape
    depth, n_exp, _, dh2 = moe_w1.shape
    xt = x.reshape(bsz * seq, d)
    mod = _ada_mod(c, ada_w, ada_b)
    w1_all = moe_w1.reshape(depth * n_exp, d, dh2)
    w2_all = moe_w2.reshape(depth * n_exp, dh2 // 2, d)
    b1_all = moe_b1.reshape(depth * n_exp, 1, dh2)
    b2_all = moe_b2.reshape(depth * n_exp, 1, d)
    for i in range(depth):
        kind, j = i % 4, i // 4
        sh_a, sc_a, g_a, sh_f, sc_f, g_f = jnp.split(mod[i], 6, axis=-1)
        if kind == 0:
            proj = _norm_mm(xt, norm_mix_g[i], sc_a, sh_a, ret_w_in[j].astype(BF16), seq)
            y = _retention_mixer(proj, ret_gn_g[j], bsz, seq)
            w_out = ret_w_out[j]
        elif kind == 1:
            proj = _norm_mm(xt, norm_mix_g[i], sc_a, sh_a, att_w_in[j].astype(BF16), seq)
            y = _attention_mixer(proj, att_rel_bias[j], bsz, seq)
            w_out = att_w_out[j]
        elif kind == 2:
            n_heads, dv = gla_gn_g[j].shape
            w_gla = _gla_proj_weight(gla_w_in[j], n_heads * dv // 2, n_heads * dv).astype(BF16)
            proj = _norm_mm(xt, norm_mix_g[i], sc_a, sh_a, w_gla, seq, tn=w_gla.shape[1] // 5)
            y = _gla_mixer(proj, gla_w_gate[j], gla_b_gate[j], gla_gn_g[j], bsz, seq)
            w_out = gla_w_out[j]
        else:
            proj = _norm_mm(xt, norm_mix_g[i], sc_a, sh_a, mlstm_w_in[j].astype(BF16), seq)
            y = _mlstm_mixer(proj, mlstm_conv_w[j], mlstm_conv_b[j], mlstm_w_q[j], mlstm_w_k[j], mlstm_w_v[j],
                             mlstm_w_gates[j], mlstm_b_gates[j], mlstm_gn_g[j], mlstm_skip[j], bsz, seq)
            w_out = mlstm_w_out[j]
        xt = _mm_res(y, w_out.astype(BF16), xt, g_a, seq)
        xt = _moe_layer(xt, norm_ffn_g[i], sc_f, sh_f, g_f, router_w[i], router_b[i],
                        w1_all, b1_all, w2_all, b2_all, i, seq)
    return _final_norm(xt, norm_final_g).reshape(bsz, seq, d)
```

```python
import functools

import jax
import jax.numpy as jnp
from jax import lax
from jax.experimental import pallas as pl
from jax.experimental.pallas import tpu as pltpu

F32 = jnp.float32
BF16 = jnp.bfloat16
HI = lax.Precision.HIGHEST

CHUNK = 64
EPS = 1e-6
ROPE_BASE = 10000.0
ATT_LEFT_CHUNKS = 8
ATT_MAX_REL = 256
GLA_GATE_RANK = 16
GLA_GATE_NORM = 16.0
MLSTM_CONV = 4
N_EXPERTS = 32
TOP_K = 4
SWIGLU_LIMIT = 7.0
SWIGLU_ALPHA = 1.702

VMEM_LIMIT_BYTES = 56 * 1024 * 1024
LANES = 128
SUBLANES = 8
MOE_TILE = 512
COMBINE_TILE = 256
COPIES_PER_ITER = 32
SEQ_BLOCK = 256
MLSTM_PRE_TILE = 256
CONV_PAD = 8
ATT_QBLOCK = 2 * CHUNK
ATT_LEFT = ATT_LEFT_CHUNKS * CHUNK
ATT_WINDOW = ATT_LEFT + ATT_QBLOCK


def _cparams(*sem):
    return pltpu.CompilerParams(dimension_semantics=sem, vmem_limit_bytes=VMEM_LIMIT_BYTES)


def _dot_bf16(a, b, dims=(((1,), (0,)), ((), ()))):
    return lax.dot_general(a.astype(BF16), b.astype(BF16), dims, preferred_element_type=F32)


def _tril_ones(n):
    row = lax.broadcasted_iota(jnp.int32, (n, n), 0)
    col = lax.broadcasted_iota(jnp.int32, (n, n), 1)
    return row >= col


def _row_norm(x, center):
    if center:
        x = x - jnp.mean(x, axis=-1, keepdims=True)
    return x * lax.rsqrt(jnp.mean(x * x, axis=-1, keepdims=True) + EPS)


def _load_row_tiles(ref, rows):
    return jnp.concatenate([ref[pl.ds(j, rows, stride=SUBLANES), :] for j in range(SUBLANES)], axis=1)


def _store_row_tiles(ref, val):
    rows = val.shape[0]
    for j in range(SUBLANES):
        ref[pl.ds(j, rows, stride=SUBLANES), :] = val[:, j * LANES:(j + 1) * LANES]


def _ada_kernel(c_ref, w_ref, b_ref, o_ref):
    c = c_ref[...]
    cond = c * jax.nn.sigmoid(c)
    o_ref[0] = jnp.dot(cond, w_ref[0], preferred_element_type=F32, precision=HI) + b_ref[0]


def _ada_mod(c, ada_w, ada_b):
    depth, d, n = ada_w.shape
    b = c.shape[0]
    tn = 1536
    return pl.pallas_call(
        _ada_kernel,
        grid=(depth, n // tn),
        in_specs=[pl.BlockSpec((b, d), lambda l, j: (0, 0)),
                  pl.BlockSpec((1, d, tn), lambda l, j: (l, 0, j)),
                  pl.BlockSpec((1, 1, tn), lambda l, j: (l, 0, j))],
        out_specs=pl.BlockSpec((1, b, tn), lambda l, j: (l, 0, j)),
        out_shape=jax.ShapeDtypeStruct((depth, b, n), F32),
        compiler_params=_cparams("arbitrary", "arbitrary"),
        name="ada_mod",
    )(c, ada_w, ada_b.reshape(depth, 1, n))


def _modulated_norm(x, g, sc, sh):
    ms = jnp.mean(x * x, axis=-1, keepdims=True)
    return (x * lax.rsqrt(ms + EPS) * g) * (1.0 + sc) + sh


def _norm_mm_kernel(x_ref, g_ref, sc_ref, sh_ref, w_ref, o_ref, h_scr):
    @pl.when(pl.program_id(1) == 0)
    def _():
        h = _modulated_norm(x_ref[...], g_ref[...], sc_ref[0], sh_ref[0])
        h_scr[...] = h.astype(BF16)

    o_ref[...] = jnp.dot(h_scr[...], w_ref[...], preferred_element_type=F32)


def _norm_mm(x, g, sc, sh, w, seq, tm=1024, tn=512):
    t, d = x.shape
    n = w.shape[1]
    per = seq // tm
    bsz = sc.shape[0]
    return pl.pallas_call(
        _norm_mm_kernel,
        grid=(t // tm, n // tn),
        in_specs=[pl.BlockSpec((tm, d), lambda i, j: (i, 0)),
                  pl.BlockSpec((1, d), lambda i, j: (0, 0)),
                  pl.BlockSpec((1, 1, d), lambda i, j: (i // per, 0, 0)),
                  pl.BlockSpec((1, 1, d), lambda i, j: (i // per, 0, 0)),
                  pl.BlockSpec((d, tn), lambda i, j: (0, j))],
        out_specs=pl.BlockSpec((tm, tn), lambda i, j: (i, j)),
        out_shape=jax.ShapeDtypeStruct((t, n), F32),
        scratch_shapes=[pltpu.VMEM((tm, d), BF16)],
        compiler_params=_cparams("arbitrary", "arbitrary"),
        name="norm_mm",
    )(x, g.reshape(1, d), sc.reshape(bsz, 1, d), sh.reshape(bsz, 1, d), w)


def _mm_res_kernel(y_ref, w_ref, x_ref, gate_ref, o_ref):
    acc = jnp.dot(y_ref[...].astype(BF16), w_ref[...], preferred_element_type=F32)
    o_ref[...] = x_ref[...] + gate_ref[0] * acc


def _mm_res(y, w, x, gate, seq, tm=512):
    t, k = y.shape
    d = w.shape[1]
    per = seq // tm
    bsz = gate.shape[0]
    return pl.pallas_call(
        _mm_res_kernel,
        grid=(t // tm,),
        in_specs=[pl.BlockSpec((tm, k), lambda i: (i, 0)),
                  pl.BlockSpec((k, d), lambda i: (0, 0)),
                  pl.BlockSpec((tm, d), lambda i: (i, 0)),
                  pl.BlockSpec((1, 1, d), lambda i: (i // per, 0, 0))],
        out_specs=pl.BlockSpec((tm, d), lambda i: (i, 0)),
        out_shape=jax.ShapeDtypeStruct((t, d), F32),
        compiler_params=_cparams("arbitrary"),
        name="mm_res",
    )(y, w, x, gate.reshape(bsz, 1, d))


def _final_norm_kernel(x_ref, g_ref, o_ref):
    x = x_ref[...]
    ms = jnp.mean(x * x, axis=-1, keepdims=True)
    o_ref[...] = x * lax.rsqrt(ms + EPS) * g_ref[...]


def _final_norm(x, g, tm=1024):
    t, d = x.shape
    return pl.pallas_call(
        _final_norm_kernel,
        grid=(t // tm,),
        in_specs=[pl.BlockSpec((tm, d), lambda i: (i, 0)),
                  pl.BlockSpec((1, d), lambda i: (0, 0))],
        out_specs=pl.BlockSpec((tm, d), lambda i: (i, 0)),
        out_shape=jax.ShapeDtypeStruct((t, d), F32),
        compiler_params=_cparams("arbitrary"),
        name="final_norm",
    )(x, g.reshape(1, d))


def _norm_router_kernel(x_ref, g_ref, sc_ref, sh_ref, rw_ref, rb_ref, h_ref, route_ref, *, n_experts):
    h = _modulated_norm(x_ref[...], g_ref[...], sc_ref[0], sh_ref[0])
    _store_row_tiles(h_ref, h)
    logits = jnp.dot(h, rw_ref[...], preferred_element_type=F32, precision=HI) + rb_ref[...]
    lane = lax.broadcasted_iota(jnp.int32, logits.shape, 1)
    rem = jnp.where(lane < n_experts, logits, -jnp.inf)
    vals, idxs = [], []
    for _ in range(TOP_K):
        m = jnp.max(rem, axis=1, keepdims=True)
        idx = jnp.min(jnp.where(rem == m, lane, LANES), axis=1, keepdims=True)
        vals.append(m)
        idxs.append(idx)
        rem = jnp.where(lane == idx, -jnp.inf, rem)
    ex = [jnp.exp(v - vals[0]) for v in vals]
    inv = 1.0 / sum(ex[1:], ex[0])
    out = jnp.zeros(logits.shape, F32)
    for k in range(TOP_K):
        out = jnp.where(lane == k, ex[k] * inv, out)
        out = jnp.where(lane == TOP_K + k, idxs[k].astype(F32), out)
    route_ref[...] = out


def _norm_router(x, g, sc, sh, rw, rb, seq, tm=512):
    t, d = x.shape
    per = seq // tm
    bsz = sc.shape[0]
    e = rw.shape[1]
    rw_p = jnp.pad(rw, ((0, 0), (0, LANES - e)))
    rb_p = jnp.pad(rb.reshape(1, e), ((0, 0), (0, LANES - e)))
    return pl.pallas_call(
        functools.partial(_norm_router_kernel, n_experts=e),
        grid=(t // tm,),
        in_specs=[pl.BlockSpec((tm, d), lambda i: (i, 0)),
                  pl.BlockSpec((1, d), lambda i: (0, 0)),
                  pl.BlockSpec((1, 1, d), lambda i: (i // per, 0, 0)),
                  pl.BlockSpec((1, 1, d), lambda i: (i // per, 0, 0)),
                  pl.BlockSpec((d, LANES), lambda i: (0, 0)),
                  pl.BlockSpec((1, LANES), lambda i: (0, 0))],
        out_specs=[pl.BlockSpec((tm * SUBLANES, LANES), lambda i: (i, 0)),
                   pl.BlockSpec((tm, LANES), lambda i: (i, 0))],
        out_shape=[jax.ShapeDtypeStruct((t * SUBLANES, LANES), F32),
                   jax.ShapeDtypeStruct((t, LANES), F32)],
        compiler_params=_cparams("arbitrary"),
        name="norm_router",
    )(x, g.reshape(1, d), sc.reshape(bsz, 1, d), sh.reshape(bsz, 1, d), rw_p, rb_p)


def _row_tile_copy(src_hbm, src_row8, dst_vmem, dst_row, sem):
    return pltpu.make_async_copy(
        src_hbm.at[pl.ds(pl.multiple_of(src_row8, SUBLANES), SUBLANES)],
        dst_vmem.at[pl.ds(pl.multiple_of(dst_row * SUBLANES, SUBLANES), SUBLANES)], sem)


def _expert_kernel(te_ref, nt_ref, idx_cur_ref, idx_nxt_ref, h_hbm, w1_ref, b1_ref, w2_ref, b2_ref,
                   o_ref, xbuf, w1_bf, w2_bf, sem):
    i = pl.program_id(0)
    n_used = nt_ref[0]
    tm = xbuf.shape[1] // SUBLANES
    slot = i % 2

    @pl.when(jnp.logical_or(i == 0, te_ref[i] != te_ref[jnp.maximum(i - 1, 0)]))
    def _():
        w1_bf[...] = w1_ref[0].astype(BF16)
        w2_bf[...] = w2_ref[0].astype(BF16)

    def issue(idx_ref, s):
        def body(g, carry):
            for u in range(COPIES_PER_ITER):
                r = g * COPIES_PER_ITER + u
                _row_tile_copy(h_hbm, idx_ref[0, 0, r], xbuf.at[s], r, sem.at[s]).start(priority=u % 2)
            return carry
        lax.fori_loop(0, tm // COPIES_PER_ITER, body, 0)

    @pl.when(jnp.logical_and(i == 0, n_used > 0))
    def _():
        issue(idx_cur_ref, 0)

    @pl.when(i + 1 < n_used)
    def _():
        issue(idx_nxt_ref, 1 - slot)

    @pl.when(i < n_used)
    def _():
        pltpu.make_async_copy(h_hbm.at[pl.ds(0, tm * SUBLANES)], xbuf.at[slot], sem.at[slot]).wait()
        x = _load_row_tiles(xbuf.at[slot], tm).astype(BF16)
        hid = jnp.dot(x, w1_bf[...], preferred_element_type=F32) + b1_ref[0]
        de = hid.shape[1] // 2
        gate = jnp.minimum(hid[:, :de], SWIGLU_LIMIT)
        lin = jnp.clip(hid[:, de:], -SWIGLU_LIMIT, SWIGLU_LIMIT)
        act = gate * jax.nn.sigmoid(SWIGLU_ALPHA * gate) * (lin + 1.0)
        y = jnp.dot(act.astype(BF16), w2_bf[...], preferred_element_type=F32) + b2_ref[0]
        _store_row_tiles(o_ref, y)

    @pl.when(i >= n_used)
    def _():
        o_ref[...] = jnp.zeros_like(o_ref)


def _expert_ffn(h_tiles, src_row8, tile_expert, n_tiles_used, w1, b1, w2, b2, layer):
    p = src_row8.shape[0]
    tm = MOE_TILE
    nt = p // tm
    _, d, dh = w1.shape
    idx3 = src_row8.reshape(nt, 1, tm)
    base = layer * N_EXPERTS
    grid_spec = pltpu.PrefetchScalarGridSpec(
        num_scalar_prefetch=2,
        grid=(nt,),
        in_specs=[
            pl.BlockSpec((1, 1, tm), lambda i, te, nu: (i, 0, 0), memory_space=pltpu.SMEM),
            pl.BlockSpec((1, 1, tm), lambda i, te, nu: (jnp.minimum(i + 1, nt - 1), 0, 0),
                         memory_space=pltpu.SMEM),
            pl.BlockSpec(memory_space=pl.ANY),
            pl.BlockSpec((1, d, dh), lambda i, te, nu: (base + te[i], 0, 0)),
            pl.BlockSpec((1, 1, dh), lambda i, te, nu: (base + te[i], 0, 0)),
            pl.BlockSpec((1, dh // 2, d), lambda i, te, nu: (base + te[i], 0, 0)),
            pl.BlockSpec((1, 1, d), lambda i, te, nu: (base + te[i], 0, 0)),
        ],
        out_specs=pl.BlockSpec((tm * SUBLANES, LANES), lambda i, te, nu: (i, 0)),
        scratch_shapes=[pltpu.VMEM((2, tm * SUBLANES, LANES), F32), pltpu.VMEM((d, dh), BF16),
                        pltpu.VMEM((dh // 2, d), BF16), pltpu.SemaphoreType.DMA((2,))],
    )
    return pl.pallas_call(
        _expert_kernel,
        grid_spec=grid_spec,
        out_shape=jax.ShapeDtypeStruct((p * SUBLANES, LANES), F32),
        compiler_params=_cparams("arbitrary"),
        name="expert_ffn",
    )(tile_expert, n_tiles_used, idx3, idx3, h_tiles, w1, b1, w2, b2)


def _combine_kernel(idx_cur_ref, idx_nxt_ref, y_hbm, x_ref, w_ref, gate_ref, o_ref, ybuf, sem):
    i = pl.program_id(0)
    n = pl.num_programs(0)
    tc = x_ref.shape[0]
    slot = i % 2

    rows_per_iter = SUBLANES

    def issue_rows(idx_ref, s, g):
        for u in range(rows_per_iter):
            r = g * rows_per_iter + u
            for k in range(TOP_K):
                _row_tile_copy(y_hbm, idx_ref[0, 0, r * TOP_K + k], ybuf.at[s, k], r,
                               sem.at[s]).start(priority=k % 2)

    def wait_tile(s):
        for k in range(TOP_K):
            pltpu.make_async_copy(y_hbm.at[pl.ds(0, tc * SUBLANES)], ybuf.at[s, k], sem.at[s]).wait()

    @pl.when(i == 0)
    def _():
        def first(g, carry):
            issue_rows(idx_cur_ref, 0, g)
            return carry
        lax.fori_loop(0, tc // rows_per_iter, first, 0)

    wait_tile(slot)
    gate = gate_ref[0]

    def body(g, carry):
        issue_rows(idx_nxt_ref, 1 - slot, g)
        r0 = pl.multiple_of(g * rows_per_iter, rows_per_iter)
        w = w_ref[pl.ds(r0, rows_per_iter), :]
        acc = None
        for k in range(TOP_K):
            yk = jnp.concatenate(
                [ybuf[slot, k, pl.ds(r0 * SUBLANES + j, rows_per_iter, stride=SUBLANES), :]
                 for j in range(SUBLANES)], axis=1)
            term = w[:, k:k + 1] * yk
            acc = term if acc is None else acc + term
        o_ref[pl.ds(r0, rows_per_iter), :] = x_ref[pl.ds(r0, rows_per_iter), :] + gate * acc
        return carry

    lax.fori_loop(0, tc // rows_per_iter, body, 0)

    @pl.when(i == n - 1)
    def _():
        wait_tile(1 - slot)


def _moe_combine(y_tiles, slot8_of_pair, top_w, x, gate, seq):
    t, d = x.shape
    tc = COMBINE_TILE
    nt = t // tc
    per = seq // tc
    bsz = gate.shape[0]
    idx3 = slot8_of_pair.reshape(nt, 1, tc * TOP_K)
    return pl.pallas_call(
        _combine_kernel,
        grid=(nt,),
        in_specs=[
            pl.BlockSpec((1, 1, tc * TOP_K), lambda i: (i, 0, 0), memory_space=pltpu.SMEM),
            pl.BlockSpec((1, 1, tc * TOP_K), lambda i: (jnp.minimum(i + 1, nt - 1), 0, 0),
                         memory_space=pltpu.SMEM),
            pl.BlockSpec(memory_space=pl.ANY),
            pl.BlockSpec((tc, d), lambda i: (i, 0)),
            pl.BlockSpec((tc, TOP_K), lambda i: (i, 0)),
            pl.BlockSpec((1, 1, d), lambda i: (i // per, 0, 0)),
        ],
        out_specs=pl.BlockSpec((tc, d), lambda i: (i, 0)),
        out_shape=jax.ShapeDtypeStruct((t, d), F32),
        scratch_shapes=[pltpu.VMEM((2, TOP_K, tc * SUBLANES, LANES), F32), pltpu.SemaphoreType.DMA((2,))],
        compiler_params=_cparams("arbitrary"),
        name="moe_combine",
    )(idx3, idx3, y_tiles, x, top_w, gate.reshape(bsz, 1, d))


def _route(route):
    t = route.shape[0]
    top_w = route[:, :TOP_K]
    flat_e = route[:, TOP_K:2 * TOP_K].astype(jnp.int32).reshape(-1)
    onehot = (flat_e[:, None] == jnp.arange(N_EXPERTS, dtype=jnp.int32)[None, :]).astype(jnp.int32)
    csum = jnp.cumsum(onehot, axis=0)
    counts = csum[-1]
    padded = ((counts + MOE_TILE - 1) // MOE_TILE) * MOE_TILE
    ends = jnp.cumsum(padded)
    starts = ends - padded
    slot_of_pair = jnp.sum(onehot * (csum - 1 + starts[None, :]), axis=1)
    n_fill = N_EXPERTS * MOE_TILE
    fill_end = jnp.cumsum(padded - counts)
    fill_key = jnp.sum((jnp.arange(n_fill, dtype=jnp.int32)[:, None] >= fill_end[None, :]).astype(jnp.int32), axis=1)
    n_pairs = t * TOP_K
    pos_bits = (n_pairs + n_fill - 1).bit_length()
    assert (N_EXPERTS + 1) << pos_bits < 2 ** 31
    keys = (jnp.concatenate([flat_e, fill_key]) << pos_bits) + jnp.arange(n_pairs + n_fill, dtype=jnp.int32)
    entry = lax.sort(keys) & ((1 << pos_bits) - 1)
    src_tok = jnp.where(entry < n_pairs, entry // TOP_K, 0)
    n_tiles = (n_pairs + n_fill) // MOE_TILE
    tile_start = jnp.arange(n_tiles, dtype=jnp.int32) * MOE_TILE
    tile_expert = jnp.sum((tile_start[:, None] >= ends[None, :]).astype(jnp.int32), axis=1)
    n_used = (ends[-1] // MOE_TILE).astype(jnp.int32)
    last_e = jnp.sum(jnp.where(jnp.arange(n_tiles) == n_used - 1, tile_expert, 0))
    tile_expert = jnp.where(jnp.arange(n_tiles) < n_used, tile_expert, last_e).astype(jnp.int32)
    return (top_w, (slot_of_pair * SUBLANES).astype(jnp.int32), (src_tok * SUBLANES).astype(jnp.int32),
            tile_expert, n_used.reshape(1))


def _moe_layer(x, g, sc, sh, gate, rw, rb, w1, b1, w2, b2, layer, seq):
    h_tiles, route = _norm_router(x, g, sc, sh, rw, rb, seq)
    top_w, slot8_of_pair, src_row8, tile_expert, n_used = _route(route)
    y_tiles = _expert_ffn(h_tiles, src_row8, tile_expert, n_used, w1, b1, w2, b2, layer)
    return _moe_combine(y_tiles, slot8_of_pair, top_w, x, gate, seq)


def _chunk_pos(shape):
    return lax.broadcasted_iota(jnp.int32, shape, 0) & (CHUNK - 1)


def _inchunk_cumsum(x):
    pos = _chunk_pos(x.shape)
    sh = 1
    while sh < CHUNK:
        x = x + jnp.where(pos >= sh, pltpu.roll(x, sh, axis=0), 0.0)
        sh *= 2
    return x


def _chunk_last_rows(b):
    rows, d = b.shape
    b3 = b.reshape(rows // CHUNK, CHUNK, d)
    return jnp.broadcast_to(b3[:, CHUNK - 1:CHUNK, :], b3.shape).reshape(rows, d)


def _chunk_causal_mask(rows):
    r = lax.broadcasted_iota(jnp.int32, (rows, rows), 0)
    c = lax.broadcasted_iota(jnp.int32, (rows, rows), 1)
    return jnp.logical_and(r >= c, (r // CHUNK) == (c // CHUNK))


def _linattn_block(q, k, v, b, b_rem, chunk_decay, state_ref):
    rows = q.shape[0]
    q_in = (q * jnp.exp(b)).astype(BF16)
    k_in = (k * jnp.exp(-b)).astype(BF16)
    k_st = (k * jnp.exp(b_rem)).astype(BF16)
    vb = v.astype(BF16)
    scores = lax.dot_general(q_in, k_in, (((1,), (1,)), ((), ())), preferred_element_type=F32)
    scores = jnp.where(_chunk_causal_mask(rows), scores, 0.0)
    o_intra = jnp.dot(scores.astype(BF16), vb, preferred_element_type=F32)
    state = state_ref[...]
    o_inter = []
    for c in range(rows // CHUNK):
        sl = slice(c * CHUNK, (c + 1) * CHUNK)
        o_inter.append(jnp.dot(q_in[sl], state.astype(BF16), preferred_element_type=F32))
        upd = lax.dot_general(k_st[sl], vb[sl], (((0,), (0,)), ((), ())), preferred_element_type=F32)
        state = chunk_decay(c) * state + upd
    state_ref[...] = state
    return o_intra + jnp.concatenate(o_inter, axis=0)


HEADS_PER_STEP = 4


def _ret_kernel(q_ref, k_ref, v_ref, g_ref, cos_ref, sin_ref, lg_ref, gn_ref, o_ref, state_ref, *, k_scale):
    @pl.when(pl.program_id(2) == 0)
    def _():
        state_ref[...] = jnp.zeros_like(state_ref)

    rows = q_ref.shape[0]
    dk = q_ref.shape[1] // HEADS_PER_STEP
    dv = v_ref.shape[1] // HEADS_PER_STEP
    half = dk // 2
    cos, sin = cos_ref[...], sin_ref[...]

    def rope(t):
        t1, t2 = t[:, :half], t[:, half:]
        return jnp.concatenate([t1 * cos - t2 * sin, t1 * sin + t2 * cos], axis=1)

    pos = _chunk_pos((rows, dk)).astype(F32)
    for hh in range(HEADS_PER_STEP):
        ks, vs = slice(hh * dk, (hh + 1) * dk), slice(hh * dv, (hh + 1) * dv)
        lg = lg_ref[hh]
        b = (pos + 1.0) * lg
        b_rem = (CHUNK - 1.0 - pos) * lg
        decay = jnp.exp(CHUNK * lg[:, 0:1])
        q = rope(q_ref[:, ks])
        k = rope(k_ref[:, ks]) * k_scale
        o = _linattn_block(q, k, v_ref[:, vs], b, b_rem, lambda c: decay, state_ref.at[hh])
        g = g_ref[:, vs]
        o_ref[:, vs] = _row_norm(o, True) * gn_ref[hh] * (g * jax.nn.sigmoid(g))


def _retention_mixer(proj, gn_g, bsz, seq):
    n_heads, dv = gn_g.shape
    dk = dv // 2
    lb = SEQ_BLOCK
    hps = HEADS_PER_STEP
    n_hp = n_heads // hps
    p3 = proj.reshape(bsz, seq, proj.shape[-1])
    half = dk // 2
    inv = ROPE_BASE ** (-jnp.arange(half, dtype=F32) / half)
    ang = jnp.arange(seq, dtype=F32)[:, None] * inv[None, :]
    cos, sin = jnp.cos(ang), jnp.sin(ang)
    log_gamma = jnp.log1p(-jnp.exp2(-5.0 - jnp.arange(n_heads, dtype=F32)))
    lg = jnp.broadcast_to(log_gamma[:, None, None], (n_heads, 1, dk))
    out = pl.pallas_call(
        functools.partial(_ret_kernel, k_scale=dk ** -0.5),
        grid=(bsz, n_hp, seq // lb),
        in_specs=[pl.BlockSpec((None, lb, hps * dk), lambda b, h, s: (b, s, h)),
                  pl.BlockSpec((None, lb, hps * dk), lambda b, h, s: (b, s, n_hp + h)),
                  pl.BlockSpec((None, lb, hps * dv), lambda b, h, s: (b, s, n_hp + h)),
                  pl.BlockSpec((None, lb, hps * dv), lambda b, h, s: (b, s, 2 * n_hp + h)),
                  pl.BlockSpec((lb, half), lambda b, h, s: (s, 0)),
                  pl.BlockSpec((lb, half), lambda b, h, s: (s, 0)),
                  pl.BlockSpec((hps, 1, dk), lambda b, h, s: (h, 0, 0)),
                  pl.BlockSpec((hps, 1, dv), lambda b, h, s: (h, 0, 0))],
        out_specs=pl.BlockSpec((None, lb, hps * dv), lambda b, h, s: (b, s, h)),
        out_shape=jax.ShapeDtypeStruct((bsz, seq, n_heads * dv), F32),
        scratch_shapes=[pltpu.VMEM((hps, dk, dv), F32)],
        compiler_params=_cparams("arbitrary", "arbitrary", "arbitrary"),
        name="retention",
    )(p3, p3, p3, p3, cos, sin, lg, gn_g.reshape(n_heads, 1, dv))
    return out.reshape(bsz * seq, n_heads * dv)


def _gla_kernel(q_ref, k_ref, v_ref, r_ref, glow_ref, wg_ref, bg_ref, gn_ref, o_ref, state_ref, *, q_scale):
    @pl.when(pl.program_id(2) == 0)
    def _():
        state_ref[...] = jnp.zeros_like(state_ref)

    dk = q_ref.shape[1] // HEADS_PER_STEP
    dv = v_ref.shape[1] // HEADS_PER_STEP

    rows = q_ref.shape[0]
    z = jnp.dot(glow_ref[...], wg_ref[...], precision=HI, preferred_element_type=F32) + bg_ref[...]
    la_all = jax.nn.log_sigmoid(z) / GLA_GATE_NORM
    chunk_sel = (lax.broadcasted_iota(jnp.int32, (rows, LANES), 0) // CHUNK
                 == lax.broadcasted_iota(jnp.int32, (rows, LANES), 1)).astype(F32)
    for hh in range(HEADS_PER_STEP):
        ks, vs = slice(hh * dk, (hh + 1) * dk), slice(hh * dv, (hh + 1) * dv)
        la = la_all[:, ks]
        b = _inchunk_cumsum(la)
        b_rem = _chunk_last_rows(b) - b
        decay = jnp.exp(lax.dot_general(la, chunk_sel, (((0,), (0,)), ((), ())),
                                        precision=HI, preferred_element_type=F32))
        o = _linattn_block(q_ref[:, ks] * q_scale, k_ref[:, ks], v_ref[:, vs], b, b_rem,
                           lambda c: decay[:, c:c + 1], state_ref.at[hh])
        r = r_ref[:, vs]
        o_ref[:, vs] = _row_norm(o, False) * gn_ref[hh] * (r * jax.nn.sigmoid(r))


def _gla_proj_weight(w_in, dq, dvv):
    q_k_v = w_in[:, :2 * dq + dvv]
    g_low = w_in[:, 2 * dq + dvv:2 * dq + dvv + GLA_GATE_RANK]
    r = w_in[:, 2 * dq + dvv + GLA_GATE_RANK:]
    pad = jnp.zeros((w_in.shape[0], LANES - GLA_GATE_RANK), w_in.dtype)
    return jnp.concatenate([q_k_v, r, g_low, pad], axis=1)


def _gla_mixer(proj, w_gate, b_gate, gn_g, bsz, seq):
    n_heads, dv = gn_g.shape
    dk = dv // 2
    dq, dvv = n_heads * dk, n_heads * dv
    lb = SEQ_BLOCK
    hps = HEADS_PER_STEP
    n_hp = n_heads // hps
    p3 = proj.reshape(bsz, seq, proj.shape[-1])
    wg = jnp.pad(w_gate, ((0, LANES - GLA_GATE_RANK), (0, 0)))
    out = pl.pallas_call(
        functools.partial(_gla_kernel, q_scale=dk ** -0.5),
        grid=(bsz, n_hp, seq // lb),
        in_specs=[pl.BlockSpec((None, lb, hps * dk), lambda b, h, s: (b, s, h)),
                  pl.BlockSpec((None, lb, hps * dk), lambda b, h, s: (b, s, n_hp + h)),
                  pl.BlockSpec((None, lb, hps * dv), lambda b, h, s: (b, s, n_hp + h)),
                  pl.BlockSpec((None, lb, hps * dv), lambda b, h, s: (b, s, 2 * n_hp + h)),
                  pl.BlockSpec((None, lb, LANES), lambda b, h, s: (b, s, (2 * dq + 2 * dvv) // LANES)),
                  pl.BlockSpec((LANES, hps * dk), lambda b, h, s: (0, h)),
                  pl.BlockSpec((1, hps * dk), lambda b, h, s: (0, h)),
                  pl.BlockSpec((hps, 1, dv), lambda b, h, s: (h, 0, 0))],
        out_specs=pl.BlockSpec((None, lb, hps * dv), lambda b, h, s: (b, s, h)),
        out_shape=jax.ShapeDtypeStruct((bsz, seq, dvv), F32),
        scratch_shapes=[pltpu.VMEM((hps, dk, dv), F32)],
        compiler_params=_cparams("arbitrary", "arbitrary", "arbitrary"),
        name="gla",
    )(p3, p3, p3, p3, p3, wg, b_gate.reshape(1, dq), gn_g.reshape(n_heads, 1, dv))
    return out.reshape(bsz * seq, dvv)


def _att_kernel(q_ref, k_ref, v_ref, bias_ref, o_ref, kpad, vpad, *, scale, dh):
    seq, width = q_ref.shape
    n_pair = width // dh
    kpad[0:ATT_LEFT, :] = jnp.zeros((ATT_LEFT, width), F32)
    vpad[0:ATT_LEFT, :] = jnp.zeros((ATT_LEFT, width), F32)
    kpad[ATT_LEFT:ATT_LEFT + seq, :] = k_ref[...]
    vpad[ATT_LEFT:ATT_LEFT + seq, :] = v_ref[...]
    lane = lax.broadcasted_iota(jnp.int32, (ATT_QBLOCK, width), 1)
    jcol = lax.broadcasted_iota(jnp.int32, (n_pair * ATT_QBLOCK, ATT_WINDOW), 1)
    in_head = [jnp.logical_and(lane >= h * dh, lane < (h + 1) * dh) for h in range(n_pair)]

    def block(start, left_edge):
        q = q_ref[pl.ds(start, ATT_QBLOCK), :] * scale
        kb = kpad[pl.ds(start, ATT_WINDOW), :].astype(BF16)
        vb = vpad[pl.ds(start, ATT_WINDOW), :].astype(BF16)
        q2 = jnp.concatenate([jnp.where(m, q, 0.0) for m in in_head], axis=0).astype(BF16)
        s = lax.dot_general(q2, kb, (((1,), (1,)), ((), ())), preferred_element_type=F32) + bias_ref[...]
        if left_edge:
            s = jnp.where(jcol >= ATT_LEFT - start, s, -jnp.inf)
        p = jnp.exp(s - jnp.max(s, axis=-1, keepdims=True))
        o2 = jnp.dot(p.astype(BF16), vb, preferred_element_type=F32) / jnp.sum(p, axis=-1, keepdims=True)
        out = o2[0:ATT_QBLOCK, :]
        for h in range(1, n_pair):
            out = jnp.where(in_head[h], o2[h * ATT_QBLOCK:(h + 1) * ATT_QBLOCK, :], out)
        o_ref[pl.ds(start, ATT_QBLOCK), :] = out

    n_edge = ATT_LEFT // ATT_QBLOCK
    for c in range(n_edge):
        block(c * ATT_QBLOCK, True)

    def body(c, carry):
        block(pl.multiple_of(c * ATT_QBLOCK, ATT_QBLOCK), False)
        return carry

    lax.fori_loop(n_edge, seq // ATT_QBLOCK, body, 0, unroll=4)


def _attention_mixer(proj, rel_bias, bsz, seq):
    d = proj.shape[-1] // 3
    n_heads = rel_bias.shape[0]
    dh = d // n_heads
    n_pair = LANES // dh
    groups = n_heads // n_pair
    qo = jnp.arange(ATT_QBLOCK)[:, None]
    kj = jnp.arange(ATT_WINDOW)[None, :]
    lo = (qo // CHUNK) * CHUNK
    in_window = jnp.logical_and(kj >= lo, kj < lo + ATT_LEFT + CHUNK)
    d_max = ATT_QBLOCK - 1 + ATT_LEFT
    d_min = -(ATT_WINDOW - 1 - ATT_LEFT)
    n_rel = rel_bias.shape[1]
    ext = jnp.concatenate(
        [jnp.broadcast_to(rel_bias[:, :1], (n_heads, -(CHUNK - 1) - d_min)), rel_bias,
         jnp.broadcast_to(rel_bias[:, n_rel - 1:], (n_heads, d_max - ATT_MAX_REL))], axis=1)
    rows = [ext[:, q_ + ATT_LEFT - (ATT_WINDOW - 1) - d_min:q_ + ATT_LEFT - d_min + 1] for q_ in range(ATT_QBLOCK)]
    table = jnp.flip(jnp.stack(rows, axis=1), axis=2).astype(F32)
    bias = jnp.where(in_window[None], table, -jnp.inf)
    bias = bias.reshape(groups, n_pair * ATT_QBLOCK, ATT_WINDOW)
    p3 = proj.reshape(bsz, seq, 3 * d)
    out = pl.pallas_call(
        functools.partial(_att_kernel, scale=dh ** -0.5, dh=dh),
        grid=(bsz, groups),
        in_specs=[pl.BlockSpec((None, seq, LANES), lambda b, g: (b, 0, g)),
                  pl.BlockSpec((None, seq, LANES), lambda b, g: (b, 0, groups + g)),
                  pl.BlockSpec((None, seq, LANES), lambda b, g: (b, 0, 2 * groups + g)),
                  pl.BlockSpec((None, n_pair * ATT_QBLOCK, ATT_WINDOW), lambda b, g: (g, 0, 0))],
        out_specs=pl.BlockSpec((None, seq, LANES), lambda b, g: (b, 0, g)),
        out_shape=jax.ShapeDtypeStruct((bsz, seq, d), F32),
        scratch_shapes=[pltpu.VMEM((ATT_LEFT + seq, LANES), F32), pltpu.VMEM((ATT_LEFT + seq, LANES), F32)],
        compiler_params=_cparams("arbitrary", "arbitrary"),
        name="chunk_attention",
    )(p3, p3, p3, bias)
    return out.reshape(bsz * seq, d)


def _mlstm_pre_kernel(xm_ref, cw_ref, cb_ref, wq_ref, wk_ref, wv_ref, wg_ref, bg_ref,
                      q_ref, k_ref, v_ref, xc_ref, g_ref, xpad, *, per, k_scale):
    i = pl.program_id(0)
    tm, inner = xm_ref.shape

    @pl.when(i % per == 0)
    def _():
        xpad[0:CONV_PAD, :] = jnp.zeros((CONV_PAD, inner), F32)

    @pl.when(i % per != 0)
    def _():
        xpad[0:CONV_PAD, :] = xpad[tm:tm + CONV_PAD, :]

    xm = xm_ref[...]
    xpad[CONV_PAD:CONV_PAD + tm, :] = xm
    acc = jnp.broadcast_to(cb_ref[...], (tm, inner))
    for j in range(MLSTM_CONV):
        off = CONV_PAD - (MLSTM_CONV - 1) + j
        acc = acc + cw_ref[j:j + 1, :] * xpad[off:off + tm, :]
    xc = acc * jax.nn.sigmoid(acc)
    xc_ref[...] = xc

    def blockdiag(t, w_ref):
        tb = t.astype(BF16)
        return jnp.concatenate(
            [jnp.dot(tb[:, g * LANES:(g + 1) * LANES], w_ref[g], preferred_element_type=F32)
             for g in range(inner // LANES)], axis=1)

    q = blockdiag(xc, wq_ref)
    k = blockdiag(xc, wk_ref) * k_scale
    v = blockdiag(xm, wv_ref)
    qb, kb, vb = q.astype(BF16), k.astype(BF16), v.astype(BF16)
    q_ref[...] = qb
    k_ref[...] = kb
    v_ref[...] = vb
    g_ref[...] = (jnp.dot(qb, wg_ref[0:inner, :], preferred_element_type=F32)
                  + jnp.dot(kb, wg_ref[inner:2 * inner, :], preferred_element_type=F32)
                  + jnp.dot(vb, wg_ref[2 * inner:3 * inner, :], preferred_element_type=F32) + bg_ref[...])


def _blockdiag_tiles(w):
    nb, c, _ = w.shape
    per = LANES // c
    wt = w.reshape(nb // per, per, c, c)
    t = jnp.einsum('gpcd,pq->gpcqd', wt, jnp.eye(per, dtype=w.dtype))
    return t.reshape(nb // per, LANES, LANES).astype(BF16)


def _mlstm_rec_kernel(q_ref, k_ref, v_ref, g_ref, xc_ref, z_ref, gn_ref, skip_ref, o_ref,
                      c_ref, n_ref, m_ref, *, n_heads):
    head0 = pl.program_id(1) * HEADS_PER_STEP
    dh = q_ref.shape[1] // HEADS_PER_STEP

    @pl.when(pl.program_id(2) == 0)
    def _():
        c_ref[...] = jnp.zeros_like(c_ref)
        n_ref[...] = jnp.zeros_like(n_ref)
        m_ref[...] = jnp.zeros_like(m_ref)

    rows = q_ref.shape[0]
    n_chunks = rows // CHUNK
    mask = _chunk_causal_mask(rows)
    lane = lax.broadcasted_iota(jnp.int32, (rows, LANES), 1)
    sub = lax.broadcasted_iota(jnp.int32, (LANES, rows), 0)

    def col(t, idx):
        return jnp.sum(jnp.where(lane == idx, t, 0.0), axis=1, keepdims=True)

    def row(t_t, idx):
        return jnp.sum(jnp.where(sub == idx, t_t, 0.0), axis=0, keepdims=True)

    def per_chunk(vals):
        return jnp.concatenate([jnp.broadcast_to(x, (CHUNK, 1)) for x in vals], axis=0)

    gates = g_ref[...]
    bcum = _inchunk_cumsum(jax.nn.log_sigmoid(gates))
    gates_t, bcum_t = gates.T, bcum.T
    for hh in range(HEADS_PER_STEP):
        hs = slice(hh * dh, (hh + 1) * dh)
        i_idx, f_idx = head0 + hh, n_heads + head0 + hh
        i_col, b_col = col(gates, i_idx), col(bcum, f_idx)
        b_last_rows = _chunk_last_rows(b_col)
        log_w = b_last_rows - b_col + i_col
        m_start, m_next, keep = [], [], []
        m_c = m_ref[hh, :, 0:1]
        for c in range(n_chunks):
            sl = slice(c * CHUNK, (c + 1) * CHUNK)
            log_keep = b_last_rows[c * CHUNK:c * CHUNK + 1, :] + m_c
            m_n = jnp.maximum(log_keep, jnp.max(log_w[sl], axis=0, keepdims=True))
            m_start.append(m_c)
            m_next.append(m_n)
            keep.append(jnp.exp(log_keep - m_n))
            m_c = m_n
        m_ref[hh] = jnp.broadcast_to(m_c, (1, LANES))
        log_intra = jnp.where(mask, b_col - row(bcum_t, f_idx) + row(gates_t, i_idx), -jnp.inf)
        log_inter = b_col + per_chunk(m_start)
        m_row = jnp.maximum(log_inter, jnp.max(log_intra, axis=1, keepdims=True))
        w_intra = jnp.exp(log_intra - m_row)
        w_inter = jnp.exp(log_inter - m_row)
        qb, kb, vb = q_ref[:, hs], k_ref[:, hs], v_ref[:, hs]
        q, k = qb.astype(F32), kb.astype(F32)
        s = lax.dot_general(qb, kb, (((1,), (1,)), ((), ())), preferred_element_type=F32) * w_intra
        num_intra = jnp.dot(s.astype(BF16), vb, preferred_element_type=F32)
        den_intra = jnp.sum(s, axis=1, keepdims=True)
        wk = jnp.exp(log_w - per_chunk(m_next)) * k
        wkb = wk.astype(BF16)
        cmat, nvec = c_ref[hh], n_ref[hh]
        num_inter, den_inter = [], []
        for c in range(n_chunks):
            sl = slice(c * CHUNK, (c + 1) * CHUNK)
            num_inter.append(jnp.dot(qb[sl], cmat.astype(BF16), preferred_element_type=F32))
            den_inter.append(jnp.sum(q[sl] * nvec, axis=1, keepdims=True))
            upd = lax.dot_general(wkb[sl], vb[sl], (((0,), (0,)), ((), ())), preferred_element_type=F32)
            cmat = keep[c] * cmat + upd
            nvec = keep[c] * nvec + jnp.sum(wk[sl], axis=0, keepdims=True)
        c_ref[hh] = cmat
        n_ref[hh] = nvec
        num = w_inter * jnp.concatenate(num_inter, axis=0) + num_intra
        den = w_inter * jnp.concatenate(den_inter, axis=0) + den_intra
        hc = num / jnp.maximum(jnp.abs(den), jnp.exp(-m_row))
        z = z_ref[:, hs]
        o_ref[:, hs] = ((_row_norm(hc, True) * gn_ref[hh] + skip_ref[:, hs] * xc_ref[:, hs])
                        * (z * jax.nn.sigmoid(z)))


def _mlstm_mixer(proj, conv_w, conv_b, w_q, w_k, w_v, w_gates, b_gates, gn_g, skip, bsz, seq):
    t = proj.shape[0]
    inner = proj.shape[1] // 2
    n_heads, dh = gn_g.shape
    tm = MLSTM_PRE_TILE
    n_g = w_gates.shape[1]
    wg = jnp.pad(w_gates, ((0, 0), (0, LANES - n_g))).astype(BF16)
    bg = jnp.pad(b_gates.reshape(1, n_g), ((0, 0), (0, LANES - n_g)))
    tile_spec = pl.BlockSpec((tm, inner), lambda i: (i, 0))
    bd_spec = pl.BlockSpec((inner // LANES, LANES, LANES), lambda i: (0, 0, 0))
    q, k, v, xc, gates = pl.pallas_call(
        functools.partial(_mlstm_pre_kernel, per=seq // tm, k_scale=dh ** -0.5),
        grid=(t // tm,),
        in_specs=[tile_spec,
                  pl.BlockSpec((MLSTM_CONV, inner), lambda i: (0, 0)),
                  pl.BlockSpec((1, inner), lambda i: (0, 0)),
                  bd_spec, bd_spec, bd_spec,
                  pl.BlockSpec((3 * inner, LANES), lambda i: (0, 0)),
                  pl.BlockSpec((1, LANES), lambda i: (0, 0))],
        out_specs=[tile_spec, tile_spec, tile_spec, tile_spec, pl.BlockSpec((tm, LANES), lambda i: (i, 0))],
        out_shape=[jax.ShapeDtypeStruct((t, inner), BF16)] * 3
        + [jax.ShapeDtypeStruct((t, inner), F32), jax.ShapeDtypeStruct((t, LANES), F32)],
        scratch_shapes=[pltpu.VMEM((tm + CONV_PAD, inner), F32)],
        compiler_params=_cparams("arbitrary"),
        name="mlstm_pre",
    )(proj, conv_w, conv_b.reshape(1, inner), _blockdiag_tiles(w_q), _blockdiag_tiles(w_k),
      _blockdiag_tiles(w_v), wg, bg)

    lb = SEQ_BLOCK
    hps = HEADS_PER_STEP
    n_hp = n_heads // hps
    r3 = lambda a: a.reshape(bsz, seq, a.shape[-1])
    head_spec = pl.BlockSpec((None, lb, hps * dh), lambda b, h, s: (b, s, h))
    out = pl.pallas_call(
        functools.partial(_mlstm_rec_kernel, n_heads=n_heads),
        grid=(bsz, n_hp, seq // lb),
        in_specs=[head_spec, head_spec, head_spec,
                  pl.BlockSpec((None, lb, LANES), lambda b, h, s: (b, s, 0)),
                  head_spec,
                  pl.BlockSpec((None, lb, hps * dh), lambda b, h, s: (b, s, n_hp + h)),
                  pl.BlockSpec((hps, 1, dh), lambda b, h, s: (h, 0, 0)),
                  pl.BlockSpec((1, hps * dh), lambda b, h, s: (0, h))],
        out_specs=head_spec,
        out_shape=jax.ShapeDtypeStruct((bsz, seq, inner), F32),
        scratch_shapes=[pltpu.VMEM((hps, dh, dh), F32), pltpu.VMEM((hps, 1, dh), F32),
                        pltpu.VMEM((hps, 1, LANES), F32)],
        compiler_params=_cparams("arbitrary", "arbitrary", "arbitrary"),
        name="mlstm_rec",
    )(r3(q), r3(k), r3(v), r3(gates), r3(xc), r3(proj), gn_g.reshape(n_heads, 1, dh), skip.reshape(1, inner))
    return out.reshape(t, inner)


def kernel(x, c, ada_w, ada_b, norm_mix_g, norm_ffn_g, norm_final_g, ret_w_in, ret_gn_g, ret_w_out, att_w_in, att_rel_bias, att_w_out, gla_w_in, gla_w_gate, gla_b_gate, gla_gn_g, gla_w_out, mlstm_w_in, mlstm_conv_w, mlstm_conv_b, mlstm_w_q, mlstm_w_k, mlstm_w_v, mlstm_w_gates, mlstm_b_gates, mlstm_gn_g, mlstm_skip, mlstm_w_out, router_w, router_b, moe_w1, moe_b1, moe_w2, moe_b2):
    bsz, seq, d = x.shape
    depth, n_exp, _, dh2 = moe_w1.shape
    xt = x.reshape(bsz * seq, d)
    mod = _ada_mod(c, ada_w, ada_b)
    w1_all = moe_w1.reshape(depth * n_exp, d, dh2)
    w2_all = moe_w2.reshape(depth * n_exp, dh2 // 2, d)
    b1_all = moe_b1.reshape(depth * n_exp, 1, dh2)
    b2_all = moe_b2.reshape(depth * n_exp, 1, d)
    for i in range(depth):
        kind, j = i % 4, i // 4
        sh_a, sc_a, g_a, sh_f, sc_f, g_f = jnp.split(mod[i], 6, axis=-1)
        if kind == 0:
            proj = _norm_mm(xt, norm_mix_g[i], sc_a, sh_a, ret_w_in[j].astype(BF16), seq)
            y = _retention_mixer(proj, ret_gn_g[j], bsz, seq)
            w_out = ret_w_out[j]
        elif kind == 1:
            proj = _norm_mm(xt, norm_mix_g[i], sc_a, sh_a, att_w_in[j].astype(BF16), seq)
            y = _attention_mixer(proj, att_rel_bias[j], bsz, seq)
            w_out = att_w_out[j]
        elif kind == 2:
            n_heads, dv = gla_gn_g[j].shape
            w_gla = _gla_proj_weight(gla_w_in[j], n_heads * dv // 2, n_heads * dv).astype(BF16)
            proj = _norm_mm(xt, norm_mix_g[i], sc_a, sh_a, w_gla, seq, tn=w_gla.shape[1] // 5)
            y = _gla_mixer(proj, gla_w_gate[j], gla_b_gate[j], gla_gn_g[j], bsz, seq)
            w_out = gla_w_out[j]
        else:
            proj = _norm_mm(xt, norm_mix_g[i], sc_a, sh_a, mlstm_w_in[j].astype(BF16), seq)
            y = _mlstm_mixer(proj, mlstm_conv_w[j], mlstm_conv_b[j], mlstm_w_q[j], mlstm_w_k[j], mlstm_w_v[j],
                             mlstm_w_gates[j], mlstm_b_gates[j], mlstm_gn_g[j], mlstm_skip[j], bsz, seq)
            w_out = mlstm_w_out[j]
        xt = _mm_res(y, w_out.astype(BF16), xt, g_a, seq)
        xt = _moe_layer(xt, norm_ffn_g[i], sc_f, sh_f, g_f, router_w[i], router_b[i],
                        w1_all, b1_all, w2_all, b2_all, i, seq)
    return _final_norm(xt, norm_final_g).reshape(bsz, seq, d)
```

```python
import functools

import jax
import jax.numpy as jnp
from jax import lax
from jax.experimental import pallas as pl
from jax.experimental.pallas import tpu as pltpu

F32 = jnp.float32
BF16 = jnp.bfloat16
HI = lax.Precision.HIGHEST

CHUNK = 64
EPS = 1e-6
ROPE_BASE = 10000.0
ATT_LEFT_CHUNKS = 8
ATT_MAX_REL = 256
GLA_GATE_RANK = 16
GLA_GATE_NORM = 16.0
MLSTM_CONV = 4
N_EXPERTS = 32
TOP_K = 4
SWIGLU_LIMIT = 7.0
SWIGLU_ALPHA = 1.702

VMEM_LIMIT_BYTES = 56 * 1024 * 1024
LANES = 128
SUBLANES = 8
MOE_TILE = 512
COMBINE_TILE = 256
COPIES_PER_ITER = 32
SEQ_BLOCK = 256
MLSTM_PRE_TILE = 256
CONV_PAD = 8
ATT_QBLOCK = 2 * CHUNK
ATT_LEFT = ATT_LEFT_CHUNKS * CHUNK
ATT_WINDOW = ATT_LEFT + ATT_QBLOCK


def _cparams(*sem):
    return pltpu.CompilerParams(dimension_semantics=sem, vmem_limit_bytes=VMEM_LIMIT_BYTES)


def _dot_bf16(a, b, dims=(((1,), (0,)), ((), ()))):
    return lax.dot_general(a.astype(BF16), b.astype(BF16), dims, preferred_element_type=F32)


def _tril_ones(n):
    row = lax.broadcasted_iota(jnp.int32, (n, n), 0)
    col = lax.broadcasted_iota(jnp.int32, (n, n), 1)
    return row >= col


def _row_norm(x, center):
    if center:
        x = x - jnp.mean(x, axis=-1, keepdims=True)
    return x * lax.rsqrt(jnp.mean(x * x, axis=-1, keepdims=True) + EPS)


def _load_row_tiles(ref, rows):
    return jnp.concatenate([ref[pl.ds(j, rows, stride=SUBLANES), :] for j in range(SUBLANES)], axis=1)


def _store_row_tiles(ref, val):
    rows = val.shape[0]
    for j in range(SUBLANES):
        ref[pl.ds(j, rows, stride=SUBLANES), :] = val[:, j * LANES:(j + 1) * LANES]


def _ada_kernel(c_ref, w_ref, b_ref, o_ref):
    c = c_ref[...]
    cond = c * jax.nn.sigmoid(c)
    o_ref[0] = jnp.dot(cond, w_ref[0], preferred_element_type=F32, precision=HI) + b_ref[0]


def _ada_mod(c, ada_w, ada_b):
    depth, d, n = ada_w.shape
    b = c.shape[0]
    tn = 1536
    return pl.pallas_call(
        _ada_kernel,
        grid=(depth, n // tn),
        in_specs=[pl.BlockSpec((b, d), lambda l, j: (0, 0)),
                  pl.BlockSpec((1, d, tn), lambda l, j: (l, 0, j)),
                  pl.BlockSpec((1, 1, tn), lambda l, j: (l, 0, j))],
        out_specs=pl.BlockSpec((1, b, tn), lambda l, j: (l, 0, j)),
        out_shape=jax.ShapeDtypeStruct((depth, b, n), F32),
        compiler_params=_cparams("arbitrary", "arbitrary"),
        name="ada_mod",
    )(c, ada_w, ada_b.reshape(depth, 1, n))


def _modulated_norm(x, g, sc, sh):
    ms = jnp.mean(x * x, axis=-1, keepdims=True)
    return (x * lax.rsqrt(ms + EPS) * g) * (1.0 + sc) + sh


def _norm_mm_kernel(x_ref, g_ref, sc_ref, sh_ref, w_ref, o_ref, h_scr):
    @pl.when(pl.program_id(1) == 0)
    def _():
        h = _modulated_norm(x_ref[...], g_ref[...], sc_ref[0], sh_ref[0])
        h_scr[...] = h.astype(BF16)

    o_ref[...] = jnp.dot(h_scr[...], w_ref[...], preferred_element_type=F32)


def _norm_mm(x, g, sc, sh, w, seq, tm=1024, tn=512):
    t, d = x.shape
    n = w.shape[1]
    per = seq // tm
    bsz = sc.shape[0]
    return pl.pallas_call(
        _norm_mm_kernel,
        grid=(t // tm, n // tn),
        in_specs=[pl.BlockSpec((tm, d), lambda i, j: (i, 0)),
                  pl.BlockSpec((1, d), lambda i, j: (0, 0)),
                  pl.BlockSpec((1, 1, d), lambda i, j: (i // per, 0, 0)),
                  pl.BlockSpec((1, 1, d), lambda i, j: (i // per, 0, 0)),
                  pl.BlockSpec((d, tn), lambda i, j: (0, j))],
        out_specs=pl.BlockSpec((tm, tn), lambda i, j: (i, j)),
        out_shape=jax.ShapeDtypeStruct((t, n), F32),
        scratch_shapes=[pltpu.VMEM((tm, d), BF16)],
        compiler_params=_cparams("arbitrary", "arbitrary"),
        name="norm_mm",
    )(x, g.reshape(1, d), sc.reshape(bsz, 1, d), sh.reshape(bsz, 1, d), w)


def _mm_res_kernel(y_ref, w_ref, x_ref, gate_ref, o_ref):
    acc = jnp.dot(y_ref[...].astype(BF16), w_ref[...], preferred_element_type=F32)
    o_ref[...] = x_ref[...] + gate_ref[0] * acc


def _mm_res(y, w, x, gate, seq, tm=512):
    t, k = y.shape
    d = w.shape[1]
    per = seq // tm
    bsz = gate.shape[0]
    return pl.pallas_call(
        _mm_res_kernel,
        grid=(t // tm,),
        in_specs=[pl.BlockSpec((tm, k), lambda i: (i, 0)),
                  pl.BlockSpec((k, d), lambda i: (0, 0)),
                  pl.BlockSpec((tm, d), lambda i: (i, 0)),
                  pl.BlockSpec((1, 1, d), lambda i: (i // per, 0, 0))],
        out_specs=pl.BlockSpec((tm, d), lambda i: (i, 0)),
        out_shape=jax.ShapeDtypeStruct((t, d), F32),
        compiler_params=_cparams("arbitrary"),
        name="mm_res",
    )(y, w, x, gate.reshape(bsz, 1, d))


def _final_norm_kernel(x_ref, g_ref, o_ref):
    x = x_ref[...]
    ms = jnp.mean(x * x, axis=-1, keepdims=True)
    o_ref[...] = x * lax.rsqrt(ms + EPS) * g_ref[...]


def _final_norm(x, g, tm=1024):
    t, d = x.shape
    return pl.pallas_call(
        _final_norm_kernel,
        grid=(t // tm,),
        in_specs=[pl.BlockSpec((tm, d), lambda i: (i, 0)),
                  pl.BlockSpec((1, d), lambda i: (0, 0))],
        out_specs=pl.BlockSpec((tm, d), lambda i: (i, 0)),
        out_shape=jax.ShapeDtypeStruct((t, d), F32),
        compiler_params=_cparams("arbitrary"),
        name="final_norm",
    )(x, g.reshape(1, d))


def _norm_router_kernel(x_ref, g_ref, sc_ref, sh_ref, rw_ref, rb_ref, h_ref, route_ref, *, n_experts):
    h = _modulated_norm(x_ref[...], g_ref[...], sc_ref[0], sh_ref[0])
    _store_row_tiles(h_ref, h)
    logits = jnp.dot(h, rw_ref[...], preferred_element_type=F32, precision=HI) + rb_ref[...]
    lane = lax.broadcasted_iota(jnp.int32, logits.shape, 1)
    rem = jnp.where(lane < n_experts, logits, -jnp.inf)
    vals, idxs = [], []
    for _ in range(TOP_K):
        m = jnp.max(rem, axis=1, keepdims=True)
        idx = jnp.min(jnp.where(rem == m, lane, LANES), axis=1, keepdims=True)
        vals.append(m)
        idxs.append(idx)
        rem = jnp.where(lane == idx, -jnp.inf, rem)
    ex = [jnp.exp(v - vals[0]) for v in vals]
    inv = 1.0 / sum(ex[1:], ex[0])
    out = jnp.zeros(logits.shape, F32)
    for k in range(TOP_K):
        out = jnp.where(lane == k, ex[k] * inv, out)
        out = jnp.where(lane == TOP_K + k, idxs[k].astype(F32), out)
    route_ref[...] = out


def _norm_router(x, g, sc, sh, rw, rb, seq, tm=512):
    t, d = x.shape
    per = seq // tm
    bsz = sc.shape[0]
    e = rw.shape[1]
    rw_p = jnp.pad(rw, ((0, 0), (0, LANES - e)))
    rb_p = jnp.pad(rb.reshape(1, e), ((0, 0), (0, LANES - e)))
    return pl.pallas_call(
        functools.partial(_norm_router_kernel, n_experts=e),
        grid=(t // tm,),
        in_specs=[pl.BlockSpec((tm, d), lambda i: (i, 0)),
                  pl.BlockSpec((1, d), lambda i: (0, 0)),
                  pl.BlockSpec((1, 1, d), lambda i: (i // per, 0, 0)),
                  pl.BlockSpec((1, 1, d), lambda i: (i // per, 0, 0)),
                  pl.BlockSpec((d, LANES), lambda i: (0, 0)),
                  pl.BlockSpec((1, LANES), lambda i: (0, 0))],
        out_specs=[pl.BlockSpec((tm * SUBLANES, LANES), lambda i: (i, 0)),
                   pl.BlockSpec((tm, LANES), lambda i: (i, 0))],
        out_shape=[jax.ShapeDtypeStruct((t * SUBLANES, LANES), F32),
                   jax.ShapeDtypeStruct((t, LANES), F32)],
        compiler_params=_cparams("arbitrary"),
        name="norm_router",
    )(x, g.reshape(1, d), sc.reshape(bsz, 1, d), sh.reshape(bsz, 1, d), rw_p, rb_p)


def _row_tile_copy(src_hbm, src_row8, dst_vmem, dst_row, sem):
    return pltpu.make_async_copy(
        src_hbm.at[pl.ds(pl.multiple_of(src_row8, SUBLANES), SUBLANES)],
        dst_vmem.at[pl.ds(pl.multiple_of(dst_row * SUBLANES, SUBLANES), SUBLANES)], sem)


def _expert_kernel(te_ref, nt_ref, idx_cur_ref, idx_nxt_ref, h_hbm, w1_ref, b1_ref, w2_ref, b2_ref,
                   o_ref, xbuf, w1_bf, w2_bf, sem):
    i = pl.program_id(0)
    n_used = nt_ref[0]
    tm = xbuf.shape[1] // SUBLANES
    slot = i % 2

    @pl.when(jnp.logical_or(i == 0, te_ref[i] != te_ref[jnp.maximum(i - 1, 0)]))
    def _():
        w1_bf[...] = w1_ref[0].astype(BF16)
        w2_bf[...] = w2_ref[0].astype(BF16)

    def issue(idx_ref, s):
        def body(g, carry):
            for u in range(COPIES_PER_ITER):
                r = g * COPIES_PER_ITER + u
                _row_tile_copy(h_hbm, idx_ref[0, 0, r], xbuf.at[s], r, sem.at[s]).start(priority=u % 2)
            return carry
        lax.fori_loop(0, tm // COPIES_PER_ITER, body, 0)

    @pl.when(jnp.logical_and(i == 0, n_used > 0))
    def _():
        issue(idx_cur_ref, 0)

    @pl.when(i + 1 < n_used)
    def _():
        issue(idx_nxt_ref, 1 - slot)

    @pl.when(i < n_used)
    def _():
        pltpu.make_async_copy(h_hbm.at[pl.ds(0, tm * SUBLANES)], xbuf.at[slot], sem.at[slot]).wait()
        x = _load_row_tiles(xbuf.at[slot], tm).astype(BF16)
        hid = jnp.dot(x, w1_bf[...], preferred_element_type=F32) + b1_ref[0]
        de = hid.shape[1] // 2
        gate = jnp.minimum(hid[:, :de], SWIGLU_LIMIT)
        lin = jnp.clip(hid[:, de:], -SWIGLU_LIMIT, SWIGLU_LIMIT)
        act = gate * jax.nn.sigmoid(SWIGLU_ALPHA * gate) * (lin + 1.0)
        y = jnp.dot(act.astype(BF16), w2_bf[...], preferred_element_type=F32) + b2_ref[0]
        _store_row_tiles(o_ref, y)

    @pl.when(i >= n_used)
    def _():
        o_ref[...] = jnp.zeros_like(o_ref)


def _expert_ffn(h_tiles, src_row8, tile_expert, n_tiles_used, w1, b1, w2, b2, layer):
    p = src_row8.shape[0]
    tm = MOE_TILE
    nt = p // tm
    _, d, dh = w1.shape
    idx3 = src_row8.reshape(nt, 1, tm)
    base = layer * N_EXPERTS
    grid_spec = pltpu.PrefetchScalarGridSpec(
        num_scalar_prefetch=2,
        grid=(nt,),
        in_specs=[
            pl.BlockSpec((1, 1, tm), lambda i, te, nu: (i, 0, 0), memory_space=pltpu.SMEM),
            pl.BlockSpec((1, 1, tm), lambda i, te, nu: (jnp.minimum(i + 1, nt - 1), 0, 0),
                         memory_space=pltpu.SMEM),
            pl.BlockSpec(memory_space=pl.ANY),
            pl.BlockSpec((1, d, dh), lambda i, te, nu: (base + te[i], 0, 0)),
            pl.BlockSpec((1, 1, dh), lambda i, te, nu: (base + te[i], 0, 0)),
            pl.BlockSpec((1, dh // 2, d), lambda i, te, nu: (base + te[i], 0, 0)),
            pl.BlockSpec((1, 1, d), lambda i, te, nu: (base + te[i], 0, 0)),
        ],
        out_specs=pl.BlockSpec((tm * SUBLANES, LANES), lambda i, te, nu: (i, 0)),
        scratch_shapes=[pltpu.VMEM((2, tm * SUBLANES, LANES), F32), pltpu.VMEM((d, dh), BF16),
                        pltpu.VMEM((dh // 2, d), BF16), pltpu.SemaphoreType.DMA((2,))],
    )
    return pl.pallas_call(
        _expert_kernel,
        grid_spec=grid_spec,
        out_shape=jax.ShapeDtypeStruct((p * SUBLANES, LANES), F32),
        compiler_params=_cparams("arbitrary"),
        name="expert_ffn",
    )(tile_expert, n_tiles_used, idx3, idx3, h_tiles, w1, b1, w2, b2)


def _combine_kernel(idx_cur_ref, idx_nxt_ref, y_hbm, x_ref, w_ref, gate_ref, o_ref, ybuf, sem):
    i = pl.program_id(0)
    n = pl.num_programs(0)
    tc = x_ref.shape[0]
    slot = i % 2

    def issue(idx_ref, s):
        rows_per_iter = COPIES_PER_ITER // TOP_K

        def body(g, carry):
            for u in range(rows_per_iter):
                r = g * rows_per_iter + u
                for k in range(TOP_K):
                    _row_tile_copy(y_hbm, idx_ref[0, 0, r * TOP_K + k], ybuf.at[s, k], r,
                                   sem.at[s]).start(priority=k % 2)
            return carry
        lax.fori_loop(0, tc // rows_per_iter, body, 0)

    @pl.when(i == 0)
    def _():
        issue(idx_cur_ref, 0)

    @pl.when(i + 1 < n)
    def _():
        issue(idx_nxt_ref, 1 - slot)

    for k in range(TOP_K):
        pltpu.make_async_copy(y_hbm.at[pl.ds(0, tc * SUBLANES)], ybuf.at[slot, k], sem.at[slot]).wait()
    w = w_ref[...]
    acc = w[:, 0:1] * _load_row_tiles(ybuf.at[slot, 0], tc)
    for k in range(1, TOP_K):
        acc = acc + w[:, k:k + 1] * _load_row_tiles(ybuf.at[slot, k], tc)
    o_ref[...] = x_ref[...] + gate_ref[0] * acc


def _moe_combine(y_tiles, slot8_of_pair, top_w, x, gate, seq):
    t, d = x.shape
    tc = COMBINE_TILE
    nt = t // tc
    per = seq // tc
    bsz = gate.shape[0]
    idx3 = slot8_of_pair.reshape(nt, 1, tc * TOP_K)
    return pl.pallas_call(
        _combine_kernel,
        grid=(nt,),
        in_specs=[
            pl.BlockSpec((1, 1, tc * TOP_K), lambda i: (i, 0, 0), memory_space=pltpu.SMEM),
            pl.BlockSpec((1, 1, tc * TOP_K), lambda i: (jnp.minimum(i + 1, nt - 1), 0, 0),
                         memory_space=pltpu.SMEM),
            pl.BlockSpec(memory_space=pl.ANY),
            pl.BlockSpec((tc, d), lambda i: (i, 0)),
            pl.BlockSpec((tc, TOP_K), lambda i: (i, 0)),
            pl.BlockSpec((1, 1, d), lambda i: (i // per, 0, 0)),
        ],
        out_specs=pl.BlockSpec((tc, d), lambda i: (i, 0)),
        out_shape=jax.ShapeDtypeStruct((t, d), F32),
        scratch_shapes=[pltpu.VMEM((2, TOP_K, tc * SUBLANES, LANES), F32), pltpu.SemaphoreType.DMA((2,))],
        compiler_params=_cparams("arbitrary"),
        name="moe_combine",
    )(idx3, idx3, y_tiles, x, top_w, gate.reshape(bsz, 1, d))


def _route(route):
    t = route.shape[0]
    top_w = route[:, :TOP_K]
    flat_e = route[:, TOP_K:2 * TOP_K].astype(jnp.int32).reshape(-1)
    onehot = (flat_e[:, None] == jnp.arange(N_EXPERTS, dtype=jnp.int32)[None, :]).astype(jnp.int32)
    csum = jnp.cumsum(onehot, axis=0)
    counts = csum[-1]
    padded = ((counts + MOE_TILE - 1) // MOE_TILE) * MOE_TILE
    ends = jnp.cumsum(padded)
    starts = ends - padded
    slot_of_pair = jnp.sum(onehot * (csum - 1 + starts[None, :]), axis=1)
    n_fill = N_EXPERTS * MOE_TILE
    fill_end = jnp.cumsum(padded - counts)
    fill_key = jnp.sum((jnp.arange(n_fill, dtype=jnp.int32)[:, None] >= fill_end[None, :]).astype(jnp.int32), axis=1)
    n_pairs = t * TOP_K
    pos_bits = (n_pairs + n_fill - 1).bit_length()
    assert (N_EXPERTS + 1) << pos_bits < 2 ** 31
    keys = (jnp.concatenate([flat_e, fill_key]) << pos_bits) + jnp.arange(n_pairs + n_fill, dtype=jnp.int32)
    entry = lax.sort(keys) & ((1 << pos_bits) - 1)
    src_tok = jnp.where(entry < n_pairs, entry // TOP_K, 0)
    n_tiles = (n_pairs + n_fill) // MOE_TILE
    tile_start = jnp.arange(n_tiles, dtype=jnp.int32) * MOE_TILE
    tile_expert = jnp.sum((tile_start[:, None] >= ends[None, :]).astype(jnp.int32), axis=1)
    n_used = (ends[-1] // MOE_TILE).astype(jnp.int32)
    last_e = jnp.sum(jnp.where(jnp.arange(n_tiles) == n_used - 1, tile_expert, 0))
    tile_expert = jnp.where(jnp.arange(n_tiles) < n_used, tile_expert, last_e).astype(jnp.int32)
    return (top_w, (slot_of_pair * SUBLANES).astype(jnp.int32), (src_tok * SUBLANES).astype(jnp.int32),
            tile_expert, n_used.reshape(1))


def _moe_layer(x, g, sc, sh, gate, rw, rb, w1, b1, w2, b2, layer, seq):
    h_tiles, route = _norm_router(x, g, sc, sh, rw, rb, seq)
    top_w, slot8_of_pair, src_row8, tile_expert, n_used = _route(route)
    y_tiles = _expert_ffn(h_tiles, src_row8, tile_expert, n_used, w1, b1, w2, b2, layer)
    return _moe_combine(y_tiles, slot8_of_pair, top_w, x, gate, seq)


def _chunk_pos(shape):
    return lax.broadcasted_iota(jnp.int32, shape, 0) & (CHUNK - 1)


def _inchunk_cumsum(x):
    pos = _chunk_pos(x.shape)
    sh = 1
    while sh < CHUNK:
        x = x + jnp.where(pos >= sh, pltpu.roll(x, sh, axis=0), 0.0)
        sh *= 2
    return x


def _chunk_last_rows(b):
    rows, d = b.shape
    b3 = b.reshape(rows // CHUNK, CHUNK, d)
    return jnp.broadcast_to(b3[:, CHUNK - 1:CHUNK, :], b3.shape).reshape(rows, d)


def _chunk_causal_mask(rows):
    r = lax.broadcasted_iota(jnp.int32, (rows, rows), 0)
    c = lax.broadcasted_iota(jnp.int32, (rows, rows), 1)
    return jnp.logical_and(r >= c, (r // CHUNK) == (c // CHUNK))


def _linattn_block(q, k, v, b, b_rem, chunk_decay, state_ref):
    rows = q.shape[0]
    q_in = (q * jnp.exp(b)).astype(BF16)
    k_in = (k * jnp.exp(-b)).astype(BF16)
    k_st = (k * jnp.exp(b_rem)).astype(BF16)
    vb = v.astype(BF16)
    scores = lax.dot_general(q_in, k_in, (((1,), (1,)), ((), ())), preferred_element_type=F32)
    scores = jnp.where(_chunk_causal_mask(rows), scores, 0.0)
    o_intra = jnp.dot(scores.astype(BF16), vb, preferred_element_type=F32)
    state = state_ref[...]
    o_inter = []
    for c in range(rows // CHUNK):
        sl = slice(c * CHUNK, (c + 1) * CHUNK)
        o_inter.append(jnp.dot(q_in[sl], state.astype(BF16), preferred_element_type=F32))
        upd = lax.dot_general(k_st[sl], vb[sl], (((0,), (0,)), ((), ())), preferred_element_type=F32)
        state = chunk_decay(c) * state + upd
    state_ref[...] = state
    return o_intra + jnp.concatenate(o_inter, axis=0)


HEADS_PER_STEP = 4


def _ret_kernel(q_ref, k_ref, v_ref, g_ref, cos_ref, sin_ref, lg_ref, gn_ref, o_ref, state_ref, *, k_scale):
    @pl.when(pl.program_id(2) == 0)
    def _():
        state_ref[...] = jnp.zeros_like(state_ref)

    rows = q_ref.shape[0]
    dk = q_ref.shape[1] // HEADS_PER_STEP
    dv = v_ref.shape[1] // HEADS_PER_STEP
    half = dk // 2
    cos, sin = cos_ref[...], sin_ref[...]

    def rope(t):
        t1, t2 = t[:, :half], t[:, half:]
        return jnp.concatenate([t1 * cos - t2 * sin, t1 * sin + t2 * cos], axis=1)

    pos = _chunk_pos((rows, dk)).astype(F32)
    for hh in range(HEADS_PER_STEP):
        ks, vs = slice(hh * dk, (hh + 1) * dk), slice(hh * dv, (hh + 1) * dv)
        lg = lg_ref[hh]
        b = (pos + 1.0) * lg
        b_rem = (CHUNK - 1.0 - pos) * lg
        decay = jnp.exp(CHUNK * lg[:, 0:1])
        q = rope(q_ref[:, ks])
        k = rope(k_ref[:, ks]) * k_scale
        o = _linattn_block(q, k, v_ref[:, vs], b, b_rem, lambda c: decay, state_ref.at[hh])
        g = g_ref[:, vs]
        o_ref[:, vs] = _row_norm(o, True) * gn_ref[hh] * (g * jax.nn.sigmoid(g))


def _retention_mixer(proj, gn_g, bsz, seq):
    n_heads, dv = gn_g.shape
    dk = dv // 2
    lb = SEQ_BLOCK
    hps = HEADS_PER_STEP
    n_hp = n_heads // hps
    p3 = proj.reshape(bsz, seq, proj.shape[-1])
    half = dk // 2
    inv = ROPE_BASE ** (-jnp.arange(half, dtype=F32) / half)
    ang = jnp.arange(seq, dtype=F32)[:, None] * inv[None, :]
    cos, sin = jnp.cos(ang), jnp.sin(ang)
    log_gamma = jnp.log1p(-jnp.exp2(-5.0 - jnp.arange(n_heads, dtype=F32)))
    lg = jnp.broadcast_to(log_gamma[:, None, None], (n_heads, 1, dk))
    out = pl.pallas_call(
        functools.partial(_ret_kernel, k_scale=dk ** -0.5),
        grid=(bsz, n_hp, seq // lb),
        in_specs=[pl.BlockSpec((None, lb, hps * dk), lambda b, h, s: (b, s, h)),
                  pl.BlockSpec((None, lb, hps * dk), lambda b, h, s: (b, s, n_hp + h)),
                  pl.BlockSpec((None, lb, hps * dv), lambda b, h, s: (b, s, n_hp + h)),
                  pl.BlockSpec((None, lb, hps * dv), lambda b, h, s: (b, s, 2 * n_hp + h)),
                  pl.BlockSpec((lb, half), lambda b, h, s: (s, 0)),
                  pl.BlockSpec((lb, half), lambda b, h, s: (s, 0)),
                  pl.BlockSpec((hps, 1, dk), lambda b, h, s: (h, 0, 0)),
                  pl.BlockSpec((hps, 1, dv), lambda b, h, s: (h, 0, 0))],
        out_specs=pl.BlockSpec((None, lb, hps * dv), lambda b, h, s: (b, s, h)),
        out_shape=jax.ShapeDtypeStruct((bsz, seq, n_heads * dv), F32),
        scratch_shapes=[pltpu.VMEM((hps, dk, dv), F32)],
        compiler_params=_cparams("arbitrary", "arbitrary", "arbitrary"),
        name="retention",
    )(p3, p3, p3, p3, cos, sin, lg, gn_g.reshape(n_heads, 1, dv))
    return out.reshape(bsz * seq, n_heads * dv)


def _gla_kernel(q_ref, k_ref, v_ref, r_ref, glow_ref, wg_ref, bg_ref, gn_ref, o_ref, state_ref, *, q_scale):
    @pl.when(pl.program_id(2) == 0)
    def _():
        state_ref[...] = jnp.zeros_like(state_ref)

    dk = q_ref.shape[1] // HEADS_PER_STEP
    dv = v_ref.shape[1] // HEADS_PER_STEP

    rows = q_ref.shape[0]
    z = jnp.dot(glow_ref[...], wg_ref[...], precision=HI, preferred_element_type=F32) + bg_ref[...]
    la_all = jax.nn.log_sigmoid(z) / GLA_GATE_NORM
    chunk_sel = (lax.broadcasted_iota(jnp.int32, (rows, LANES), 0) // CHUNK
                 == lax.broadcasted_iota(jnp.int32, (rows, LANES), 1)).astype(F32)
    for hh in range(HEADS_PER_STEP):
        ks, vs = slice(hh * dk, (hh + 1) * dk), slice(hh * dv, (hh + 1) * dv)
        la = la_all[:, ks]
        b = _inchunk_cumsum(la)
        b_rem = _chunk_last_rows(b) - b
        decay = jnp.exp(lax.dot_general(la, chunk_sel, (((0,), (0,)), ((), ())),
                                        precision=HI, preferred_element_type=F32))
        o = _linattn_block(q_ref[:, ks] * q_scale, k_ref[:, ks], v_ref[:, vs], b, b_rem,
                           lambda c: decay[:, c:c + 1], state_ref.at[hh])
        r = r_ref[:, vs]
        o_ref[:, vs] = _row_norm(o, False) * gn_ref[hh] * (r * jax.nn.sigmoid(r))


def _gla_proj_weight(w_in, dq, dvv):
    q_k_v = w_in[:, :2 * dq + dvv]
    g_low = w_in[:, 2 * dq + dvv:2 * dq + dvv + GLA_GATE_RANK]
    r = w_in[:, 2 * dq + dvv + GLA_GATE_RANK:]
    pad = jnp.zeros((w_in.shape[0], LANES - GLA_GATE_RANK), w_in.dtype)
    return jnp.concatenate([q_k_v, r, g_low, pad], axis=1)


def _gla_mixer(proj, w_gate, b_gate, gn_g, bsz, seq):
    n_heads, dv = gn_g.shape
    dk = dv // 2
    dq, dvv = n_heads * dk, n_heads * dv
    lb = SEQ_BLOCK
    hps = HEADS_PER_STEP
    n_hp = n_heads // hps
    p3 = proj.reshape(bsz, seq, proj.shape[-1])
    wg = jnp.pad(w_gate, ((0, LANES - GLA_GATE_RANK), (0, 0)))
    out = pl.pallas_call(
        functools.partial(_gla_kernel, q_scale=dk ** -0.5),
        grid=(bsz, n_hp, seq // lb),
        in_specs=[pl.BlockSpec((None, lb, hps * dk), lambda b, h, s: (b, s, h)),
                  pl.BlockSpec((None, lb, hps * dk), lambda b, h, s: (b, s, n_hp + h)),
                  pl.BlockSpec((None, lb, hps * dv), lambda b, h, s: (b, s, n_hp + h)),
                  pl.BlockSpec((None, lb, hps * dv), lambda b, h, s: (b, s, 2 * n_hp + h)),
                  pl.BlockSpec((None, lb, LANES), lambda b, h, s: (b, s, (2 * dq + 2 * dvv) // LANES)),
                  pl.BlockSpec((LANES, hps * dk), lambda b, h, s: (0, h)),
                  pl.BlockSpec((1, hps * dk), lambda b, h, s: (0, h)),
                  pl.BlockSpec((hps, 1, dv), lambda b, h, s: (h, 0, 0))],
        out_specs=pl.BlockSpec((None, lb, hps * dv), lambda b, h, s: (b, s, h)),
        out_shape=jax.ShapeDtypeStruct((bsz, seq, dvv), F32),
        scratch_shapes=[pltpu.VMEM((hps, dk, dv), F32)],
        compiler_params=_cparams("arbitrary", "arbitrary", "arbitrary"),
        name="gla",
    )(p3, p3, p3, p3, p3, wg, b_gate.reshape(1, dq), gn_g.reshape(n_heads, 1, dv))
    return out.reshape(bsz * seq, dvv)


def _att_kernel(q_ref, k_ref, v_ref, bias_ref, o_ref, kpad, vpad, *, scale, dh):
    seq, width = q_ref.shape
    n_pair = width // dh
    kpad[0:ATT_LEFT, :] = jnp.zeros((ATT_LEFT, width), F32)
    vpad[0:ATT_LEFT, :] = jnp.zeros((ATT_LEFT, width), F32)
    kpad[ATT_LEFT:ATT_LEFT + seq, :] = k_ref[...]
    vpad[ATT_LEFT:ATT_LEFT + seq, :] = v_ref[...]
    lane = lax.broadcasted_iota(jnp.int32, (ATT_QBLOCK, width), 1)
    jcol = lax.broadcasted_iota(jnp.int32, (n_pair * ATT_QBLOCK, ATT_WINDOW), 1)
    in_head = [jnp.logical_and(lane >= h * dh, lane < (h + 1) * dh) for h in range(n_pair)]

    def block(start, left_edge):
        q = q_ref[pl.ds(start, ATT_QBLOCK), :] * scale
        kb = kpad[pl.ds(start, ATT_WINDOW), :].astype(BF16)
        vb = vpad[pl.ds(start, ATT_WINDOW), :].astype(BF16)
        q2 = jnp.concatenate([jnp.where(m, q, 0.0) for m in in_head], axis=0).astype(BF16)
        s = lax.dot_general(q2, kb, (((1,), (1,)), ((), ())), preferred_element_type=F32) + bias_ref[...]
        if left_edge:
            s = jnp.where(jcol >= ATT_LEFT - start, s, -jnp.inf)
        p = jnp.exp(s - jnp.max(s, axis=-1, keepdims=True))
        o2 = jnp.dot(p.astype(BF16), vb, preferred_element_type=F32) / jnp.sum(p, axis=-1, keepdims=True)
        out = o2[0:ATT_QBLOCK, :]
        for h in range(1, n_pair):
            out = jnp.where(in_head[h], o2[h * ATT_QBLOCK:(h + 1) * ATT_QBLOCK, :], out)
        o_ref[pl.ds(start, ATT_QBLOCK), :] = out

    n_edge = ATT_LEFT // ATT_QBLOCK
    for c in range(n_edge):
        block(c * ATT_QBLOCK, True)

    def body(c, carry):
        block(pl.multiple_of(c * ATT_QBLOCK, ATT_QBLOCK), False)
        return carry

    lax.fori_loop(n_edge, seq // ATT_QBLOCK, body, 0, unroll=4)


def _attention_mixer(proj, rel_bias, bsz, seq):
    d = proj.shape[-1] // 3
    n_heads = rel_bias.shape[0]
    dh = d // n_heads
    n_pair = LANES // dh
    groups = n_heads // n_pair
    qo = jnp.arange(ATT_QBLOCK)[:, None]
    kj = jnp.arange(ATT_WINDOW)[None, :]
    lo = (qo // CHUNK) * CHUNK
    in_window = jnp.logical_and(kj >= lo, kj < lo + ATT_LEFT + CHUNK)
    d_max = ATT_QBLOCK - 1 + ATT_LEFT
    d_min = -(ATT_WINDOW - 1 - ATT_LEFT)
    n_rel = rel_bias.shape[1]
    ext = jnp.concatenate(
        [jnp.broadcast_to(rel_bias[:, :1], (n_heads, -(CHUNK - 1) - d_min)), rel_bias,
         jnp.broadcast_to(rel_bias[:, n_rel - 1:], (n_heads, d_max - ATT_MAX_REL))], axis=1)
    rows = [ext[:, q_ + ATT_LEFT - (ATT_WINDOW - 1) - d_min:q_ + ATT_LEFT - d_min + 1] for q_ in range(ATT_QBLOCK)]
    table = jnp.flip(jnp.stack(rows, axis=1), axis=2).astype(F32)
    bias = jnp.where(in_window[None], table, -jnp.inf)
    bias = bias.reshape(groups, n_pair * ATT_QBLOCK, ATT_WINDOW)
    p3 = proj.reshape(bsz, seq, 3 * d)
    out = pl.pallas_call(
        functools.partial(_att_kernel, scale=dh ** -0.5, dh=dh),
        grid=(bsz, groups),
        in_specs=[pl.BlockSpec((None, seq, LANES), lambda b, g: (b, 0, g)),
                  pl.BlockSpec((None, seq, LANES), lambda b, g: (b, 0, groups + g)),
                  pl.BlockSpec((None, seq, LANES), lambda b, g: (b, 0, 2 * groups + g)),
                  pl.BlockSpec((None, n_pair * ATT_QBLOCK, ATT_WINDOW), lambda b, g: (g, 0, 0))],
        out_specs=pl.BlockSpec((None, seq, LANES), lambda b, g: (b, 0, g)),
        out_shape=jax.ShapeDtypeStruct((bsz, seq, d), F32),
        scratch_shapes=[pltpu.VMEM((ATT_LEFT + seq, LANES), F32), pltpu.VMEM((ATT_LEFT + seq, LANES), F32)],
        compiler_params=_cparams("arbitrary", "arbitrary"),
        name="chunk_attention",
    )(p3, p3, p3, bias)
    return out.reshape(bsz * seq, d)


def _mlstm_pre_kernel(xm_ref, cw_ref, cb_ref, wq_ref, wk_ref, wv_ref, wg_ref, bg_ref,
                      q_ref, k_ref, v_ref, xc_ref, g_ref, xpad, *, per, k_scale):
    i = pl.program_id(0)
    tm, inner = xm_ref.shape

    @pl.when(i % per == 0)
    def _():
        xpad[0:CONV_PAD, :] = jnp.zeros((CONV_PAD, inner), F32)

    @pl.when(i % per != 0)
    def _():
        xpad[0:CONV_PAD, :] = xpad[tm:tm + CONV_PAD, :]

    xm = xm_ref[...]
    xpad[CONV_PAD:CONV_PAD + tm, :] = xm
    acc = jnp.broadcast_to(cb_ref[...], (tm, inner))
    for j in range(MLSTM_CONV):
        off = CONV_PAD - (MLSTM_CONV - 1) + j
        acc = acc + cw_ref[j:j + 1, :] * xpad[off:off + tm, :]
    xc = acc * jax.nn.sigmoid(acc)
    xc_ref[...] = xc

    def blockdiag(t, w_ref):
        tb = t.astype(BF16)
        return jnp.concatenate(
            [jnp.dot(tb[:, g * LANES:(g + 1) * LANES], w_ref[g], preferred_element_type=F32)
             for g in range(inner // LANES)], axis=1)

    q = blockdiag(xc, wq_ref)
    k = blockdiag(xc, wk_ref) * k_scale
    v = blockdiag(xm, wv_ref)
    qb, kb, vb = q.astype(BF16), k.astype(BF16), v.astype(BF16)
    q_ref[...] = qb
    k_ref[...] = kb
    v_ref[...] = vb
    g_ref[...] = (jnp.dot(qb, wg_ref[0:inner, :], preferred_element_type=F32)
                  + jnp.dot(kb, wg_ref[inner:2 * inner, :], preferred_element_type=F32)
                  + jnp.dot(vb, wg_ref[2 * inner:3 * inner, :], preferred_element_type=F32) + bg_ref[...])


def _blockdiag_tiles(w):
    nb, c, _ = w.shape
    per = LANES // c
    wt = w.reshape(nb // per, per, c, c)
    t = jnp.einsum('gpcd,pq->gpcqd', wt, jnp.eye(per, dtype=w.dtype))
    return t.reshape(nb // per, LANES, LANES).astype(BF16)


def _mlstm_rec_kernel(q_ref, k_ref, v_ref, g_ref, xc_ref, z_ref, gn_ref, skip_ref, o_ref,
                      c_ref, n_ref, m_ref, *, n_heads):
    head0 = pl.program_id(1) * HEADS_PER_STEP
    dh = q_ref.shape[1] // HEADS_PER_STEP

    @pl.when(pl.program_id(2) == 0)
    def _():
        c_ref[...] = jnp.zeros_like(c_ref)
        n_ref[...] = jnp.zeros_like(n_ref)
        m_ref[...] = jnp.zeros_like(m_ref)

    rows = q_ref.shape[0]
    n_chunks = rows // CHUNK
    mask = _chunk_causal_mask(rows)
    lane = lax.broadcasted_iota(jnp.int32, (rows, LANES), 1)
    sub = lax.broadcasted_iota(jnp.int32, (LANES, rows), 0)

    def col(t, idx):
        return jnp.sum(jnp.where(lane == idx, t, 0.0), axis=1, keepdims=True)

    def row(t_t, idx):
        return jnp.sum(jnp.where(sub == idx, t_t, 0.0), axis=0, keepdims=True)

    def per_chunk(vals):
        return jnp.concatenate([jnp.broadcast_to(x, (CHUNK, 1)) for x in vals], axis=0)

    gates = g_ref[...]
    bcum = _inchunk_cumsum(jax.nn.log_sigmoid(gates))
    gates_t, bcum_t = gates.T, bcum.T
    for hh in range(HEADS_PER_STEP):
        hs = slice(hh * dh, (hh + 1) * dh)
        i_idx, f_idx = head0 + hh, n_heads + head0 + hh
        i_col, b_col = col(gates, i_idx), col(bcum, f_idx)
        b_last_rows = _chunk_last_rows(b_col)
        log_w = b_last_rows - b_col + i_col
        m_start, m_next, keep = [], [], []
        m_c = m_ref[hh, :, 0:1]
        for c in range(n_chunks):
            sl = slice(c * CHUNK, (c + 1) * CHUNK)
            log_keep = b_last_rows[c * CHUNK:c * CHUNK + 1, :] + m_c
            m_n = jnp.maximum(log_keep, jnp.max(log_w[sl], axis=0, keepdims=True))
            m_start.append(m_c)
            m_next.append(m_n)
            keep.append(jnp.exp(log_keep - m_n))
            m_c = m_n
        m_ref[hh] = jnp.broadcast_to(m_c, (1, LANES))
        log_intra = jnp.where(mask, b_col - row(bcum_t, f_idx) + row(gates_t, i_idx), -jnp.inf)
        log_inter = b_col + per_chunk(m_start)
        m_row = jnp.maximum(log_inter, jnp.max(log_intra, axis=1, keepdims=True))
        w_intra = jnp.exp(log_intra - m_row)
        w_inter = jnp.exp(log_inter - m_row)
        qb, kb, vb = q_ref[:, hs], k_ref[:, hs], v_ref[:, hs]
        q, k = qb.astype(F32), kb.astype(F32)
        s = lax.dot_general(qb, kb, (((1,), (1,)), ((), ())), preferred_element_type=F32) * w_intra
        num_intra = jnp.dot(s.astype(BF16), vb, preferred_element_type=F32)
        den_intra = jnp.sum(s, axis=1, keepdims=True)
        wk = jnp.exp(log_w - per_chunk(m_next)) * k
        wkb = wk.astype(BF16)
        cmat, nvec = c_ref[hh], n_ref[hh]
        num_inter, den_inter = [], []
        for c in range(n_chunks):
            sl = slice(c * CHUNK, (c + 1) * CHUNK)
            num_inter.append(jnp.dot(qb[sl], cmat.astype(BF16), preferred_element_type=F32))
            den_inter.append(jnp.sum(q[sl] * nvec, axis=1, keepdims=True))
            upd = lax.dot_general(wkb[sl], vb[sl], (((0,), (0,)), ((), ())), preferred_element_type=F32)
            cmat = keep[c] * cmat + upd
            nvec = keep[c] * nvec + jnp.sum(wk[sl], axis=0, keepdims=True)
        c_ref[hh] = cmat
        n_ref[hh] = nvec
        num = w_inter * jnp.concatenate(num_inter, axis=0) + num_intra
        den = w_inter * jnp.concatenate(den_inter, axis=0) + den_intra
        hc = num / jnp.maximum(jnp.abs(den), jnp.exp(-m_row))
        z = z_ref[:, hs]
        o_ref[:, hs] = ((_row_norm(hc, True) * gn_ref[hh] + skip_ref[:, hs] * xc_ref[:, hs])
                        * (z * jax.nn.sigmoid(z)))


def _mlstm_mixer(proj, conv_w, conv_b, w_q, w_k, w_v, w_gates, b_gates, gn_g, skip, bsz, seq):
    t = proj.shape[0]
    inner = proj.shape[1] // 2
    n_heads, dh = gn_g.shape
    tm = MLSTM_PRE_TILE
    n_g = w_gates.shape[1]
    wg = jnp.pad(w_gates, ((0, 0), (0, LANES - n_g))).astype(BF16)
    bg = jnp.pad(b_gates.reshape(1, n_g), ((0, 0), (0, LANES - n_g)))
    tile_spec = pl.BlockSpec((tm, inner), lambda i: (i, 0))
    bd_spec = pl.BlockSpec((inner // LANES, LANES, LANES), lambda i: (0, 0, 0))
    q, k, v, xc, gates = pl.pallas_call(
        functools.partial(_mlstm_pre_kernel, per=seq // tm, k_scale=dh ** -0.5),
        grid=(t // tm,),
        in_specs=[tile_spec,
                  pl.BlockSpec((MLSTM_CONV, inner), lambda i: (0, 0)),
                  pl.BlockSpec((1, inner), lambda i: (0, 0)),
                  bd_spec, bd_spec, bd_spec,
                  pl.BlockSpec((3 * inner, LANES), lambda i: (0, 0)),
                  pl.BlockSpec((1, LANES), lambda i: (0, 0))],
        out_specs=[tile_spec, tile_spec, tile_spec, tile_spec, pl.BlockSpec((tm, LANES), lambda i: (i, 0))],
        out_shape=[jax.ShapeDtypeStruct((t, inner), BF16)] * 3
        + [jax.ShapeDtypeStruct((t, inner), F32), jax.ShapeDtypeStruct((t, LANES), F32)],
        scratch_shapes=[pltpu.VMEM((tm + CONV_PAD, inner), F32)],
        compiler_params=_cparams("arbitrary"),
        name="mlstm_pre",
    )(proj, conv_w, conv_b.reshape(1, inner), _blockdiag_tiles(w_q), _blockdiag_tiles(w_k),
      _blockdiag_tiles(w_v), wg, bg)

    lb = SEQ_BLOCK
    hps = HEADS_PER_STEP
    n_hp = n_heads // hps
    r3 = lambda a: a.reshape(bsz, seq, a.shape[-1])
    head_spec = pl.BlockSpec((None, lb, hps * dh), lambda b, h, s: (b, s, h))
    out = pl.pallas_call(
        functools.partial(_mlstm_rec_kernel, n_heads=n_heads),
        grid=(bsz, n_hp, seq // lb),
        in_specs=[head_spec, head_spec, head_spec,
                  pl.BlockSpec((None, lb, LANES), lambda b, h, s: (b, s, 0)),
                  head_spec,
                  pl.BlockSpec((None, lb, hps * dh), lambda b, h, s: (b, s, n_hp + h)),
                  pl.BlockSpec((hps, 1, dh), lambda b, h, s: (h, 0, 0)),
                  pl.BlockSpec((1, hps * dh), lambda b, h, s: (0, h))],
        out_specs=head_spec,
        out_shape=jax.ShapeDtypeStruct((bsz, seq, inner), F32),
        scratch_shapes=[pltpu.VMEM((hps, dh, dh), F32), pltpu.VMEM((hps, 1, dh), F32),
                        pltpu.VMEM((hps, 1, LANES), F32)],
        compiler_params=_cparams("arbitrary", "arbitrary", "arbitrary"),
        name="mlstm_rec",
    )(r3(q), r3(k), r3(v), r3(gates), r3(xc), r3(proj), gn_g.reshape(n_heads, 1, dh), skip.reshape(1, inner))
    return out.reshape(t, inner)


def kernel(x, c, ada_w, ada_b, norm_mix_g, norm_ffn_g, norm_final_g, ret_w_in, ret_gn_g, ret_w_out, att_w_in, att_rel_bias, att_w_out, gla_w_in, gla_w_gate, gla_b_gate, gla_gn_g, gla_w_out, mlstm_w_in, mlstm_conv_w, mlstm_conv_b, mlstm_w_q, mlstm_w_k, mlstm_w_v, mlstm_w_gates, mlstm_b_gates, mlstm_gn_g, mlstm_skip, mlstm_w_out, router_w, router_b, moe_w1, moe_b1, moe_w2, moe_b2):
    bsz, seq, d = x.shape
    depth, n_exp, _, dh2 = moe_w1.shape
    xt = x.reshape(bsz * seq, d)
    mod = _ada_mod(c, ada_w, ada_b)
    w1_all = moe_w1.reshape(depth * n_exp, d, dh2)
    w2_all = moe_w2.reshape(depth * n_exp, dh2 // 2, d)
    b1_all = moe_b1.reshape(depth * n_exp, 1, dh2)
    b2_all = moe_b2.reshape(depth * n_exp, 1, d)
    for i in range(depth):
        kind, j = i % 4, i // 4
        sh_a, sc_a, g_a, sh_f, sc_f, g_f = jnp.split(mod[i], 6, axis=-1)
        if kind == 0:
            proj = _norm_mm(xt, norm_mix_g[i], sc_a, sh_a, ret_w_in[j].astype(BF16), seq)
            y = _retention_mixer(proj, ret_gn_g[j], bsz, seq)
            w_out = ret_w_out[j]
        elif kind == 1:
            proj = _norm_mm(xt, norm_mix_g[i], sc_a, sh_a, att_w_in[j].astype(BF16), seq)
            y = _attention_mixer(proj, att_rel_bias[j], bsz, seq)
            w_out = att_w_out[j]
        elif kind == 2:
            n_heads, dv = gla_gn_g[j].shape
            w_gla = _gla_proj_weight(gla_w_in[j], n_heads * dv // 2, n_heads * dv).astype(BF16)
            proj = _norm_mm(xt, norm_mix_g[i], sc_a, sh_a, w_gla, seq, tn=w_gla.shape[1] // 5)
            y = _gla_mixer(proj, gla_w_gate[j], gla_b_gate[j], gla_gn_g[j], bsz, seq)
            w_out = gla_w_out[j]
        else:
            proj = _norm_mm(xt, norm_mix_g[i], sc_a, sh_a, mlstm_w_in[j].astype(BF16), seq)
            y = _mlstm_mixer(proj, mlstm_conv_w[j], mlstm_conv_b[j], mlstm_w_q[j], mlstm_w_k[j], mlstm_w_v[j],
                             mlstm_w_gates[j], mlstm_b_gates[j], mlstm_gn_g[j], mlstm_skip[j], bsz, seq)
            w_out = mlstm_w_out[j]
        xt = _mm_res(y, w_out.astype(BF16), xt, g_a, seq)
        xt = _moe_layer(xt, norm_ffn_g[i], sc_f, sh_f, g_f, router_w[i], router_b[i],
                        w1_all, b1_all, w2_all, b2_all, i, seq)
    return _final_norm(xt, norm_final_g).reshape(bsz, seq, d)
```

```python
import functools

import jax
import jax.numpy as jnp
from jax import lax
from jax.experimental import pallas as pl
from jax.experimental.pallas import tpu as pltpu

F32 = jnp.float32
BF16 = jnp.bfloat16
HI = lax.Precision.HIGHEST

CHUNK = 64
EPS = 1e-6
ROPE_BASE = 10000.0
ATT_LEFT_CHUNKS = 8
ATT_MAX_REL = 256
GLA_GATE_RANK = 16
GLA_GATE_NORM = 16.0
MLSTM_CONV = 4
N_EXPERTS = 32
TOP_K = 4
SWIGLU_LIMIT = 7.0
SWIGLU_ALPHA = 1.702

VMEM_LIMIT_BYTES = 56 * 1024 * 1024
LANES = 128
SUBLANES = 8
MOE_TILE = 512
COMBINE_TILE = 512
COPIES_PER_ITER = 32
SEQ_BLOCK = 256
MLSTM_PRE_TILE = 256
CONV_PAD = 8
ATT_QBLOCK = 2 * CHUNK
ATT_LEFT = ATT_LEFT_CHUNKS * CHUNK
ATT_WINDOW = ATT_LEFT + ATT_QBLOCK


def _cparams(*sem):
    return pltpu.CompilerParams(dimension_semantics=sem, vmem_limit_bytes=VMEM_LIMIT_BYTES)


def _dot_bf16(a, b, dims=(((1,), (0,)), ((), ()))):
    return lax.dot_general(a.astype(BF16), b.astype(BF16), dims, preferred_element_type=F32)


def _tril_ones(n):
    row = lax.broadcasted_iota(jnp.int32, (n, n), 0)
    col = lax.broadcasted_iota(jnp.int32, (n, n), 1)
    return row >= col


def _row_norm(x, center):
    if center:
        x = x - jnp.mean(x, axis=-1, keepdims=True)
    return x * lax.rsqrt(jnp.mean(x * x, axis=-1, keepdims=True) + EPS)


def _load_row_tiles(ref, rows):
    return jnp.concatenate([ref[pl.ds(j, rows, stride=SUBLANES), :] for j in range(SUBLANES)], axis=1)


def _store_row_tiles(ref, val):
    rows = val.shape[0]
    for j in range(SUBLANES):
        ref[pl.ds(j, rows, stride=SUBLANES), :] = val[:, j * LANES:(j + 1) * LANES]


def _ada_kernel(c_ref, w_ref, b_ref, o_ref):
    c = c_ref[...]
    cond = c * jax.nn.sigmoid(c)
    o_ref[0] = jnp.dot(cond, w_ref[0], preferred_element_type=F32, precision=HI) + b_ref[0]


def _ada_mod(c, ada_w, ada_b):
    depth, d, n = ada_w.shape
    b = c.shape[0]
    tn = 1536
    return pl.pallas_call(
        _ada_kernel,
        grid=(depth, n // tn),
        in_specs=[pl.BlockSpec((b, d), lambda l, j: (0, 0)),
                  pl.BlockSpec((1, d, tn), lambda l, j: (l, 0, j)),
                  pl.BlockSpec((1, 1, tn), lambda l, j: (l, 0, j))],
        out_specs=pl.BlockSpec((1, b, tn), lambda l, j: (l, 0, j)),
        out_shape=jax.ShapeDtypeStruct((depth, b, n), F32),
        compiler_params=_cparams("arbitrary", "arbitrary"),
        name="ada_mod",
    )(c, ada_w, ada_b.reshape(depth, 1, n))


def _modulated_norm(x, g, sc, sh):
    ms = jnp.mean(x * x, axis=-1, keepdims=True)
    return (x * lax.rsqrt(ms + EPS) * g) * (1.0 + sc) + sh


def _norm_mm_kernel(x_ref, g_ref, sc_ref, sh_ref, w_ref, o_ref, h_scr):
    @pl.when(pl.program_id(1) == 0)
    def _():
        h = _modulated_norm(x_ref[...], g_ref[...], sc_ref[0], sh_ref[0])
        h_scr[...] = h.astype(BF16)

    o_ref[...] = jnp.dot(h_scr[...], w_ref[...], preferred_element_type=F32)


def _norm_mm(x, g, sc, sh, w, seq, tm=1024, tn=1024):
    t, d = x.shape
    n = w.shape[1]
    per = seq // tm
    bsz = sc.shape[0]
    return pl.pallas_call(
        _norm_mm_kernel,
        grid=(t // tm, n // tn),
        in_specs=[pl.BlockSpec((tm, d), lambda i, j: (i, 0)),
                  pl.BlockSpec((1, d), lambda i, j: (0, 0)),
                  pl.BlockSpec((1, 1, d), lambda i, j: (i // per, 0, 0)),
                  pl.BlockSpec((1, 1, d), lambda i, j: (i // per, 0, 0)),
                  pl.BlockSpec((d, tn), lambda i, j: (0, j))],
        out_specs=pl.BlockSpec((tm, tn), lambda i, j: (i, j)),
        out_shape=jax.ShapeDtypeStruct((t, n), F32),
        scratch_shapes=[pltpu.VMEM((tm, d), BF16)],
        compiler_params=_cparams("arbitrary", "arbitrary"),
        name="norm_mm",
    )(x, g.reshape(1, d), sc.reshape(bsz, 1, d), sh.reshape(bsz, 1, d), w)


def _mm_res_kernel(y_ref, w_ref, x_ref, gate_ref, o_ref):
    acc = jnp.dot(y_ref[...].astype(BF16), w_ref[...], preferred_element_type=F32)
    o_ref[...] = x_ref[...] + gate_ref[0] * acc


def _mm_res(y, w, x, gate, seq, tm=512):
    t, k = y.shape
    d = w.shape[1]
    per = seq // tm
    bsz = gate.shape[0]
    return pl.pallas_call(
        _mm_res_kernel,
        grid=(t // tm,),
        in_specs=[pl.BlockSpec((tm, k), lambda i: (i, 0)),
                  pl.BlockSpec((k, d), lambda i: (0, 0)),
                  pl.BlockSpec((tm, d), lambda i: (i, 0)),
                  pl.BlockSpec((1, 1, d), lambda i: (i // per, 0, 0))],
        out_specs=pl.BlockSpec((tm, d), lambda i: (i, 0)),
        out_shape=jax.ShapeDtypeStruct((t, d), F32),
        compiler_params=_cparams("arbitrary"),
        name="mm_res",
    )(y, w, x, gate.reshape(bsz, 1, d))


def _final_norm_kernel(x_ref, g_ref, o_ref):
    x = x_ref[...]
    ms = jnp.mean(x * x, axis=-1, keepdims=True)
    o_ref[...] = x * lax.rsqrt(ms + EPS) * g_ref[...]


def _final_norm(x, g, tm=1024):
    t, d = x.shape
    return pl.pallas_call(
        _final_norm_kernel,
        grid=(t // tm,),
        in_specs=[pl.BlockSpec((tm, d), lambda i: (i, 0)),
                  pl.BlockSpec((1, d), lambda i: (0, 0))],
        out_specs=pl.BlockSpec((tm, d), lambda i: (i, 0)),
        out_shape=jax.ShapeDtypeStruct((t, d), F32),
        compiler_params=_cparams("arbitrary"),
        name="final_norm",
    )(x, g.reshape(1, d))


def _norm_router_kernel(x_ref, g_ref, sc_ref, sh_ref, rw_ref, rb_ref, h_ref, route_ref, *, n_experts):
    h = _modulated_norm(x_ref[...], g_ref[...], sc_ref[0], sh_ref[0])
    _store_row_tiles(h_ref, h)
    logits = jnp.dot(h, rw_ref[...], preferred_element_type=F32, precision=HI) + rb_ref[...]
    lane = lax.broadcasted_iota(jnp.int32, logits.shape, 1)
    rem = jnp.where(lane < n_experts, logits, -jnp.inf)
    vals, idxs = [], []
    for _ in range(TOP_K):
        m = jnp.max(rem, axis=1, keepdims=True)
        idx = jnp.min(jnp.where(rem == m, lane, LANES), axis=1, keepdims=True)
        vals.append(m)
        idxs.append(idx)
        rem = jnp.where(lane == idx, -jnp.inf, rem)
    ex = [jnp.exp(v - vals[0]) for v in vals]
    inv = 1.0 / sum(ex[1:], ex[0])
    out = jnp.zeros(logits.shape, F32)
    for k in range(TOP_K):
        out = jnp.where(lane == k, ex[k] * inv, out)
        out = jnp.where(lane == TOP_K + k, idxs[k].astype(F32), out)
    route_ref[...] = out


def _norm_router(x, g, sc, sh, rw, rb, seq, tm=512):
    t, d = x.shape
    per = seq // tm
    bsz = sc.shape[0]
    e = rw.shape[1]
    rw_p = jnp.pad(rw, ((0, 0), (0, LANES - e)))
    rb_p = jnp.pad(rb.reshape(1, e), ((0, 0), (0, LANES - e)))
    return pl.pallas_call(
        functools.partial(_norm_router_kernel, n_experts=e),
        grid=(t // tm,),
        in_specs=[pl.BlockSpec((tm, d), lambda i: (i, 0)),
                  pl.BlockSpec((1, d), lambda i: (0, 0)),
                  pl.BlockSpec((1, 1, d), lambda i: (i // per, 0, 0)),
                  pl.BlockSpec((1, 1, d), lambda i: (i // per, 0, 0)),
                  pl.BlockSpec((d, LANES), lambda i: (0, 0)),
                  pl.BlockSpec((1, LANES), lambda i: (0, 0))],
        out_specs=[pl.BlockSpec((tm * SUBLANES, LANES), lambda i: (i, 0)),
                   pl.BlockSpec((tm, LANES), lambda i: (i, 0))],
        out_shape=[jax.ShapeDtypeStruct((t * SUBLANES, LANES), F32),
                   jax.ShapeDtypeStruct((t, LANES), F32)],
        compiler_params=_cparams("arbitrary"),
        name="norm_router",
    )(x, g.reshape(1, d), sc.reshape(bsz, 1, d), sh.reshape(bsz, 1, d), rw_p, rb_p)


def _row_tile_copy(src_hbm, src_row8, dst_vmem, dst_row, sem):
    return pltpu.make_async_copy(
        src_hbm.at[pl.ds(pl.multiple_of(src_row8, SUBLANES), SUBLANES)],
        dst_vmem.at[pl.ds(pl.multiple_of(dst_row * SUBLANES, SUBLANES), SUBLANES)], sem)


def _expert_kernel(te_ref, nt_ref, idx_cur_ref, idx_nxt_ref, h_hbm, w1_ref, b1_ref, w2_ref, b2_ref,
                   o_ref, xbuf, w1_bf, w2_bf, sem):
    i = pl.program_id(0)
    n_used = nt_ref[0]
    tm = xbuf.shape[1] // SUBLANES
    slot = i % 2

    @pl.when(jnp.logical_or(i == 0, te_ref[i] != te_ref[jnp.maximum(i - 1, 0)]))
    def _():
        w1_bf[...] = w1_ref[0].astype(BF16)
        w2_bf[...] = w2_ref[0].astype(BF16)

    def issue(idx_ref, s):
        def body(g, carry):
            for u in range(COPIES_PER_ITER):
                r = g * COPIES_PER_ITER + u
                _row_tile_copy(h_hbm, idx_ref[0, 0, r], xbuf.at[s], r, sem.at[s]).start(priority=u % 2)
            return carry
        lax.fori_loop(0, tm // COPIES_PER_ITER, body, 0)

    @pl.when(jnp.logical_and(i == 0, n_used > 0))
    def _():
        issue(idx_cur_ref, 0)

    @pl.when(i + 1 < n_used)
    def _():
        issue(idx_nxt_ref, 1 - slot)

    @pl.when(i < n_used)
    def _():
        pltpu.make_async_copy(h_hbm.at[pl.ds(0, tm * SUBLANES)], xbuf.at[slot], sem.at[slot]).wait()
        x = _load_row_tiles(xbuf.at[slot], tm).astype(BF16)
        hid = jnp.dot(x, w1_bf[...], preferred_element_type=F32) + b1_ref[0]
        de = hid.shape[1] // 2
        gate = jnp.minimum(hid[:, :de], SWIGLU_LIMIT)
        lin = jnp.clip(hid[:, de:], -SWIGLU_LIMIT, SWIGLU_LIMIT)
        act = gate * jax.nn.sigmoid(SWIGLU_ALPHA * gate) * (lin + 1.0)
        y = jnp.dot(act.astype(BF16), w2_bf[...], preferred_element_type=F32) + b2_ref[0]
        _store_row_tiles(o_ref, y)

    @pl.when(i >= n_used)
    def _():
        o_ref[...] = jnp.zeros_like(o_ref)


def _expert_ffn(h_tiles, src_row8, tile_expert, n_tiles_used, w1, b1, w2, b2, layer):
    p = src_row8.shape[0]
    tm = MOE_TILE
    nt = p // tm
    _, d, dh = w1.shape
    idx3 = src_row8.reshape(nt, 1, tm)
    base = layer * N_EXPERTS
    grid_spec = pltpu.PrefetchScalarGridSpec(
        num_scalar_prefetch=2,
        grid=(nt,),
        in_specs=[
            pl.BlockSpec((1, 1, tm), lambda i, te, nu: (i, 0, 0), memory_space=pltpu.SMEM),
            pl.BlockSpec((1, 1, tm), lambda i, te, nu: (jnp.minimum(i + 1, nt - 1), 0, 0),
                         memory_space=pltpu.SMEM),
            pl.BlockSpec(memory_space=pl.ANY),
            pl.BlockSpec((1, d, dh), lambda i, te, nu: (base + te[i], 0, 0)),
            pl.BlockSpec((1, 1, dh), lambda i, te, nu: (base + te[i], 0, 0)),
            pl.BlockSpec((1, dh // 2, d), lambda i, te, nu: (base + te[i], 0, 0)),
            pl.BlockSpec((1, 1, d), lambda i, te, nu: (base + te[i], 0, 0)),
        ],
        out_specs=pl.BlockSpec((tm * SUBLANES, LANES), lambda i, te, nu: (i, 0)),
        scratch_shapes=[pltpu.VMEM((2, tm * SUBLANES, LANES), F32), pltpu.VMEM((d, dh), BF16),
                        pltpu.VMEM((dh // 2, d), BF16), pltpu.SemaphoreType.DMA((2,))],
    )
    return pl.pallas_call(
        _expert_kernel,
        grid_spec=grid_spec,
        out_shape=jax.ShapeDtypeStruct((p * SUBLANES, LANES), F32),
        compiler_params=_cparams("arbitrary"),
        name="expert_ffn",
    )(tile_expert, n_tiles_used, idx3, idx3, h_tiles, w1, b1, w2, b2)


def _combine_kernel(idx_cur_ref, idx_nxt_ref, y_hbm, x_ref, w_ref, gate_ref, o_ref, ybuf, sem):
    i = pl.program_id(0)
    n = pl.num_programs(0)
    tc = x_ref.shape[0]
    slot = i % 2

    def issue(idx_ref, s):
        rows_per_iter = COPIES_PER_ITER // TOP_K

        def body(g, carry):
            for u in range(rows_per_iter):
                r = g * rows_per_iter + u
                for k in range(TOP_K):
                    _row_tile_copy(y_hbm, idx_ref[0, 0, r * TOP_K + k], ybuf.at[s, k], r,
                                   sem.at[s]).start(priority=k % 2)
            return carry
        lax.fori_loop(0, tc // rows_per_iter, body, 0)

    @pl.when(i == 0)
    def _():
        issue(idx_cur_ref, 0)

    @pl.when(i + 1 < n)
    def _():
        issue(idx_nxt_ref, 1 - slot)

    for k in range(TOP_K):
        pltpu.make_async_copy(y_hbm.at[pl.ds(0, tc * SUBLANES)], ybuf.at[slot, k], sem.at[slot]).wait()
    w = w_ref[...]
    acc = w[:, 0:1] * _load_row_tiles(ybuf.at[slot, 0], tc)
    for k in range(1, TOP_K):
        acc = acc + w[:, k:k + 1] * _load_row_tiles(ybuf.at[slot, k], tc)
    o_ref[...] = x_ref[...] + gate_ref[0] * acc


def _moe_combine(y_tiles, slot8_of_pair, top_w, x, gate, seq):
    t, d = x.shape
    tc = COMBINE_TILE
    nt = t // tc
    per = seq // tc
    bsz = gate.shape[0]
    idx3 = slot8_of_pair.reshape(nt, 1, tc * TOP_K)
    return pl.pallas_call(
        _combine_kernel,
        grid=(nt,),
        in_specs=[
            pl.BlockSpec((1, 1, tc * TOP_K), lambda i: (i, 0, 0), memory_space=pltpu.SMEM),
            pl.BlockSpec((1, 1, tc * TOP_K), lambda i: (jnp.minimum(i + 1, nt - 1), 0, 0),
                         memory_space=pltpu.SMEM),
            pl.BlockSpec(memory_space=pl.ANY),
            pl.BlockSpec((tc, d), lambda i: (i, 0)),
            pl.BlockSpec((tc, TOP_K), lambda i: (i, 0)),
            pl.BlockSpec((1, 1, d), lambda i: (i // per, 0, 0)),
        ],
        out_specs=pl.BlockSpec((tc, d), lambda i: (i, 0)),
        out_shape=jax.ShapeDtypeStruct((t, d), F32),
        scratch_shapes=[pltpu.VMEM((2, TOP_K, tc * SUBLANES, LANES), F32), pltpu.SemaphoreType.DMA((2,))],
        compiler_params=_cparams("arbitrary"),
        name="moe_combine",
    )(idx3, idx3, y_tiles, x, top_w, gate.reshape(bsz, 1, d))


def _route(route):
    t = route.shape[0]
    top_w = route[:, :TOP_K]
    flat_e = route[:, TOP_K:2 * TOP_K].astype(jnp.int32).reshape(-1)
    onehot = (flat_e[:, None] == jnp.arange(N_EXPERTS, dtype=jnp.int32)[None, :]).astype(jnp.int32)
    csum = jnp.cumsum(onehot, axis=0)
    counts = csum[-1]
    padded = ((counts + MOE_TILE - 1) // MOE_TILE) * MOE_TILE
    ends = jnp.cumsum(padded)
    starts = ends - padded
    slot_of_pair = jnp.sum(onehot * (csum - 1 + starts[None, :]), axis=1)
    n_fill = N_EXPERTS * MOE_TILE
    fill_end = jnp.cumsum(padded - counts)
    fill_key = jnp.sum((jnp.arange(n_fill, dtype=jnp.int32)[:, None] >= fill_end[None, :]).astype(jnp.int32), axis=1)
    n_pairs = t * TOP_K
    pos_bits = (n_pairs + n_fill - 1).bit_length()
    assert (N_EXPERTS + 1) << pos_bits < 2 ** 31
    keys = (jnp.concatenate([flat_e, fill_key]) << pos_bits) + jnp.arange(n_pairs + n_fill, dtype=jnp.int32)
    entry = lax.sort(keys) & ((1 << pos_bits) - 1)
    src_tok = jnp.where(entry < n_pairs, entry // TOP_K, 0)
    n_tiles = (n_pairs + n_fill) // MOE_TILE
    tile_start = jnp.arange(n_tiles, dtype=jnp.int32) * MOE_TILE
    tile_expert = jnp.sum((tile_start[:, None] >= ends[None, :]).astype(jnp.int32), axis=1)
    n_used = (ends[-1] // MOE_TILE).astype(jnp.int32)
    last_e = jnp.sum(jnp.where(jnp.arange(n_tiles) == n_used - 1, tile_expert, 0))
    tile_expert = jnp.where(jnp.arange(n_tiles) < n_used, tile_expert, last_e).astype(jnp.int32)
    return (top_w, (slot_of_pair * SUBLANES).astype(jnp.int32), (src_tok * SUBLANES).astype(jnp.int32),
            tile_expert, n_used.reshape(1))


def _moe_layer(x, g, sc, sh, gate, rw, rb, w1, b1, w2, b2, layer, seq):
    h_tiles, route = _norm_router(x, g, sc, sh, rw, rb, seq)
    top_w, slot8_of_pair, src_row8, tile_expert, n_used = _route(route)
    y_tiles = _expert_ffn(h_tiles, src_row8, tile_expert, n_used, w1, b1, w2, b2, layer)
    return _moe_combine(y_tiles, slot8_of_pair, top_w, x, gate, seq)


def _chunk_pos(shape):
    return lax.broadcasted_iota(jnp.int32, shape, 0) & (CHUNK - 1)


def _inchunk_cumsum(x):
    pos = _chunk_pos(x.shape)
    sh = 1
    while sh < CHUNK:
        x = x + jnp.where(pos >= sh, pltpu.roll(x, sh, axis=0), 0.0)
        sh *= 2
    return x


def _chunk_last_rows(b):
    rows, d = b.shape
    b3 = b.reshape(rows // CHUNK, CHUNK, d)
    return jnp.broadcast_to(b3[:, CHUNK - 1:CHUNK, :], b3.shape).reshape(rows, d)


def _chunk_causal_mask(rows):
    r = lax.broadcasted_iota(jnp.int32, (rows, rows), 0)
    c = lax.broadcasted_iota(jnp.int32, (rows, rows), 1)
    return jnp.logical_and(r >= c, (r // CHUNK) == (c // CHUNK))


def _linattn_block(q, k, v, b, b_rem, chunk_decay, state_ref):
    rows = q.shape[0]
    q_in = (q * jnp.exp(b)).astype(BF16)
    k_in = (k * jnp.exp(-b)).astype(BF16)
    k_st = (k * jnp.exp(b_rem)).astype(BF16)
    vb = v.astype(BF16)
    scores = lax.dot_general(q_in, k_in, (((1,), (1,)), ((), ())), preferred_element_type=F32)
    scores = jnp.where(_chunk_causal_mask(rows), scores, 0.0)
    o_intra = jnp.dot(scores.astype(BF16), vb, preferred_element_type=F32)
    state = state_ref[...]
    o_inter = []
    for c in range(rows // CHUNK):
        sl = slice(c * CHUNK, (c + 1) * CHUNK)
        o_inter.append(jnp.dot(q_in[sl], state.astype(BF16), preferred_element_type=F32))
        upd = lax.dot_general(k_st[sl], vb[sl], (((0,), (0,)), ((), ())), preferred_element_type=F32)
        state = chunk_decay(c) * state + upd
    state_ref[...] = state
    return o_intra + jnp.concatenate(o_inter, axis=0)


HEADS_PER_STEP = 4


def _ret_kernel(q_ref, k_ref, v_ref, g_ref, cos_ref, sin_ref, lg_ref, gn_ref, o_ref, state_ref, *, k_scale):
    @pl.when(pl.program_id(2) == 0)
    def _():
        state_ref[...] = jnp.zeros_like(state_ref)

    rows = q_ref.shape[0]
    dk = q_ref.shape[1] // HEADS_PER_STEP
    dv = v_ref.shape[1] // HEADS_PER_STEP
    half = dk // 2
    cos, sin = cos_ref[...], sin_ref[...]

    def rope(t):
        t1, t2 = t[:, :half], t[:, half:]
        return jnp.concatenate([t1 * cos - t2 * sin, t1 * sin + t2 * cos], axis=1)

    pos = _chunk_pos((rows, dk)).astype(F32)
    for hh in range(HEADS_PER_STEP):
        ks, vs = slice(hh * dk, (hh + 1) * dk), slice(hh * dv, (hh + 1) * dv)
        lg = lg_ref[hh]
        b = (pos + 1.0) * lg
        b_rem = (CHUNK - 1.0 - pos) * lg
        decay = jnp.exp(CHUNK * lg[:, 0:1])
        q = rope(q_ref[:, ks])
        k = rope(k_ref[:, ks]) * k_scale
        o = _linattn_block(q, k, v_ref[:, vs], b, b_rem, lambda c: decay, state_ref.at[hh])
        g = g_ref[:, vs]
        o_ref[:, vs] = _row_norm(o, True) * gn_ref[hh] * (g * jax.nn.sigmoid(g))


def _retention_mixer(proj, gn_g, bsz, seq):
    n_heads, dv = gn_g.shape
    dk = dv // 2
    lb = SEQ_BLOCK
    hps = HEADS_PER_STEP
    n_hp = n_heads // hps
    p3 = proj.reshape(bsz, seq, proj.shape[-1])
    half = dk // 2
    inv = ROPE_BASE ** (-jnp.arange(half, dtype=F32) / half)
    ang = jnp.arange(seq, dtype=F32)[:, None] * inv[None, :]
    cos, sin = jnp.cos(ang), jnp.sin(ang)
    log_gamma = jnp.log1p(-jnp.exp2(-5.0 - jnp.arange(n_heads, dtype=F32)))
    lg = jnp.broadcast_to(log_gamma[:, None, None], (n_heads, 1, dk))
    out = pl.pallas_call(
        functools.partial(_ret_kernel, k_scale=dk ** -0.5),
        grid=(bsz, n_hp, seq // lb),
        in_specs=[pl.BlockSpec((None, lb, hps * dk), lambda b, h, s: (b, s, h)),
                  pl.BlockSpec((None, lb, hps * dk), lambda b, h, s: (b, s, n_hp + h)),
                  pl.BlockSpec((None, lb, hps * dv), lambda b, h, s: (b, s, n_hp + h)),
                  pl.BlockSpec((None, lb, hps * dv), lambda b, h, s: (b, s, 2 * n_hp + h)),
                  pl.BlockSpec((lb, half), lambda b, h, s: (s, 0)),
                  pl.BlockSpec((lb, half), lambda b, h, s: (s, 0)),
                  pl.BlockSpec((hps, 1, dk), lambda b, h, s: (h, 0, 0)),
                  pl.BlockSpec((hps, 1, dv), lambda b, h, s: (h, 0, 0))],
        out_specs=pl.BlockSpec((None, lb, hps * dv), lambda b, h, s: (b, s, h)),
        out_shape=jax.ShapeDtypeStruct((bsz, seq, n_heads * dv), F32),
        scratch_shapes=[pltpu.VMEM((hps, dk, dv), F32)],
        compiler_params=_cparams("arbitrary", "arbitrary", "arbitrary"),
        name="retention",
    )(p3, p3, p3, p3, cos, sin, lg, gn_g.reshape(n_heads, 1, dv))
    return out.reshape(bsz * seq, n_heads * dv)


def _gla_kernel(q_ref, k_ref, v_ref, r_ref, glow_ref, wg_ref, bg_ref, gn_ref, o_ref, state_ref, *, q_scale):
    @pl.when(pl.program_id(2) == 0)
    def _():
        state_ref[...] = jnp.zeros_like(state_ref)

    dk = q_ref.shape[1] // HEADS_PER_STEP
    dv = v_ref.shape[1] // HEADS_PER_STEP

    rows = q_ref.shape[0]
    z = jnp.dot(glow_ref[...], wg_ref[...], precision=HI, preferred_element_type=F32) + bg_ref[...]
    la_all = jax.nn.log_sigmoid(z) / GLA_GATE_NORM
    chunk_sel = (lax.broadcasted_iota(jnp.int32, (rows, LANES), 0) // CHUNK
                 == lax.broadcasted_iota(jnp.int32, (rows, LANES), 1)).astype(F32)
    for hh in range(HEADS_PER_STEP):
        ks, vs = slice(hh * dk, (hh + 1) * dk), slice(hh * dv, (hh + 1) * dv)
        la = la_all[:, ks]
        b = _inchunk_cumsum(la)
        b_rem = _chunk_last_rows(b) - b
        decay = jnp.exp(lax.dot_general(la, chunk_sel, (((0,), (0,)), ((), ())),
                                        precision=HI, preferred_element_type=F32))
        o = _linattn_block(q_ref[:, ks] * q_scale, k_ref[:, ks], v_ref[:, vs], b, b_rem,
                           lambda c: decay[:, c:c + 1], state_ref.at[hh])
        r = r_ref[:, vs]
        o_ref[:, vs] = _row_norm(o, False) * gn_ref[hh] * (r * jax.nn.sigmoid(r))


def _gla_proj_weight(w_in, dq, dvv):
    q_k_v = w_in[:, :2 * dq + dvv]
    g_low = w_in[:, 2 * dq + dvv:2 * dq + dvv + GLA_GATE_RANK]
    r = w_in[:, 2 * dq + dvv + GLA_GATE_RANK:]
    pad = jnp.zeros((w_in.shape[0], LANES - GLA_GATE_RANK), w_in.dtype)
    return jnp.concatenate([q_k_v, r, g_low, pad], axis=1)


def _gla_mixer(proj, w_gate, b_gate, gn_g, bsz, seq):
    n_heads, dv = gn_g.shape
    dk = dv // 2
    dq, dvv = n_heads * dk, n_heads * dv
    lb = SEQ_BLOCK
    hps = HEADS_PER_STEP
    n_hp = n_heads // hps
    p3 = proj.reshape(bsz, seq, proj.shape[-1])
    wg = jnp.pad(w_gate, ((0, LANES - GLA_GATE_RANK), (0, 0)))
    out = pl.pallas_call(
        functools.partial(_gla_kernel, q_scale=dk ** -0.5),
        grid=(bsz, n_hp, seq // lb),
        in_specs=[pl.BlockSpec((None, lb, hps * dk), lambda b, h, s: (b, s, h)),
                  pl.BlockSpec((None, lb, hps * dk), lambda b, h, s: (b, s, n_hp + h)),
                  pl.BlockSpec((None, lb, hps * dv), lambda b, h, s: (b, s, n_hp + h)),
                  pl.BlockSpec((None, lb, hps * dv), lambda b, h, s: (b, s, 2 * n_hp + h)),
                  pl.BlockSpec((None, lb, LANES), lambda b, h, s: (b, s, (2 * dq + 2 * dvv) // LANES)),
                  pl.BlockSpec((LANES, hps * dk), lambda b, h, s: (0, h)),
                  pl.BlockSpec((1, hps * dk), lambda b, h, s: (0, h)),
                  pl.BlockSpec((hps, 1, dv), lambda b, h, s: (h, 0, 0))],
        out_specs=pl.BlockSpec((None, lb, hps * dv), lambda b, h, s: (b, s, h)),
        out_shape=jax.ShapeDtypeStruct((bsz, seq, dvv), F32),
        scratch_shapes=[pltpu.VMEM((hps, dk, dv), F32)],
        compiler_params=_cparams("arbitrary", "arbitrary", "arbitrary"),
        name="gla",
    )(p3, p3, p3, p3, p3, wg, b_gate.reshape(1, dq), gn_g.reshape(n_heads, 1, dv))
    return out.reshape(bsz * seq, dvv)


def _att_kernel(q_ref, k_ref, v_ref, bias_ref, o_ref, kpad, vpad, *, scale, dh):
    seq, width = q_ref.shape
    n_pair = width // dh
    kpad[0:ATT_LEFT, :] = jnp.zeros((ATT_LEFT, width), F32)
    vpad[0:ATT_LEFT, :] = jnp.zeros((ATT_LEFT, width), F32)
    kpad[ATT_LEFT:ATT_LEFT + seq, :] = k_ref[...]
    vpad[ATT_LEFT:ATT_LEFT + seq, :] = v_ref[...]
    lane = lax.broadcasted_iota(jnp.int32, (ATT_QBLOCK, width), 1)
    jcol = lax.broadcasted_iota(jnp.int32, (n_pair * ATT_QBLOCK, ATT_WINDOW), 1)
    in_head = [jnp.logical_and(lane >= h * dh, lane < (h + 1) * dh) for h in range(n_pair)]

    def block(start, left_edge):
        q = q_ref[pl.ds(start, ATT_QBLOCK), :] * scale
        kb = kpad[pl.ds(start, ATT_WINDOW), :].astype(BF16)
        vb = vpad[pl.ds(start, ATT_WINDOW), :].astype(BF16)
        q2 = jnp.concatenate([jnp.where(m, q, 0.0) for m in in_head], axis=0).astype(BF16)
        s = lax.dot_general(q2, kb, (((1,), (1,)), ((), ())), preferred_element_type=F32) + bias_ref[...]
        if left_edge:
            s = jnp.where(jcol >= ATT_LEFT - start, s, -jnp.inf)
        p = jnp.exp(s - jnp.max(s, axis=-1, keepdims=True))
        o2 = jnp.dot(p.astype(BF16), vb, preferred_element_type=F32) / jnp.sum(p, axis=-1, keepdims=True)
        out = o2[0:ATT_QBLOCK, :]
        for h in range(1, n_pair):
            out = jnp.where(in_head[h], o2[h * ATT_QBLOCK:(h + 1) * ATT_QBLOCK, :], out)
        o_ref[pl.ds(start, ATT_QBLOCK), :] = out

    n_edge = ATT_LEFT // ATT_QBLOCK
    for c in range(n_edge):
        block(c * ATT_QBLOCK, True)

    def body(c, carry):
        block(pl.multiple_of(c * ATT_QBLOCK, ATT_QBLOCK), False)
        return carry

    lax.fori_loop(n_edge, seq // ATT_QBLOCK, body, 0, unroll=4)


def _attention_mixer(proj, rel_bias, bsz, seq):
    d = proj.shape[-1] // 3
    n_heads = rel_bias.shape[0]
    dh = d // n_heads
    n_pair = LANES // dh
    groups = n_heads // n_pair
    qo = jnp.arange(ATT_QBLOCK)[:, None]
    kj = jnp.arange(ATT_WINDOW)[None, :]
    lo = (qo // CHUNK) * CHUNK
    in_window = jnp.logical_and(kj >= lo, kj < lo + ATT_LEFT + CHUNK)
    d_max = ATT_QBLOCK - 1 + ATT_LEFT
    d_min = -(ATT_WINDOW - 1 - ATT_LEFT)
    n_rel = rel_bias.shape[1]
    ext = jnp.concatenate(
        [jnp.broadcast_to(rel_bias[:, :1], (n_heads, -(CHUNK - 1) - d_min)), rel_bias,
         jnp.broadcast_to(rel_bias[:, n_rel - 1:], (n_heads, d_max - ATT_MAX_REL))], axis=1)
    rows = [ext[:, q_ + ATT_LEFT - (ATT_WINDOW - 1) - d_min:q_ + ATT_LEFT - d_min + 1] for q_ in range(ATT_QBLOCK)]
    table = jnp.flip(jnp.stack(rows, axis=1), axis=2).astype(F32)
    bias = jnp.where(in_window[None], table, -jnp.inf)
    bias = bias.reshape(groups, n_pair * ATT_QBLOCK, ATT_WINDOW)
    p3 = proj.reshape(bsz, seq, 3 * d)
    out = pl.pallas_call(
        functools.partial(_att_kernel, scale=dh ** -0.5, dh=dh),
        grid=(bsz, groups),
        in_specs=[pl.BlockSpec((None, seq, LANES), lambda b, g: (b, 0, g)),
                  pl.BlockSpec((None, seq, LANES), lambda b, g: (b, 0, groups + g)),
                  pl.BlockSpec((None, seq, LANES), lambda b, g: (b, 0, 2 * groups + g)),
                  pl.BlockSpec((None, n_pair * ATT_QBLOCK, ATT_WINDOW), lambda b, g: (g, 0, 0))],
        out_specs=pl.BlockSpec((None, seq, LANES), lambda b, g: (b, 0, g)),
        out_shape=jax.ShapeDtypeStruct((bsz, seq, d), F32),
        scratch_shapes=[pltpu.VMEM((ATT_LEFT + seq, LANES), F32), pltpu.VMEM((ATT_LEFT + seq, LANES), F32)],
        compiler_params=_cparams("arbitrary", "arbitrary"),
        name="chunk_attention",
    )(p3, p3, p3, bias)
    return out.reshape(bsz * seq, d)


def _mlstm_pre_kernel(xm_ref, cw_ref, cb_ref, wq_ref, wk_ref, wv_ref, wg_ref, bg_ref,
                      q_ref, k_ref, v_ref, xc_ref, g_ref, xpad, *, per, k_scale):
    i = pl.program_id(0)
    tm, inner = xm_ref.shape

    @pl.when(i % per == 0)
    def _():
        xpad[0:CONV_PAD, :] = jnp.zeros((CONV_PAD, inner), F32)

    @pl.when(i % per != 0)
    def _():
        xpad[0:CONV_PAD, :] = xpad[tm:tm + CONV_PAD, :]

    xm = xm_ref[...]
    xpad[CONV_PAD:CONV_PAD + tm, :] = xm
    acc = jnp.broadcast_to(cb_ref[...], (tm, inner))
    for j in range(MLSTM_CONV):
        off = CONV_PAD - (MLSTM_CONV - 1) + j
        acc = acc + cw_ref[j:j + 1, :] * xpad[off:off + tm, :]
    xc = acc * jax.nn.sigmoid(acc)
    xc_ref[...] = xc.astype(BF16)

    def blockdiag(t, w_ref):
        tb = t.astype(BF16)
        return jnp.concatenate(
            [jnp.dot(tb[:, g * LANES:(g + 1) * LANES], w_ref[g], preferred_element_type=F32)
             for g in range(inner // LANES)], axis=1)

    q = blockdiag(xc, wq_ref)
    k = blockdiag(xc, wk_ref) * k_scale
    v = blockdiag(xm, wv_ref)
    qb, kb, vb = q.astype(BF16), k.astype(BF16), v.astype(BF16)
    q_ref[...] = qb
    k_ref[...] = kb
    v_ref[...] = vb
    g_ref[...] = (jnp.dot(qb, wg_ref[0:inner, :], preferred_element_type=F32)
                  + jnp.dot(kb, wg_ref[inner:2 * inner, :], preferred_element_type=F32)
                  + jnp.dot(vb, wg_ref[2 * inner:3 * inner, :], preferred_element_type=F32) + bg_ref[...])


def _blockdiag_tiles(w):
    nb, c, _ = w.shape
    per = LANES // c
    wt = w.reshape(nb // per, per, c, c)
    t = jnp.einsum('gpcd,pq->gpcqd', wt, jnp.eye(per, dtype=w.dtype))
    return t.reshape(nb // per, LANES, LANES).astype(BF16)


def _mlstm_rec_kernel(q_ref, k_ref, v_ref, g_ref, xc_ref, z_ref, gn_ref, skip_ref, o_ref,
                      c_ref, n_ref, m_ref, *, n_heads):
    head0 = pl.program_id(1) * HEADS_PER_STEP
    dh = q_ref.shape[1] // HEADS_PER_STEP

    @pl.when(pl.program_id(2) == 0)
    def _():
        c_ref[...] = jnp.zeros_like(c_ref)
        n_ref[...] = jnp.zeros_like(n_ref)
        m_ref[...] = jnp.zeros_like(m_ref)

    rows = q_ref.shape[0]
    n_chunks = rows // CHUNK
    mask = _chunk_causal_mask(rows)
    lane = lax.broadcasted_iota(jnp.int32, (rows, LANES), 1)
    sub = lax.broadcasted_iota(jnp.int32, (LANES, rows), 0)

    def col(t, idx):
        return jnp.sum(jnp.where(lane == idx, t, 0.0), axis=1, keepdims=True)

    def row(t_t, idx):
        return jnp.sum(jnp.where(sub == idx, t_t, 0.0), axis=0, keepdims=True)

    def per_chunk(vals):
        return jnp.concatenate([jnp.broadcast_to(x, (CHUNK, 1)) for x in vals], axis=0)

    gates = g_ref[...]
    bcum = _inchunk_cumsum(jax.nn.log_sigmoid(gates))
    gates_t, bcum_t = gates.T, bcum.T
    for hh in range(HEADS_PER_STEP):
        hs = slice(hh * dh, (hh + 1) * dh)
        i_idx, f_idx = head0 + hh, n_heads + head0 + hh
        i_col, b_col = col(gates, i_idx), col(bcum, f_idx)
        b_last_rows = _chunk_last_rows(b_col)
        log_w = b_last_rows - b_col + i_col
        m_start, m_next, keep = [], [], []
        m_c = m_ref[hh, :, 0:1]
        for c in range(n_chunks):
            sl = slice(c * CHUNK, (c + 1) * CHUNK)
            log_keep = b_last_rows[c * CHUNK:c * CHUNK + 1, :] + m_c
            m_n = jnp.maximum(log_keep, jnp.max(log_w[sl], axis=0, keepdims=True))
            m_start.append(m_c)
            m_next.append(m_n)
            keep.append(jnp.exp(log_keep - m_n))
            m_c = m_n
        m_ref[hh] = jnp.broadcast_to(m_c, (1, LANES))
        log_intra = jnp.where(mask, b_col - row(bcum_t, f_idx) + row(gates_t, i_idx), -jnp.inf)
        log_inter = b_col + per_chunk(m_start)
        m_row = jnp.maximum(log_inter, jnp.max(log_intra, axis=1, keepdims=True))
        w_intra = jnp.exp(log_intra - m_row)
        w_inter = jnp.exp(log_inter - m_row)
        qb, kb, vb = q_ref[:, hs], k_ref[:, hs], v_ref[:, hs]
        q, k = qb.astype(F32), kb.astype(F32)
        s = lax.dot_general(qb, kb, (((1,), (1,)), ((), ())), preferred_element_type=F32) * w_intra
        num_intra = jnp.dot(s.astype(BF16), vb, preferred_element_type=F32)
        den_intra = jnp.sum(s, axis=1, keepdims=True)
        wk = jnp.exp(log_w - per_chunk(m_next)) * k
        wkb = wk.astype(BF16)
        cmat, nvec = c_ref[hh], n_ref[hh]
        num_inter, den_inter = [], []
        for c in range(n_chunks):
            sl = slice(c * CHUNK, (c + 1) * CHUNK)
            num_inter.append(jnp.dot(qb[sl], cmat.astype(BF16), preferred_element_type=F32))
            den_inter.append(jnp.sum(q[sl] * nvec, axis=1, keepdims=True))
            upd = lax.dot_general(wkb[sl], vb[sl], (((0,), (0,)), ((), ())), preferred_element_type=F32)
            cmat = keep[c] * cmat + upd
            nvec = keep[c] * nvec + jnp.sum(wk[sl], axis=0, keepdims=True)
        c_ref[hh] = cmat
        n_ref[hh] = nvec
        num = w_inter * jnp.concatenate(num_inter, axis=0) + num_intra
        den = w_inter * jnp.concatenate(den_inter, axis=0) + den_intra
        hc = num / jnp.maximum(jnp.abs(den), jnp.exp(-m_row))
        z = z_ref[:, hs]
        o_ref[:, hs] = ((_row_norm(hc, True) * gn_ref[hh] + skip_ref[:, hs] * xc_ref[:, hs])
                        * (z * jax.nn.sigmoid(z)))


def _mlstm_mixer(proj, conv_w, conv_b, w_q, w_k, w_v, w_gates, b_gates, gn_g, skip, bsz, seq):
    t = proj.shape[0]
    inner = proj.shape[1] // 2
    n_heads, dh = gn_g.shape
    tm = MLSTM_PRE_TILE
    n_g = w_gates.shape[1]
    wg = jnp.pad(w_gates, ((0, 0), (0, LANES - n_g))).astype(BF16)
    bg = jnp.pad(b_gates.reshape(1, n_g), ((0, 0), (0, LANES - n_g)))
    tile_spec = pl.BlockSpec((tm, inner), lambda i: (i, 0))
    bd_spec = pl.BlockSpec((inner // LANES, LANES, LANES), lambda i: (0, 0, 0))
    q, k, v, xc, gates = pl.pallas_call(
        functools.partial(_mlstm_pre_kernel, per=seq // tm, k_scale=dh ** -0.5),
        grid=(t // tm,),
        in_specs=[tile_spec,
                  pl.BlockSpec((MLSTM_CONV, inner), lambda i: (0, 0)),
                  pl.BlockSpec((1, inner), lambda i: (0, 0)),
                  bd_spec, bd_spec, bd_spec,
                  pl.BlockSpec((3 * inner, LANES), lambda i: (0, 0)),
                  pl.BlockSpec((1, LANES), lambda i: (0, 0))],
        out_specs=[tile_spec, tile_spec, tile_spec, tile_spec, pl.BlockSpec((tm, LANES), lambda i: (i, 0))],
        out_shape=[jax.ShapeDtypeStruct((t, inner), BF16)] * 4 + [jax.ShapeDtypeStruct((t, LANES), F32)],
        scratch_shapes=[pltpu.VMEM((tm + CONV_PAD, inner), F32)],
        compiler_params=_cparams("arbitrary"),
        name="mlstm_pre",
    )(proj, conv_w, conv_b.reshape(1, inner), _blockdiag_tiles(w_q), _blockdiag_tiles(w_k),
      _blockdiag_tiles(w_v), wg, bg)

    lb = SEQ_BLOCK
    hps = HEADS_PER_STEP
    n_hp = n_heads // hps
    r3 = lambda a: a.reshape(bsz, seq, a.shape[-1])
    head_spec = pl.BlockSpec((None, lb, hps * dh), lambda b, h, s: (b, s, h))
    out = pl.pallas_call(
        functools.partial(_mlstm_rec_kernel, n_heads=n_heads),
        grid=(bsz, n_hp, seq // lb),
        in_specs=[head_spec, head_spec, head_spec,
                  pl.BlockSpec((None, lb, LANES), lambda b, h, s: (b, s, 0)),
                  head_spec,
                  pl.BlockSpec((None, lb, hps * dh), lambda b, h, s: (b, s, n_hp + h)),
                  pl.BlockSpec((hps, 1, dh), lambda b, h, s: (h, 0, 0)),
                  pl.BlockSpec((1, hps * dh), lambda b, h, s: (0, h))],
        out_specs=head_spec,
        out_shape=jax.ShapeDtypeStruct((bsz, seq, inner), F32),
        scratch_shapes=[pltpu.VMEM((hps, dh, dh), F32), pltpu.VMEM((hps, 1, dh), F32),
                        pltpu.VMEM((hps, 1, LANES), F32)],
        compiler_params=_cparams("arbitrary", "arbitrary", "arbitrary"),
        name="mlstm_rec",
    )(r3(q), r3(k), r3(v), r3(gates), r3(xc), r3(proj), gn_g.reshape(n_heads, 1, dh), skip.reshape(1, inner))
    return out.reshape(t, inner)


def kernel(x, c, ada_w, ada_b, norm_mix_g, norm_ffn_g, norm_final_g, ret_w_in, ret_gn_g, ret_w_out, att_w_in, att_rel_bias, att_w_out, gla_w_in, gla_w_gate, gla_b_gate, gla_gn_g, gla_w_out, mlstm_w_in, mlstm_conv_w, mlstm_conv_b, mlstm_w_q, mlstm_w_k, mlstm_w_v, mlstm_w_gates, mlstm_b_gates, mlstm_gn_g, mlstm_skip, mlstm_w_out, router_w, router_b, moe_w1, moe_b1, moe_w2, moe_b2):
    bsz, seq, d = x.shape
    depth, n_exp, _, dh2 = moe_w1.shape
    xt = x.reshape(bsz * seq, d)
    mod = _ada_mod(c, ada_w, ada_b)
    w1_all = moe_w1.reshape(depth * n_exp, d, dh2)
    w2_all = moe_w2.reshape(depth * n_exp, dh2 // 2, d)
    b1_all = moe_b1.reshape(depth * n_exp, 1, dh2)
    b2_all = moe_b2.reshape(depth * n_exp, 1, d)
    for i in range(depth):
        kind, j = i % 4, i // 4
        sh_a, sc_a, g_a, sh_f, sc_f, g_f = jnp.split(mod[i], 6, axis=-1)
        if kind == 0:
            proj = _norm_mm(xt, norm_mix_g[i], sc_a, sh_a, ret_w_in[j].astype(BF16), seq)
            y = _retention_mixer(proj, ret_gn_g[j], bsz, seq)
            w_out = ret_w_out[j]
        elif kind == 1:
            proj = _norm_mm(xt, norm_mix_g[i], sc_a, sh_a, att_w_in[j].astype(BF16), seq)
            y = _attention_mixer(proj, att_rel_bias[j], bsz, seq)
            w_out = att_w_out[j]
        elif kind == 2:
            n_heads, dv = gla_gn_g[j].shape
            w_gla = _gla_proj_weight(gla_w_in[j], n_heads * dv // 2, n_heads * dv).astype(BF16)
            proj = _norm_mm(xt, norm_mix_g[i], sc_a, sh_a, w_gla, seq, tn=w_gla.shape[1] // 5)
            y = _gla_mixer(proj, gla_w_gate[j], gla_b_gate[j], gla_gn_g[j], bsz, seq)
            w_out = gla_w_out[j]
        else:
            proj = _norm_mm(xt, norm_mix_g[i], sc_a, sh_a, mlstm_w_in[j].astype(BF16), seq)
            y = _mlstm_mixer(proj, mlstm_conv_w[j], mlstm_conv_b[j], mlstm_w_q[j], mlstm_w_k[j], mlstm_w_v[j],
                             mlstm_w_gates[j], mlstm_b_gates[j], mlstm_gn_g[j], mlstm_skip[j], bsz, seq)
            w_out = mlstm_w_out[j]
        xt = _mm_res(y, w_out.astype(BF16), xt, g_a, seq)
        xt = _moe_layer(xt, norm_ffn_g[i], sc_f, sh_f, g_f, router_w[i], router_b[i],
                        w1_all, b1_all, w2_all, b2_all, i, seq)
    return _final_norm(xt, norm_final_g).reshape(bsz, seq, d)
```

```python
import functools

import jax
import jax.numpy as jnp
from jax import lax
from jax.experimental import pallas as pl
from jax.experimental.pallas import tpu as pltpu

F32 = jnp.float32
BF16 = jnp.bfloat16
HI = lax.Precision.HIGHEST

CHUNK = 64
EPS = 1e-6
ROPE_BASE = 10000.0
ATT_LEFT_CHUNKS = 8
ATT_MAX_REL = 256
GLA_GATE_RANK = 16
GLA_GATE_NORM = 16.0
MLSTM_CONV = 4
N_EXPERTS = 32
TOP_K = 4
SWIGLU_LIMIT = 7.0
SWIGLU_ALPHA = 1.702

VMEM_LIMIT_BYTES = 56 * 1024 * 1024
LANES = 128
SUBLANES = 8
MOE_TILE = 512
COMBINE_TILE = 256
COPIES_PER_ITER = 32
SEQ_BLOCK = 256
MLSTM_PRE_TILE = 256
CONV_PAD = 8
ATT_QBLOCK = 2 * CHUNK
ATT_LEFT = ATT_LEFT_CHUNKS * CHUNK
ATT_WINDOW = ATT_LEFT + ATT_QBLOCK


def _cparams(*sem):
    return pltpu.CompilerParams(dimension_semantics=sem, vmem_limit_bytes=VMEM_LIMIT_BYTES)


def _dot_bf16(a, b, dims=(((1,), (0,)), ((), ()))):
    return lax.dot_general(a.astype(BF16), b.astype(BF16), dims, preferred_element_type=F32)


def _tril_ones(n):
    row = lax.broadcasted_iota(jnp.int32, (n, n), 0)
    col = lax.broadcasted_iota(jnp.int32, (n, n), 1)
    return row >= col


def _row_norm(x, center):
    if center:
        x = x - jnp.mean(x, axis=-1, keepdims=True)
    return x * lax.rsqrt(jnp.mean(x * x, axis=-1, keepdims=True) + EPS)


def _load_row_tiles(ref, rows):
    return jnp.concatenate([ref[pl.ds(j, rows, stride=SUBLANES), :] for j in range(SUBLANES)], axis=1)


def _store_row_tiles(ref, val):
    rows = val.shape[0]
    for j in range(SUBLANES):
        ref[pl.ds(j, rows, stride=SUBLANES), :] = val[:, j * LANES:(j + 1) * LANES]


def _ada_kernel(c_ref, w_ref, b_ref, o_ref):
    c = c_ref[...]
    cond = c * jax.nn.sigmoid(c)
    o_ref[0] = jnp.dot(cond, w_ref[0], preferred_element_type=F32, precision=HI) + b_ref[0]


def _ada_mod(c, ada_w, ada_b):
    depth, d, n = ada_w.shape
    b = c.shape[0]
    tn = 1536
    return pl.pallas_call(
        _ada_kernel,
        grid=(depth, n // tn),
        in_specs=[pl.BlockSpec((b, d), lambda l, j: (0, 0)),
                  pl.BlockSpec((1, d, tn), lambda l, j: (l, 0, j)),
                  pl.BlockSpec((1, 1, tn), lambda l, j: (l, 0, j))],
        out_specs=pl.BlockSpec((1, b, tn), lambda l, j: (l, 0, j)),
        out_shape=jax.ShapeDtypeStruct((depth, b, n), F32),
        compiler_params=_cparams("arbitrary", "arbitrary"),
        name="ada_mod",
    )(c, ada_w, ada_b.reshape(depth, 1, n))


def _modulated_norm(x, g, sc, sh):
    ms = jnp.mean(x * x, axis=-1, keepdims=True)
    return (x * lax.rsqrt(ms + EPS) * g) * (1.0 + sc) + sh


def _norm_mm_kernel(x_ref, g_ref, sc_ref, sh_ref, w_ref, o_ref, h_scr):
    @pl.when(pl.program_id(1) == 0)
    def _():
        h = _modulated_norm(x_ref[...], g_ref[...], sc_ref[0], sh_ref[0])
        h_scr[...] = h.astype(BF16)

    o_ref[...] = jnp.dot(h_scr[...], w_ref[...], preferred_element_type=F32)


NORM_MM_MAX_COLS = 2048


def _col_tile(n):
    best = LANES
    for m in range(1, n // LANES + 1):
        if (n // LANES) % m == 0 and m * LANES <= NORM_MM_MAX_COLS:
            best = m * LANES
    return best


def _norm_mm(x, g, sc, sh, w, seq, tm=1024):
    t, d = x.shape
    n = w.shape[1]
    tn = _col_tile(n)
    per = seq // tm
    bsz = sc.shape[0]
    return pl.pallas_call(
        _norm_mm_kernel,
        grid=(t // tm, n // tn),
        in_specs=[pl.BlockSpec((tm, d), lambda i, j: (i, 0)),
                  pl.BlockSpec((1, d), lambda i, j: (0, 0)),
                  pl.BlockSpec((1, 1, d), lambda i, j: (i // per, 0, 0)),
                  pl.BlockSpec((1, 1, d), lambda i, j: (i // per, 0, 0)),
                  pl.BlockSpec((d, tn), lambda i, j: (0, j))],
        out_specs=pl.BlockSpec((tm, tn), lambda i, j: (i, j)),
        out_shape=jax.ShapeDtypeStruct((t, n), F32),
        scratch_shapes=[pltpu.VMEM((tm, d), BF16)],
        compiler_params=_cparams("arbitrary", "arbitrary"),
        name="norm_mm",
    )(x, g.reshape(1, d), sc.reshape(bsz, 1, d), sh.reshape(bsz, 1, d), w)


def _mm_res_kernel(y_ref, w_ref, x_ref, gate_ref, o_ref):
    acc = jnp.dot(y_ref[...].astype(BF16), w_ref[...], preferred_element_type=F32)
    o_ref[...] = x_ref[...] + gate_ref[0] * acc


def _mm_res(y, w, x, gate, seq, tm=512):
    t, k = y.shape
    d = w.shape[1]
    per = seq // tm
    bsz = gate.shape[0]
    return pl.pallas_call(
        _mm_res_kernel,
        grid=(t // tm,),
        in_specs=[pl.BlockSpec((tm, k), lambda i: (i, 0)),
                  pl.BlockSpec((k, d), lambda i: (0, 0)),
                  pl.BlockSpec((tm, d), lambda i: (i, 0)),
                  pl.BlockSpec((1, 1, d), lambda i: (i // per, 0, 0))],
        out_specs=pl.BlockSpec((tm, d), lambda i: (i, 0)),
        out_shape=jax.ShapeDtypeStruct((t, d), F32),
        compiler_params=_cparams("arbitrary"),
        name="mm_res",
    )(y, w, x, gate.reshape(bsz, 1, d))


def _final_norm_kernel(x_ref, g_ref, o_ref):
    x = x_ref[...]
    ms = jnp.mean(x * x, axis=-1, keepdims=True)
    o_ref[...] = x * lax.rsqrt(ms + EPS) * g_ref[...]


def _final_norm(x, g, tm=1024):
    t, d = x.shape
    return pl.pallas_call(
        _final_norm_kernel,
        grid=(t // tm,),
        in_specs=[pl.BlockSpec((tm, d), lambda i: (i, 0)),
                  pl.BlockSpec((1, d), lambda i: (0, 0))],
        out_specs=pl.BlockSpec((tm, d), lambda i: (i, 0)),
        out_shape=jax.ShapeDtypeStruct((t, d), F32),
        compiler_params=_cparams("arbitrary"),
        name="final_norm",
    )(x, g.reshape(1, d))


def _norm_router_kernel(x_ref, g_ref, sc_ref, sh_ref, rw_ref, rb_ref, h_ref, route_ref, *, n_experts):
    h = _modulated_norm(x_ref[...], g_ref[...], sc_ref[0], sh_ref[0])
    _store_row_tiles(h_ref, h)
    logits = jnp.dot(h, rw_ref[...], preferred_element_type=F32, precision=HI) + rb_ref[...]
    lane = lax.broadcasted_iota(jnp.int32, logits.shape, 1)
    rem = jnp.where(lane < n_experts, logits, -jnp.inf)
    vals, idxs = [], []
    for _ in range(TOP_K):
        m = jnp.max(rem, axis=1, keepdims=True)
        idx = jnp.min(jnp.where(rem == m, lane, LANES), axis=1, keepdims=True)
        vals.append(m)
        idxs.append(idx)
        rem = jnp.where(lane == idx, -jnp.inf, rem)
    ex = [jnp.exp(v - vals[0]) for v in vals]
    inv = 1.0 / sum(ex[1:], ex[0])
    out = jnp.zeros(logits.shape, F32)
    for k in range(TOP_K):
        out = jnp.where(lane == k, ex[k] * inv, out)
        out = jnp.where(lane == TOP_K + k, idxs[k].astype(F32), out)
    route_ref[...] = out


def _norm_router(x, g, sc, sh, rw, rb, seq, tm=512):
    t, d = x.shape
    per = seq // tm
    bsz = sc.shape[0]
    e = rw.shape[1]
    rw_p = jnp.pad(rw, ((0, 0), (0, LANES - e)))
    rb_p = jnp.pad(rb.reshape(1, e), ((0, 0), (0, LANES - e)))
    return pl.pallas_call(
        functools.partial(_norm_router_kernel, n_experts=e),
        grid=(t // tm,),
        in_specs=[pl.BlockSpec((tm, d), lambda i: (i, 0)),
                  pl.BlockSpec((1, d), lambda i: (0, 0)),
                  pl.BlockSpec((1, 1, d), lambda i: (i // per, 0, 0)),
                  pl.BlockSpec((1, 1, d), lambda i: (i // per, 0, 0)),
                  pl.BlockSpec((d, LANES), lambda i: (0, 0)),
                  pl.BlockSpec((1, LANES), lambda i: (0, 0))],
        out_specs=[pl.BlockSpec((tm * SUBLANES, LANES), lambda i: (i, 0)),
                   pl.BlockSpec((tm, LANES), lambda i: (i, 0))],
        out_shape=[jax.ShapeDtypeStruct((t * SUBLANES, LANES), F32),
                   jax.ShapeDtypeStruct((t, LANES), F32)],
        compiler_params=_cparams("arbitrary"),
        name="norm_router",
    )(x, g.reshape(1, d), sc.reshape(bsz, 1, d), sh.reshape(bsz, 1, d), rw_p, rb_p)


def _row_tile_copy(src_hbm, src_row8, dst_vmem, dst_row, sem):
    return pltpu.make_async_copy(
        src_hbm.at[pl.ds(pl.multiple_of(src_row8, SUBLANES), SUBLANES)],
        dst_vmem.at[pl.ds(pl.multiple_of(dst_row * SUBLANES, SUBLANES), SUBLANES)], sem)


def _expert_kernel(te_ref, nt_ref, idx_cur_ref, idx_nxt_ref, h_hbm, w1_ref, b1_ref, w2_ref, b2_ref,
                   o_ref, xbuf, w1_bf, w2_bf, sem):
    i = pl.program_id(0)
    n_used = nt_ref[0]
    tm = xbuf.shape[1] // SUBLANES
    slot = i % 2

    @pl.when(jnp.logical_or(i == 0, te_ref[i] != te_ref[jnp.maximum(i - 1, 0)]))
    def _():
        w1_bf[...] = w1_ref[0].astype(BF16)
        w2_bf[...] = w2_ref[0].astype(BF16)

    def issue(idx_ref, s):
        def body(g, carry):
            for u in range(COPIES_PER_ITER):
                r = g * COPIES_PER_ITER + u
                _row_tile_copy(h_hbm, idx_ref[0, 0, r], xbuf.at[s], r, sem.at[s]).start(priority=u % 2)
            return carry
        lax.fori_loop(0, tm // COPIES_PER_ITER, body, 0)

    @pl.when(jnp.logical_and(i == 0, n_used > 0))
    def _():
        issue(idx_cur_ref, 0)

    @pl.when(i + 1 < n_used)
    def _():
        issue(idx_nxt_ref, 1 - slot)

    @pl.when(i < n_used)
    def _():
        pltpu.make_async_copy(h_hbm.at[pl.ds(0, tm * SUBLANES)], xbuf.at[slot], sem.at[slot]).wait()
        x = _load_row_tiles(xbuf.at[slot], tm).astype(BF16)
        hid = jnp.dot(x, w1_bf[...], preferred_element_type=F32) + b1_ref[0]
        de = hid.shape[1] // 2
        gate = jnp.minimum(hid[:, :de], SWIGLU_LIMIT)
        lin = jnp.clip(hid[:, de:], -SWIGLU_LIMIT, SWIGLU_LIMIT)
        act = gate * jax.nn.sigmoid(SWIGLU_ALPHA * gate) * (lin + 1.0)
        y = jnp.dot(act.astype(BF16), w2_bf[...], preferred_element_type=F32) + b2_ref[0]
        _store_row_tiles(o_ref, y)

    @pl.when(i >= n_used)
    def _():
        o_ref[...] = jnp.zeros_like(o_ref)


def _expert_ffn(h_tiles, src_row8, tile_expert, n_tiles_used, w1, b1, w2, b2, layer):
    p = src_row8.shape[0]
    tm = MOE_TILE
    nt = p // tm
    _, d, dh = w1.shape
    idx3 = src_row8.reshape(nt, 1, tm)
    base = layer * N_EXPERTS
    grid_spec = pltpu.PrefetchScalarGridSpec(
        num_scalar_prefetch=2,
        grid=(nt,),
        in_specs=[
            pl.BlockSpec((1, 1, tm), lambda i, te, nu: (i, 0, 0), memory_space=pltpu.SMEM),
            pl.BlockSpec((1, 1, tm), lambda i, te, nu: (jnp.minimum(i + 1, nt - 1), 0, 0),
                         memory_space=pltpu.SMEM),
            pl.BlockSpec(memory_space=pl.ANY),
            pl.BlockSpec((1, d, dh), lambda i, te, nu: (base + te[i], 0, 0)),
            pl.BlockSpec((1, 1, dh), lambda i, te, nu: (base + te[i], 0, 0)),
            pl.BlockSpec((1, dh // 2, d), lambda i, te, nu: (base + te[i], 0, 0)),
            pl.BlockSpec((1, 1, d), lambda i, te, nu: (base + te[i], 0, 0)),
        ],
        out_specs=pl.BlockSpec((tm * SUBLANES, LANES), lambda i, te, nu: (i, 0)),
        scratch_shapes=[pltpu.VMEM((2, tm * SUBLANES, LANES), F32), pltpu.VMEM((d, dh), BF16),
                        pltpu.VMEM((dh // 2, d), BF16), pltpu.SemaphoreType.DMA((2,))],
    )
    return pl.pallas_call(
        _expert_kernel,
        grid_spec=grid_spec,
        out_shape=jax.ShapeDtypeStruct((p * SUBLANES, LANES), F32),
        compiler_params=_cparams("arbitrary"),
        name="expert_ffn",
    )(tile_expert, n_tiles_used, idx3, idx3, h_tiles, w1, b1, w2, b2)


def _combine_kernel(idx_cur_ref, idx_nxt_ref, y_hbm, x_ref, w_ref, gate_ref, o_ref, ybuf, sem):
    i = pl.program_id(0)
    n = pl.num_programs(0)
    tc = x_ref.shape[0]
    slot = i % 2

    def issue(idx_ref, s):
        rows_per_iter = COPIES_PER_ITER // TOP_K

        def body(g, carry):
            for u in range(rows_per_iter):
                r = g * rows_per_iter + u
                for k in range(TOP_K):
                    _row_tile_copy(y_hbm, idx_ref[0, 0, r * TOP_K + k], ybuf.at[s, k], r,
                                   sem.at[s]).start(priority=k % 2)
            return carry
        lax.fori_loop(0, tc // rows_per_iter, body, 0)

    @pl.when(i == 0)
    def _():
        issue(idx_cur_ref, 0)

    @pl.when(i + 1 < n)
    def _():
        issue(idx_nxt_ref, 1 - slot)

    for k in range(TOP_K):
        pltpu.make_async_copy(y_hbm.at[pl.ds(0, tc * SUBLANES)], ybuf.at[slot, k], sem.at[slot]).wait()
    w = w_ref[...]
    acc = w[:, 0:1] * _load_row_tiles(ybuf.at[slot, 0], tc)
    for k in range(1, TOP_K):
        acc = acc + w[:, k:k + 1] * _load_row_tiles(ybuf.at[slot, k], tc)
    o_ref[...] = x_ref[...] + gate_ref[0] * acc


def _moe_combine(y_tiles, slot8_of_pair, top_w, x, gate, seq):
    t, d = x.shape
    tc = COMBINE_TILE
    nt = t // tc
    per = seq // tc
    bsz = gate.shape[0]
    idx3 = slot8_of_pair.reshape(nt, 1, tc * TOP_K)
    return pl.pallas_call(
        _combine_kernel,
        grid=(nt,),
        in_specs=[
            pl.BlockSpec((1, 1, tc * TOP_K), lambda i: (i, 0, 0), memory_space=pltpu.SMEM),
            pl.BlockSpec((1, 1, tc * TOP_K), lambda i: (jnp.minimum(i + 1, nt - 1), 0, 0),
                         memory_space=pltpu.SMEM),
            pl.BlockSpec(memory_space=pl.ANY),
            pl.BlockSpec((tc, d), lambda i: (i, 0)),
            pl.BlockSpec((tc, TOP_K), lambda i: (i, 0)),
            pl.BlockSpec((1, 1, d), lambda i: (i // per, 0, 0)),
        ],
        out_specs=pl.BlockSpec((tc, d), lambda i: (i, 0)),
        out_shape=jax.ShapeDtypeStruct((t, d), F32),
        scratch_shapes=[pltpu.VMEM((2, TOP_K, tc * SUBLANES, LANES), F32), pltpu.SemaphoreType.DMA((2,))],
        compiler_params=_cparams("arbitrary"),
        name="moe_combine",
    )(idx3, idx3, y_tiles, x, top_w, gate.reshape(bsz, 1, d))


def _route(route):
    t = route.shape[0]
    top_w = route[:, :TOP_K]
    flat_e = route[:, TOP_K:2 * TOP_K].astype(jnp.int32).reshape(-1)
    onehot = (flat_e[:, None] == jnp.arange(N_EXPERTS, dtype=jnp.int32)[None, :]).astype(jnp.int32)
    csum = jnp.cumsum(onehot, axis=0)
    counts = csum[-1]
    padded = ((counts + MOE_TILE - 1) // MOE_TILE) * MOE_TILE
    ends = jnp.cumsum(padded)
    starts = ends - padded
    slot_of_pair = jnp.sum(onehot * (csum - 1 + starts[None, :]), axis=1)
    n_fill = N_EXPERTS * MOE_TILE
    fill_end = jnp.cumsum(padded - counts)
    fill_key = jnp.sum((jnp.arange(n_fill, dtype=jnp.int32)[:, None] >= fill_end[None, :]).astype(jnp.int32), axis=1)
    n_pairs = t * TOP_K
    pos_bits = (n_pairs + n_fill - 1).bit_length()
    assert (N_EXPERTS + 1) << pos_bits < 2 ** 31
    keys = (jnp.concatenate([flat_e, fill_key]) << pos_bits) + jnp.arange(n_pairs + n_fill, dtype=jnp.int32)
    entry = lax.sort(keys) & ((1 << pos_bits) - 1)
    src_tok = jnp.where(entry < n_pairs, entry // TOP_K, 0)
    n_tiles = (n_pairs + n_fill) // MOE_TILE
    tile_start = jnp.arange(n_tiles, dtype=jnp.int32) * MOE_TILE
    tile_expert = jnp.sum((tile_start[:, None] >= ends[None, :]).astype(jnp.int32), axis=1)
    n_used = (ends[-1] // MOE_TILE).astype(jnp.int32)
    last_e = jnp.sum(jnp.where(jnp.arange(n_tiles) == n_used - 1, tile_expert, 0))
    tile_expert = jnp.where(jnp.arange(n_tiles) < n_used, tile_expert, last_e).astype(jnp.int32)
    return (top_w, (slot_of_pair * SUBLANES).astype(jnp.int32), (src_tok * SUBLANES).astype(jnp.int32),
            tile_expert, n_used.reshape(1))


def _moe_layer(x, g, sc, sh, gate, rw, rb, w1, b1, w2, b2, layer, seq):
    h_tiles, route = _norm_router(x, g, sc, sh, rw, rb, seq)
    top_w, slot8_of_pair, src_row8, tile_expert, n_used = _route(route)
    y_tiles = _expert_ffn(h_tiles, src_row8, tile_expert, n_used, w1, b1, w2, b2, layer)
    return _moe_combine(y_tiles, slot8_of_pair, top_w, x, gate, seq)


def _chunk_pos(shape):
    return lax.broadcasted_iota(jnp.int32, shape, 0) & (CHUNK - 1)


def _inchunk_cumsum(x):
    pos = _chunk_pos(x.shape)
    sh = 1
    while sh < CHUNK:
        x = x + jnp.where(pos >= sh, pltpu.roll(x, sh, axis=0), 0.0)
        sh *= 2
    return x


def _chunk_last_rows(b):
    rows, d = b.shape
    b3 = b.reshape(rows // CHUNK, CHUNK, d)
    return jnp.broadcast_to(b3[:, CHUNK - 1:CHUNK, :], b3.shape).reshape(rows, d)


def _chunk_causal_mask(rows):
    r = lax.broadcasted_iota(jnp.int32, (rows, rows), 0)
    c = lax.broadcasted_iota(jnp.int32, (rows, rows), 1)
    return jnp.logical_and(r >= c, (r // CHUNK) == (c // CHUNK))


def _linattn_block(q, k, v, b, b_rem, chunk_decay, state_ref):
    rows = q.shape[0]
    q_in = (q * jnp.exp(b)).astype(BF16)
    k_in = (k * jnp.exp(-b)).astype(BF16)
    k_st = (k * jnp.exp(b_rem)).astype(BF16)
    vb = v.astype(BF16)
    scores = lax.dot_general(q_in, k_in, (((1,), (1,)), ((), ())), preferred_element_type=F32)
    scores = jnp.where(_chunk_causal_mask(rows), scores, 0.0)
    o_intra = jnp.dot(scores.astype(BF16), vb, preferred_element_type=F32)
    state = state_ref[...]
    o_inter = []
    for c in range(rows // CHUNK):
        sl = slice(c * CHUNK, (c + 1) * CHUNK)
        o_inter.append(jnp.dot(q_in[sl], state.astype(BF16), preferred_element_type=F32))
        upd = lax.dot_general(k_st[sl], vb[sl], (((0,), (0,)), ((), ())), preferred_element_type=F32)
        state = chunk_decay(c) * state + upd
    state_ref[...] = state
    return o_intra + jnp.concatenate(o_inter, axis=0)


HEADS_PER_STEP = 4


def _ret_kernel(q_ref, k_ref, v_ref, g_ref, cos_ref, sin_ref, lg_ref, gn_ref, o_ref, state_ref, *, k_scale):
    @pl.when(pl.program_id(2) == 0)
    def _():
        state_ref[...] = jnp.zeros_like(state_ref)

    rows = q_ref.shape[0]
    dk = q_ref.shape[1] // HEADS_PER_STEP
    dv = v_ref.shape[1] // HEADS_PER_STEP
    half = dk // 2
    cos, sin = cos_ref[...], sin_ref[...]

    def rope(t):
        t1, t2 = t[:, :half], t[:, half:]
        return jnp.concatenate([t1 * cos - t2 * sin, t1 * sin + t2 * cos], axis=1)

    pos = _chunk_pos((rows, dk)).astype(F32)
    for hh in range(HEADS_PER_STEP):
        ks, vs = slice(hh * dk, (hh + 1) * dk), slice(hh * dv, (hh + 1) * dv)
        lg = lg_ref[hh]
        b = (pos + 1.0) * lg
        b_rem = (CHUNK - 1.0 - pos) * lg
        decay = jnp.exp(CHUNK * lg[:, 0:1])
        q = rope(q_ref[:, ks])
        k = rope(k_ref[:, ks]) * k_scale
        o = _linattn_block(q, k, v_ref[:, vs], b, b_rem, lambda c: decay, state_ref.at[hh])
        g = g_ref[:, vs]
        o_ref[:, vs] = _row_norm(o, True) * gn_ref[hh] * (g * jax.nn.sigmoid(g))


def _retention_mixer(proj, gn_g, bsz, seq):
    n_heads, dv = gn_g.shape
    dk = dv // 2
    lb = SEQ_BLOCK
    hps = HEADS_PER_STEP
    n_hp = n_heads // hps
    p3 = proj.reshape(bsz, seq, proj.shape[-1])
    half = dk // 2
    inv = ROPE_BASE ** (-jnp.arange(half, dtype=F32) / half)
    ang = jnp.arange(seq, dtype=F32)[:, None] * inv[None, :]
    cos, sin = jnp.cos(ang), jnp.sin(ang)
    log_gamma = jnp.log1p(-jnp.exp2(-5.0 - jnp.arange(n_heads, dtype=F32)))
    lg = jnp.broadcast_to(log_gamma[:, None, None], (n_heads, 1, dk))
    out = pl.pallas_call(
        functools.partial(_ret_kernel, k_scale=dk ** -0.5),
        grid=(bsz, n_hp, seq // lb),
        in_specs=[pl.BlockSpec((None, lb, hps * dk), lambda b, h, s: (b, s, h)),
                  pl.BlockSpec((None, lb, hps * dk), lambda b, h, s: (b, s, n_hp + h)),
                  pl.BlockSpec((None, lb, hps * dv), lambda b, h, s: (b, s, n_hp + h)),
                  pl.BlockSpec((None, lb, hps * dv), lambda b, h, s: (b, s, 2 * n_hp + h)),
                  pl.BlockSpec((lb, half), lambda b, h, s: (s, 0)),
                  pl.BlockSpec((lb, half), lambda b, h, s: (s, 0)),
                  pl.BlockSpec((hps, 1, dk), lambda b, h, s: (h, 0, 0)),
                  pl.BlockSpec((hps, 1, dv), lambda b, h, s: (h, 0, 0))],
        out_specs=pl.BlockSpec((None, lb, hps * dv), lambda b, h, s: (b, s, h)),
        out_shape=jax.ShapeDtypeStruct((bsz, seq, n_heads * dv), F32),
        scratch_shapes=[pltpu.VMEM((hps, dk, dv), F32)],
        compiler_params=_cparams("arbitrary", "arbitrary", "arbitrary"),
        name="retention",
    )(p3, p3, p3, p3, cos, sin, lg, gn_g.reshape(n_heads, 1, dv))
    return out.reshape(bsz * seq, n_heads * dv)


def _gla_kernel(q_ref, k_ref, v_ref, r_ref, glow_ref, wg_ref, bg_ref, gn_ref, o_ref, state_ref, *, q_scale):
    @pl.when(pl.program_id(2) == 0)
    def _():
        state_ref[...] = jnp.zeros_like(state_ref)

    dk = q_ref.shape[1] // HEADS_PER_STEP
    dv = v_ref.shape[1] // HEADS_PER_STEP

    rows = q_ref.shape[0]
    z = jnp.dot(glow_ref[...], wg_ref[...], precision=HI, preferred_element_type=F32) + bg_ref[...]
    la_all = jax.nn.log_sigmoid(z) / GLA_GATE_NORM
    chunk_sel = (lax.broadcasted_iota(jnp.int32, (rows, LANES), 0) // CHUNK
                 == lax.broadcasted_iota(jnp.int32, (rows, LANES), 1)).astype(F32)
    for hh in range(HEADS_PER_STEP):
        ks, vs = slice(hh * dk, (hh + 1) * dk), slice(hh * dv, (hh + 1) * dv)
        la = la_all[:, ks]
        b = _inchunk_cumsum(la)
        b_rem = _chunk_last_rows(b) - b
        decay = jnp.exp(lax.dot_general(la, chunk_sel, (((0,), (0,)), ((), ())),
                                        precision=HI, preferred_element_type=F32))
        o = _linattn_block(q_ref[:, ks] * q_scale, k_ref[:, ks], v_ref[:, vs], b, b_rem,
                           lambda c: decay[:, c:c + 1], state_ref.at[hh])
        r = r_ref[:, vs]
        o_ref[:, vs] = _row_norm(o, False) * gn_ref[hh] * (r * jax.nn.sigmoid(r))


def _gla_proj_weight(w_in, dq, dvv):
    q_k_v = w_in[:, :2 * dq + dvv]
    g_low = w_in[:, 2 * dq + dvv:2 * dq + dvv + GLA_GATE_RANK]
    r = w_in[:, 2 * dq + dvv + GLA_GATE_RANK:]
    pad = jnp.zeros((w_in.shape[0], LANES - GLA_GATE_RANK), w_in.dtype)
    return jnp.concatenate([q_k_v, r, g_low, pad], axis=1)


def _gla_mixer(proj, w_gate, b_gate, gn_g, bsz, seq):
    n_heads, dv = gn_g.shape
    dk = dv // 2
    dq, dvv = n_heads * dk, n_heads * dv
    lb = SEQ_BLOCK
    hps = HEADS_PER_STEP
    n_hp = n_heads // hps
    p3 = proj.reshape(bsz, seq, proj.shape[-1])
    wg = jnp.pad(w_gate, ((0, LANES - GLA_GATE_RANK), (0, 0)))
    out = pl.pallas_call(
        functools.partial(_gla_kernel, q_scale=dk ** -0.5),
        grid=(bsz, n_hp, seq // lb),
        in_specs=[pl.BlockSpec((None, lb, hps * dk), lambda b, h, s: (b, s, h)),
                  pl.BlockSpec((None, lb, hps * dk), lambda b, h, s: (b, s, n_hp + h)),
                  pl.BlockSpec((None, lb, hps * dv), lambda b, h, s: (b, s, n_hp + h)),
                  pl.BlockSpec((None, lb, hps * dv), lambda b, h, s: (b, s, 2 * n_hp + h)),
                  pl.BlockSpec((None, lb, LANES), lambda b, h, s: (b, s, (2 * dq + 2 * dvv) // LANES)),
                  pl.BlockSpec((LANES, hps * dk), lambda b, h, s: (0, h)),
                  pl.BlockSpec((1, hps * dk), lambda b, h, s: (0, h)),
                  pl.BlockSpec((hps, 1, dv), lambda b, h, s: (h, 0, 0))],
        out_specs=pl.BlockSpec((None, lb, hps * dv), lambda b, h, s: (b, s, h)),
        out_shape=jax.ShapeDtypeStruct((bsz, seq, dvv), F32),
        scratch_shapes=[pltpu.VMEM((hps, dk, dv), F32)],
        compiler_params=_cparams("arbitrary", "arbitrary", "arbitrary"),
        name="gla",
    )(p3, p3, p3, p3, p3, wg, b_gate.reshape(1, dq), gn_g.reshape(n_heads, 1, dv))
    return out.reshape(bsz * seq, dvv)


def _att_kernel(q_ref, k_ref, v_ref, bias_ref, o_ref, kpad, vpad, *, scale, dh):
    seq, width = q_ref.shape
    n_pair = width // dh
    kpad[0:ATT_LEFT, :] = jnp.zeros((ATT_LEFT, width), F32)
    vpad[0:ATT_LEFT, :] = jnp.zeros((ATT_LEFT, width), F32)
    kpad[ATT_LEFT:ATT_LEFT + seq, :] = k_ref[...]
    vpad[ATT_LEFT:ATT_LEFT + seq, :] = v_ref[...]
    lane = lax.broadcasted_iota(jnp.int32, (ATT_QBLOCK, width), 1)
    jcol = lax.broadcasted_iota(jnp.int32, (n_pair * ATT_QBLOCK, ATT_WINDOW), 1)
    in_head = [jnp.logical_and(lane >= h * dh, lane < (h + 1) * dh) for h in range(n_pair)]

    def block(start, left_edge):
        q = q_ref[pl.ds(start, ATT_QBLOCK), :] * scale
        kb = kpad[pl.ds(start, ATT_WINDOW), :].astype(BF16)
        vb = vpad[pl.ds(start, ATT_WINDOW), :].astype(BF16)
        q2 = jnp.concatenate([jnp.where(m, q, 0.0) for m in in_head], axis=0).astype(BF16)
        s = lax.dot_general(q2, kb, (((1,), (1,)), ((), ())), preferred_element_type=F32) + bias_ref[...]
        if left_edge:
            s = jnp.where(jcol >= ATT_LEFT - start, s, -jnp.inf)
        p = jnp.exp(s - jnp.max(s, axis=-1, keepdims=True))
        o2 = jnp.dot(p.astype(BF16), vb, preferred_element_type=F32) / jnp.sum(p, axis=-1, keepdims=True)
        out = o2[0:ATT_QBLOCK, :]
        for h in range(1, n_pair):
            out = jnp.where(in_head[h], o2[h * ATT_QBLOCK:(h + 1) * ATT_QBLOCK, :], out)
        o_ref[pl.ds(start, ATT_QBLOCK), :] = out

    n_edge = ATT_LEFT // ATT_QBLOCK
    for c in range(n_edge):
        block(c * ATT_QBLOCK, True)

    def body(c, carry):
        block(pl.multiple_of(c * ATT_QBLOCK, ATT_QBLOCK), False)
        return carry

    lax.fori_loop(n_edge, seq // ATT_QBLOCK, body, 0, unroll=4)


def _attention_mixer(proj, rel_bias, bsz, seq):
    d = proj.shape[-1] // 3
    n_heads = rel_bias.shape[0]
    dh = d // n_heads
    n_pair = LANES // dh
    groups = n_heads // n_pair
    qo = jnp.arange(ATT_QBLOCK)[:, None]
    kj = jnp.arange(ATT_WINDOW)[None, :]
    lo = (qo // CHUNK) * CHUNK
    in_window = jnp.logical_and(kj >= lo, kj < lo + ATT_LEFT + CHUNK)
    d_max = ATT_QBLOCK - 1 + ATT_LEFT
    d_min = -(ATT_WINDOW - 1 - ATT_LEFT)
    n_rel = rel_bias.shape[1]
    ext = jnp.concatenate(
        [jnp.broadcast_to(rel_bias[:, :1], (n_heads, -(CHUNK - 1) - d_min)), rel_bias,
         jnp.broadcast_to(rel_bias[:, n_rel - 1:], (n_heads, d_max - ATT_MAX_REL))], axis=1)
    rows = [ext[:, q_ + ATT_LEFT - (ATT_WINDOW - 1) - d_min:q_ + ATT_LEFT - d_min + 1] for q_ in range(ATT_QBLOCK)]
    table = jnp.flip(jnp.stack(rows, axis=1), axis=2).astype(F32)
    bias = jnp.where(in_window[None], table, -jnp.inf)
    bias = bias.reshape(groups, n_pair * ATT_QBLOCK, ATT_WINDOW)
    p3 = proj.reshape(bsz, seq, 3 * d)
    out = pl.pallas_call(
        functools.partial(_att_kernel, scale=dh ** -0.5, dh=dh),
        grid=(bsz, groups),
        in_specs=[pl.BlockSpec((None, seq, LANES), lambda b, g: (b, 0, g)),
                  pl.BlockSpec((None, seq, LANES), lambda b, g: (b, 0, groups + g)),
                  pl.BlockSpec((None, seq, LANES), lambda b, g: (b, 0, 2 * groups + g)),
                  pl.BlockSpec((None, n_pair * ATT_QBLOCK, ATT_WINDOW), lambda b, g: (g, 0, 0))],
        out_specs=pl.BlockSpec((None, seq, LANES), lambda b, g: (b, 0, g)),
        out_shape=jax.ShapeDtypeStruct((bsz, seq, d), F32),
        scratch_shapes=[pltpu.VMEM((ATT_LEFT + seq, LANES), F32), pltpu.VMEM((ATT_LEFT + seq, LANES), F32)],
        compiler_params=_cparams("arbitrary", "arbitrary"),
        name="chunk_attention",
    )(p3, p3, p3, bias)
    return out.reshape(bsz * seq, d)


def _mlstm_pre_kernel(xm_ref, cw_ref, cb_ref, wq_ref, wk_ref, wv_ref, wg_ref, bg_ref,
                      q_ref, k_ref, v_ref, xc_ref, g_ref, xpad, *, per, k_scale):
    i = pl.program_id(0)
    tm, inner = xm_ref.shape

    @pl.when(i % per == 0)
    def _():
        xpad[0:CONV_PAD, :] = jnp.zeros((CONV_PAD, inner), F32)

    @pl.when(i % per != 0)
    def _():
        xpad[0:CONV_PAD, :] = xpad[tm:tm + CONV_PAD, :]

    xm = xm_ref[...]
    xpad[CONV_PAD:CONV_PAD + tm, :] = xm
    acc = jnp.broadcast_to(cb_ref[...], (tm, inner))
    for j in range(MLSTM_CONV):
        off = CONV_PAD - (MLSTM_CONV - 1) + j
        acc = acc + cw_ref[j:j + 1, :] * xpad[off:off + tm, :]
    xc = acc * jax.nn.sigmoid(acc)
    xc_ref[...] = xc.astype(BF16)

    def blockdiag(t, w_ref):
        tb = t.astype(BF16)
        return jnp.concatenate(
            [jnp.dot(tb[:, g * LANES:(g + 1) * LANES], w_ref[g], preferred_element_type=F32)
             for g in range(inner // LANES)], axis=1)

    q = blockdiag(xc, wq_ref)
    k = blockdiag(xc, wk_ref) * k_scale
    v = blockdiag(xm, wv_ref)
    qb, kb, vb = q.astype(BF16), k.astype(BF16), v.astype(BF16)
    q_ref[...] = qb
    k_ref[...] = kb
    v_ref[...] = vb
    g_ref[...] = (jnp.dot(qb, wg_ref[0:inner, :], preferred_element_type=F32)
                  + jnp.dot(kb, wg_ref[inner:2 * inner, :], preferred_element_type=F32)
                  + jnp.dot(vb, wg_ref[2 * inner:3 * inner, :], preferred_element_type=F32) + bg_ref[...])


def _blockdiag_tiles(w):
    nb, c, _ = w.shape
    per = LANES // c
    wt = w.reshape(nb // per, per, c, c)
    t = jnp.einsum('gpcd,pq->gpcqd', wt, jnp.eye(per, dtype=w.dtype))
    return t.reshape(nb // per, LANES, LANES).astype(BF16)


def _mlstm_rec_kernel(q_ref, k_ref, v_ref, g_ref, xc_ref, z_ref, gn_ref, skip_ref, o_ref,
                      c_ref, n_ref, m_ref, *, n_heads):
    head0 = pl.program_id(1) * HEADS_PER_STEP
    dh = q_ref.shape[1] // HEADS_PER_STEP

    @pl.when(pl.program_id(2) == 0)
    def _():
        c_ref[...] = jnp.zeros_like(c_ref)
        n_ref[...] = jnp.zeros_like(n_ref)
        m_ref[...] = jnp.zeros_like(m_ref)

    rows = q_ref.shape[0]
    n_chunks = rows // CHUNK
    mask = _chunk_causal_mask(rows)
    lane = lax.broadcasted_iota(jnp.int32, (rows, LANES), 1)
    sub = lax.broadcasted_iota(jnp.int32, (LANES, rows), 0)

    def col(t, idx):
        return jnp.sum(jnp.where(lane == idx, t, 0.0), axis=1, keepdims=True)

    def row(t_t, idx):
        return jnp.sum(jnp.where(sub == idx, t_t, 0.0), axis=0, keepdims=True)

    def per_chunk(vals):
        return jnp.concatenate([jnp.broadcast_to(x, (CHUNK, 1)) for x in vals], axis=0)

    gates = g_ref[...]
    bcum = _inchunk_cumsum(jax.nn.log_sigmoid(gates))
    gates_t, bcum_t = gates.T, bcum.T
    for hh in range(HEADS_PER_STEP):
        hs = slice(hh * dh, (hh + 1) * dh)
        i_idx, f_idx = head0 + hh, n_heads + head0 + hh
        i_col, b_col = col(gates, i_idx), col(bcum, f_idx)
        b_last_rows = _chunk_last_rows(b_col)
        log_w = b_last_rows - b_col + i_col
        m_start, m_next, keep = [], [], []
        m_c = m_ref[hh, :, 0:1]
        for c in range(n_chunks):
            sl = slice(c * CHUNK, (c + 1) * CHUNK)
            log_keep = b_last_rows[c * CHUNK:c * CHUNK + 1, :] + m_c
            m_n = jnp.maximum(log_keep, jnp.max(log_w[sl], axis=0, keepdims=True))
            m_start.append(m_c)
            m_next.append(m_n)
            keep.append(jnp.exp(log_keep - m_n))
            m_c = m_n
        m_ref[hh] = jnp.broadcast_to(m_c, (1, LANES))
        log_intra = jnp.where(mask, b_col - row(bcum_t, f_idx) + row(gates_t, i_idx), -jnp.inf)
        log_inter = b_col + per_chunk(m_start)
        m_row = jnp.maximum(log_inter, jnp.max(log_intra, axis=1, keepdims=True))
        w_intra = jnp.exp(log_intra - m_row)
        w_inter = jnp.exp(log_inter - m_row)
        qb, kb, vb = q_ref[:, hs], k_ref[:, hs], v_ref[:, hs]
        q, k = qb.astype(F32), kb.astype(F32)
        s = lax.dot_general(qb, kb, (((1,), (1,)), ((), ())), preferred_element_type=F32) * w_intra
        num_intra = jnp.dot(s.astype(BF16), vb, preferred_element_type=F32)
        den_intra = jnp.sum(s, axis=1, keepdims=True)
        wk = jnp.exp(log_w - per_chunk(m_next)) * k
        wkb = wk.astype(BF16)
        cmat, nvec = c_ref[hh], n_ref[hh]
        num_inter, den_inter = [], []
        for c in range(n_chunks):
            sl = slice(c * CHUNK, (c + 1) * CHUNK)
            num_inter.append(jnp.dot(qb[sl], cmat.astype(BF16), preferred_element_type=F32))
            den_inter.append(jnp.sum(q[sl] * nvec, axis=1, keepdims=True))
            upd = lax.dot_general(wkb[sl], vb[sl], (((0,), (0,)), ((), ())), preferred_element_type=F32)
            cmat = keep[c] * cmat + upd
            nvec = keep[c] * nvec + jnp.sum(wk[sl], axis=0, keepdims=True)
        c_ref[hh] = cmat
        n_ref[hh] = nvec
        num = w_inter * jnp.concatenate(num_inter, axis=0) + num_intra
        den = w_inter * jnp.concatenate(den_inter, axis=0) + den_intra
        hc = num / jnp.maximum(jnp.abs(den), jnp.exp(-m_row))
        z = z_ref[:, hs]
        o_ref[:, hs] = ((_row_norm(hc, True) * gn_ref[hh] + skip_ref[:, hs] * xc_ref[:, hs])
                        * (z * jax.nn.sigmoid(z)))


def _mlstm_mixer(proj, conv_w, conv_b, w_q, w_k, w_v, w_gates, b_gates, gn_g, skip, bsz, seq):
    t = proj.shape[0]
    inner = proj.shape[1] // 2
    n_heads, dh = gn_g.shape
    tm = MLSTM_PRE_TILE
    n_g = w_gates.shape[1]
    wg = jnp.pad(w_gates, ((0, 0), (0, LANES - n_g))).astype(BF16)
    bg = jnp.pad(b_gates.reshape(1, n_g), ((0, 0), (0, LANES - n_g)))
    tile_spec = pl.BlockSpec((tm, inner), lambda i: (i, 0))
    bd_spec = pl.BlockSpec((inner // LANES, LANES, LANES), lambda i: (0, 0, 0))
    q, k, v, xc, gates = pl.pallas_call(
        functools.partial(_mlstm_pre_kernel, per=seq // tm, k_scale=dh ** -0.5),
        grid=(t // tm,),
        in_specs=[tile_spec,
                  pl.BlockSpec((MLSTM_CONV, inner), lambda i: (0, 0)),
                  pl.BlockSpec((1, inner), lambda i: (0, 0)),
                  bd_spec, bd_spec, bd_spec,
                  pl.BlockSpec((3 * inner, LANES), lambda i: (0, 0)),
                  pl.BlockSpec((1, LANES), lambda i: (0, 0))],
        out_specs=[tile_spec, tile_spec, tile_spec, tile_spec, pl.BlockSpec((tm, LANES), lambda i: (i, 0))],
        out_shape=[jax.ShapeDtypeStruct((t, inner), BF16)] * 4 + [jax.ShapeDtypeStruct((t, LANES), F32)],
        scratch_shapes=[pltpu.VMEM((tm + CONV_PAD, inner), F32)],
        compiler_params=_cparams("arbitrary"),
        name="mlstm_pre",
    )(proj, conv_w, conv_b.reshape(1, inner), _blockdiag_tiles(w_q), _blockdiag_tiles(w_k),
      _blockdiag_tiles(w_v), wg, bg)

    lb = SEQ_BLOCK
    hps = HEADS_PER_STEP
    n_hp = n_heads // hps
    r3 = lambda a: a.reshape(bsz, seq, a.shape[-1])
    head_spec = pl.BlockSpec((None, lb, hps * dh), lambda b, h, s: (b, s, h))
    out = pl.pallas_call(
        functools.partial(_mlstm_rec_kernel, n_heads=n_heads),
        grid=(bsz, n_hp, seq // lb),
        in_specs=[head_spec, head_spec, head_spec,
                  pl.BlockSpec((None, lb, LANES), lambda b, h, s: (b, s, 0)),
                  head_spec,
                  pl.BlockSpec((None, lb, hps * dh), lambda b, h, s: (b, s, n_hp + h)),
                  pl.BlockSpec((hps, 1, dh), lambda b, h, s: (h, 0, 0)),
                  pl.BlockSpec((1, hps * dh), lambda b, h, s: (0, h))],
        out_specs=head_spec,
        out_shape=jax.ShapeDtypeStruct((bsz, seq, inner), F32),
        scratch_shapes=[pltpu.VMEM((hps, dh, dh), F32), pltpu.VMEM((hps, 1, dh), F32),
                        pltpu.VMEM((hps, 1, LANES), F32)],
        compiler_params=_cparams("arbitrary", "arbitrary", "arbitrary"),
        name="mlstm_rec",
    )(r3(q), r3(k), r3(v), r3(gates), r3(xc), r3(proj), gn_g.reshape(n_heads, 1, dh), skip.reshape(1, inner))
    return out.reshape(t, inner)


def kernel(x, c, ada_w, ada_b, norm_mix_g, norm_ffn_g, norm_final_g, ret_w_in, ret_gn_g, ret_w_out, att_w_in, att_rel_bias, att_w_out, gla_w_in, gla_w_gate, gla_b_gate, gla_gn_g, gla_w_out, mlstm_w_in, mlstm_conv_w, mlstm_conv_b, mlstm_w_q, mlstm_w_k, mlstm_w_v, mlstm_w_gates, mlstm_b_gates, mlstm_gn_g, mlstm_skip, mlstm_w_out, router_w, router_b, moe_w1, moe_b1, moe_w2, moe_b2):
    bsz, seq, d = x.shape
    depth, n_exp, _, dh2 = moe_w1.shape
    xt = x.reshape(bsz * seq, d)
    mod = _ada_mod(c, ada_w, ada_b)
    w1_all = moe_w1.reshape(depth * n_exp, d, dh2)
    w2_all = moe_w2.reshape(depth * n_exp, dh2 // 2, d)
    b1_all = moe_b1.reshape(depth * n_exp, 1, dh2)
    b2_all = moe_b2.reshape(depth * n_exp, 1, d)
    for i in range(depth):
        kind, j = i % 4, i // 4
        sh_a, sc_a, g_a, sh_f, sc_f, g_f = jnp.split(mod[i], 6, axis=-1)
        if kind == 0:
            proj = _norm_mm(xt, norm_mix_g[i], sc_a, sh_a, ret_w_in[j].astype(BF16), seq)
            y = _retention_mixer(proj, ret_gn_g[j], bsz, seq)
            w_out = ret_w_out[j]
        elif kind == 1:
            proj = _norm_mm(xt, norm_mix_g[i], sc_a, sh_a, att_w_in[j].astype(BF16), seq)
            y = _attention_mixer(proj, att_rel_bias[j], bsz, seq)
            w_out = att_w_out[j]
        elif kind == 2:
            n_heads, dv = gla_gn_g[j].shape
            w_gla = _gla_proj_weight(gla_w_in[j], n_heads * dv // 2, n_heads * dv).astype(BF16)
            proj = _norm_mm(xt, norm_mix_g[i], sc_a, sh_a, w_gla, seq)
            y = _gla_mixer(proj, gla_w_gate[j], gla_b_gate[j], gla_gn_g[j], bsz, seq)
            w_out = gla_w_out[j]
        else:
            proj = _norm_mm(xt, norm_mix_g[i], sc_a, sh_a, mlstm_w_in[j].astype(BF16), seq)
            y = _mlstm_mixer(proj, mlstm_conv_w[j], mlstm_conv_b[j], mlstm_w_q[j], mlstm_w_k[j], mlstm_w_v[j],
                             mlstm_w_gates[j], mlstm_b_gates[j], mlstm_gn_g[j], mlstm_skip[j], bsz, seq)
            w_out = mlstm_w_out[j]
        xt = _mm_res(y, w_out.astype(BF16), xt, g_a, seq)
        xt = _moe_layer(xt, norm_ffn_g[i], sc_f, sh_f, g_f, router_w[i], router_b[i],
                        w1_all, b1_all, w2_all, b2_all, i, seq)
    return _final_norm(xt, norm_final_g).reshape(bsz, seq, d)
```

```python
import functools

import jax
import jax.numpy as jnp
from jax import lax
from jax.experimental import pallas as pl
from jax.experimental.pallas import tpu as pltpu

F32 = jnp.float32
BF16 = jnp.bfloat16
HI = lax.Precision.HIGHEST

CHUNK = 64
EPS = 1e-6
ROPE_BASE = 10000.0
ATT_LEFT_CHUNKS = 8
ATT_MAX_REL = 256
GLA_GATE_RANK = 16
GLA_GATE_NORM = 16.0
MLSTM_CONV = 4
N_EXPERTS = 32
TOP_K = 4
SWIGLU_LIMIT = 7.0
SWIGLU_ALPHA = 1.702

VMEM_LIMIT_BYTES = 56 * 1024 * 1024
LANES = 128
SUBLANES = 8
MOE_TILE = 512
COMBINE_TILE = 256
COPIES_PER_ITER = 32
SEQ_BLOCK = 256
MLSTM_PRE_TILE = 256
CONV_PAD = 8
ATT_QBLOCK = 2 * CHUNK
ATT_LEFT = ATT_LEFT_CHUNKS * CHUNK
ATT_WINDOW = ATT_LEFT + ATT_QBLOCK


def _cparams(*sem):
    return pltpu.CompilerParams(dimension_semantics=sem, vmem_limit_bytes=VMEM_LIMIT_BYTES)


def _row_norm(x, center):
    if center:
        x = x - jnp.mean(x, axis=-1, keepdims=True)
    return x * lax.rsqrt(jnp.mean(x * x, axis=-1, keepdims=True) + EPS)


def _load_row_tiles(ref, rows):
    return jnp.concatenate([ref[pl.ds(j, rows, stride=SUBLANES), :] for j in range(SUBLANES)], axis=1)


def _store_row_tiles(ref, val):
    rows = val.shape[0]
    for j in range(SUBLANES):
        ref[pl.ds(j, rows, stride=SUBLANES), :] = val[:, j * LANES:(j + 1) * LANES]


def _ada_kernel(c_ref, w_ref, b_ref, o_ref):
    c = c_ref[...]
    cond = c * jax.nn.sigmoid(c)
    o_ref[0] = jnp.dot(cond, w_ref[0], preferred_element_type=F32, precision=HI) + b_ref[0]


def _ada_mod(c, ada_w, ada_b):
    depth, d, n = ada_w.shape
    b = c.shape[0]
    tn = 1536
    return pl.pallas_call(
        _ada_kernel,
        grid=(depth, n // tn),
        in_specs=[pl.BlockSpec((b, d), lambda l, j: (0, 0)),
                  pl.BlockSpec((1, d, tn), lambda l, j: (l, 0, j)),
                  pl.BlockSpec((1, 1, tn), lambda l, j: (l, 0, j))],
        out_specs=pl.BlockSpec((1, b, tn), lambda l, j: (l, 0, j)),
        out_shape=jax.ShapeDtypeStruct((depth, b, n), F32),
        compiler_params=_cparams("arbitrary", "arbitrary"),
        name="ada_mod",
    )(c, ada_w, ada_b.reshape(depth, 1, n))


def _modulated_norm(x, g, sc, sh):
    ms = jnp.mean(x * x, axis=-1, keepdims=True)
    return (x * lax.rsqrt(ms + EPS) * g) * (1.0 + sc) + sh


def _norm_mm_kernel(x_ref, g_ref, sc_ref, sh_ref, w_ref, o_ref, h_scr):
    @pl.when(pl.program_id(1) == 0)
    def _():
        h = _modulated_norm(x_ref[...], g_ref[...], sc_ref[0], sh_ref[0])
        h_scr[...] = h.astype(BF16)

    o_ref[...] = jnp.dot(h_scr[...], w_ref[...], preferred_element_type=F32)


NORM_MM_MAX_COLS = 2048


def _col_tile(n):
    best = LANES
    for m in range(1, n // LANES + 1):
        if (n // LANES) % m == 0 and m * LANES <= NORM_MM_MAX_COLS:
            best = m * LANES
    return best


def _norm_mm(x, g, sc, sh, w, seq, tm=1024):
    t, d = x.shape
    n = w.shape[1]
    tn = _col_tile(n)
    per = seq // tm
    bsz = sc.shape[0]
    return pl.pallas_call(
        _norm_mm_kernel,
        grid=(t // tm, n // tn),
        in_specs=[pl.BlockSpec((tm, d), lambda i, j: (i, 0)),
                  pl.BlockSpec((1, d), lambda i, j: (0, 0)),
                  pl.BlockSpec((1, 1, d), lambda i, j: (i // per, 0, 0)),
                  pl.BlockSpec((1, 1, d), lambda i, j: (i // per, 0, 0)),
                  pl.BlockSpec((d, tn), lambda i, j: (0, j))],
        out_specs=pl.BlockSpec((tm, tn), lambda i, j: (i, j)),
        out_shape=jax.ShapeDtypeStruct((t, n), F32),
        scratch_shapes=[pltpu.VMEM((tm, d), BF16)],
        compiler_params=_cparams("arbitrary", "arbitrary"),
        name="norm_mm",
    )(x, g.reshape(1, d), sc.reshape(bsz, 1, d), sh.reshape(bsz, 1, d), w)


def _mm_res_kernel(y_ref, w_ref, x_ref, gate_ref, o_ref):
    acc = jnp.dot(y_ref[...].astype(BF16), w_ref[...], preferred_element_type=F32)
    o_ref[...] = x_ref[...] + gate_ref[0] * acc


def _mm_res(y, w, x, gate, seq, tm=512):
    t, k = y.shape
    d = w.shape[1]
    per = seq // tm
    bsz = gate.shape[0]
    return pl.pallas_call(
        _mm_res_kernel,
        grid=(t // tm,),
        in_specs=[pl.BlockSpec((tm, k), lambda i: (i, 0)),
                  pl.BlockSpec((k, d), lambda i: (0, 0)),
                  pl.BlockSpec((tm, d), lambda i: (i, 0)),
                  pl.BlockSpec((1, 1, d), lambda i: (i // per, 0, 0))],
        out_specs=pl.BlockSpec((tm, d), lambda i: (i, 0)),
        out_shape=jax.ShapeDtypeStruct((t, d), F32),
        compiler_params=_cparams("arbitrary"),
        name="mm_res",
    )(y, w, x, gate.reshape(bsz, 1, d))


def _final_norm_kernel(x_ref, g_ref, o_ref):
    x = x_ref[...]
    ms = jnp.mean(x * x, axis=-1, keepdims=True)
    o_ref[...] = x * lax.rsqrt(ms + EPS) * g_ref[...]


def _final_norm(x, g, tm=1024):
    t, d = x.shape
    return pl.pallas_call(
        _final_norm_kernel,
        grid=(t // tm,),
        in_specs=[pl.BlockSpec((tm, d), lambda i: (i, 0)),
                  pl.BlockSpec((1, d), lambda i: (0, 0))],
        out_specs=pl.BlockSpec((tm, d), lambda i: (i, 0)),
        out_shape=jax.ShapeDtypeStruct((t, d), F32),
        compiler_params=_cparams("arbitrary"),
        name="final_norm",
    )(x, g.reshape(1, d))


def _norm_router_kernel(x_ref, g_ref, sc_ref, sh_ref, rw_ref, rb_ref, h_ref, route_ref, *, n_experts):
    h = _modulated_norm(x_ref[...], g_ref[...], sc_ref[0], sh_ref[0])
    _store_row_tiles(h_ref, h)
    logits = jnp.dot(h, rw_ref[...], preferred_element_type=F32, precision=HI) + rb_ref[...]
    lane = lax.broadcasted_iota(jnp.int32, logits.shape, 1)
    rem = jnp.where(lane < n_experts, logits, -jnp.inf)
    vals, idxs = [], []
    for _ in range(TOP_K):
        m = jnp.max(rem, axis=1, keepdims=True)
        idx = jnp.min(jnp.where(rem == m, lane, LANES), axis=1, keepdims=True)
        vals.append(m)
        idxs.append(idx)
        rem = jnp.where(lane == idx, -jnp.inf, rem)
    ex = [jnp.exp(v - vals[0]) for v in vals]
    inv = 1.0 / sum(ex[1:], ex[0])
    out = jnp.zeros(logits.shape, F32)
    for k in range(TOP_K):
        out = jnp.where(lane == k, ex[k] * inv, out)
        out = jnp.where(lane == TOP_K + k, idxs[k].astype(F32), out)
    route_ref[...] = out


def _norm_router(x, g, sc, sh, rw, rb, seq, tm=512):
    t, d = x.shape
    per = seq // tm
    bsz = sc.shape[0]
    e = rw.shape[1]
    rw_p = jnp.pad(rw, ((0, 0), (0, LANES - e)))
    rb_p = jnp.pad(rb.reshape(1, e), ((0, 0), (0, LANES - e)))
    return pl.pallas_call(
        functools.partial(_norm_router_kernel, n_experts=e),
        grid=(t // tm,),
        in_specs=[pl.BlockSpec((tm, d), lambda i: (i, 0)),
                  pl.BlockSpec((1, d), lambda i: (0, 0)),
                  pl.BlockSpec((1, 1, d), lambda i: (i // per, 0, 0)),
                  pl.BlockSpec((1, 1, d), lambda i: (i // per, 0, 0)),
                  pl.BlockSpec((d, LANES), lambda i: (0, 0)),
                  pl.BlockSpec((1, LANES), lambda i: (0, 0))],
        out_specs=[pl.BlockSpec((tm * SUBLANES, LANES), lambda i: (i, 0)),
                   pl.BlockSpec((tm, LANES), lambda i: (i, 0))],
        out_shape=[jax.ShapeDtypeStruct((t * SUBLANES, LANES), F32),
                   jax.ShapeDtypeStruct((t, LANES), F32)],
        compiler_params=_cparams("arbitrary"),
        name="norm_router",
    )(x, g.reshape(1, d), sc.reshape(bsz, 1, d), sh.reshape(bsz, 1, d), rw_p, rb_p)


def _row_tile_copy(src_hbm, src_row8, dst_vmem, dst_row, sem):
    return pltpu.make_async_copy(
        src_hbm.at[pl.ds(pl.multiple_of(src_row8, SUBLANES), SUBLANES)],
        dst_vmem.at[pl.ds(pl.multiple_of(dst_row * SUBLANES, SUBLANES), SUBLANES)], sem)


def _expert_kernel(te_ref, nt_ref, idx_cur_ref, idx_nxt_ref, h_hbm, w1_ref, b1_ref, w2_ref, b2_ref,
                   o_ref, xbuf, w1_bf, w2_bf, sem):
    i = pl.program_id(0)
    n_used = nt_ref[0]
    tm = xbuf.shape[1] // SUBLANES
    slot = i % 2

    @pl.when(jnp.logical_or(i == 0, te_ref[i] != te_ref[jnp.maximum(i - 1, 0)]))
    def _():
        w1_bf[...] = w1_ref[0].astype(BF16)
        w2_bf[...] = w2_ref[0].astype(BF16)

    def issue(idx_ref, s):
        def body(g, carry):
            for u in range(COPIES_PER_ITER):
                r = g * COPIES_PER_ITER + u
                _row_tile_copy(h_hbm, idx_ref[0, 0, r], xbuf.at[s], r, sem.at[s]).start(priority=u % 2)
            return carry
        lax.fori_loop(0, tm // COPIES_PER_ITER, body, 0)

    @pl.when(jnp.logical_and(i == 0, n_used > 0))
    def _():
        issue(idx_cur_ref, 0)

    @pl.when(i + 1 < n_used)
    def _():
        issue(idx_nxt_ref, 1 - slot)

    @pl.when(i < n_used)
    def _():
        pltpu.make_async_copy(h_hbm.at[pl.ds(0, tm * SUBLANES)], xbuf.at[slot], sem.at[slot]).wait()
        x = _load_row_tiles(xbuf.at[slot], tm).astype(BF16)
        hid = jnp.dot(x, w1_bf[...], preferred_element_type=F32) + b1_ref[0]
        de = hid.shape[1] // 2
        gate = jnp.minimum(hid[:, :de], SWIGLU_LIMIT)
        lin = jnp.clip(hid[:, de:], -SWIGLU_LIMIT, SWIGLU_LIMIT)
        act = gate * jax.nn.sigmoid(SWIGLU_ALPHA * gate) * (lin + 1.0)
        y = jnp.dot(act.astype(BF16), w2_bf[...], preferred_element_type=F32) + b2_ref[0]
        _store_row_tiles(o_ref, y)

    @pl.when(i >= n_used)
    def _():
        o_ref[...] = jnp.zeros_like(o_ref)


def _expert_ffn(h_tiles, src_row8, tile_expert, n_tiles_used, w1, b1, w2, b2, layer):
    p = src_row8.shape[0]
    tm = MOE_TILE
    nt = p // tm
    _, d, dh = w1.shape
    idx3 = src_row8.reshape(nt, 1, tm)
    base = layer * N_EXPERTS
    grid_spec = pltpu.PrefetchScalarGridSpec(
        num_scalar_prefetch=2,
        grid=(nt,),
        in_specs=[
            pl.BlockSpec((1, 1, tm), lambda i, te, nu: (i, 0, 0), memory_space=pltpu.SMEM),
            pl.BlockSpec((1, 1, tm), lambda i, te, nu: (jnp.minimum(i + 1, nt - 1), 0, 0),
                         memory_space=pltpu.SMEM),
            pl.BlockSpec(memory_space=pl.ANY),
            pl.BlockSpec((1, d, dh), lambda i, te, nu: (base + te[i], 0, 0)),
            pl.BlockSpec((1, 1, dh), lambda i, te, nu: (base + te[i], 0, 0)),
            pl.BlockSpec((1, dh // 2, d), lambda i, te, nu: (base + te[i], 0, 0)),
            pl.BlockSpec((1, 1, d), lambda i, te, nu: (base + te[i], 0, 0)),
        ],
        out_specs=pl.BlockSpec((tm * SUBLANES, LANES), lambda i, te, nu: (i, 0)),
        scratch_shapes=[pltpu.VMEM((2, tm * SUBLANES, LANES), F32), pltpu.VMEM((d, dh), BF16),
                        pltpu.VMEM((dh // 2, d), BF16), pltpu.SemaphoreType.DMA((2,))],
    )
    return pl.pallas_call(
        _expert_kernel,
        grid_spec=grid_spec,
        out_shape=jax.ShapeDtypeStruct((p * SUBLANES, LANES), F32),
        compiler_params=_cparams("arbitrary"),
        name="expert_ffn",
    )(tile_expert, n_tiles_used, idx3, idx3, h_tiles, w1, b1, w2, b2)


def _combine_kernel(idx_cur_ref, idx_nxt_ref, y_hbm, x_ref, w_ref, gate_ref, o_ref, ybuf, sem):
    i = pl.program_id(0)
    n = pl.num_programs(0)
    tc = x_ref.shape[0]
    slot = i % 2

    def issue(idx_ref, s):
        rows_per_iter = COPIES_PER_ITER // TOP_K

        def body(g, carry):
            for u in range(rows_per_iter):
                r = g * rows_per_iter + u
                for k in range(TOP_K):
                    _row_tile_copy(y_hbm, idx_ref[0, 0, r * TOP_K + k], ybuf.at[s, k], r,
                                   sem.at[s]).start(priority=k % 2)
            return carry
        lax.fori_loop(0, tc // rows_per_iter, body, 0)

    @pl.when(i == 0)
    def _():
        issue(idx_cur_ref, 0)

    @pl.when(i + 1 < n)
    def _():
        issue(idx_nxt_ref, 1 - slot)

    for k in range(TOP_K):
        pltpu.make_async_copy(y_hbm.at[pl.ds(0, tc * SUBLANES)], ybuf.at[slot, k], sem.at[slot]).wait()
    w = w_ref[...]
    acc = w[:, 0:1] * _load_row_tiles(ybuf.at[slot, 0], tc)
    for k in range(1, TOP_K):
        acc = acc + w[:, k:k + 1] * _load_row_tiles(ybuf.at[slot, k], tc)
    o_ref[...] = x_ref[...] + gate_ref[0] * acc


def _moe_combine(y_tiles, slot8_of_pair, top_w, x, gate, seq):
    t, d = x.shape
    tc = COMBINE_TILE
    nt = t // tc
    per = seq // tc
    bsz = gate.shape[0]
    idx3 = slot8_of_pair.reshape(nt, 1, tc * TOP_K)
    return pl.pallas_call(
        _combine_kernel,
        grid=(nt,),
        in_specs=[
            pl.BlockSpec((1, 1, tc * TOP_K), lambda i: (i, 0, 0), memory_space=pltpu.SMEM),
            pl.BlockSpec((1, 1, tc * TOP_K), lambda i: (jnp.minimum(i + 1, nt - 1), 0, 0),
                         memory_space=pltpu.SMEM),
            pl.BlockSpec(memory_space=pl.ANY),
            pl.BlockSpec((tc, d), lambda i: (i, 0)),
            pl.BlockSpec((tc, TOP_K), lambda i: (i, 0)),
            pl.BlockSpec((1, 1, d), lambda i: (i // per, 0, 0)),
        ],
        out_specs=pl.BlockSpec((tc, d), lambda i: (i, 0)),
        out_shape=jax.ShapeDtypeStruct((t, d), F32),
        scratch_shapes=[pltpu.VMEM((2, TOP_K, tc * SUBLANES, LANES), F32), pltpu.SemaphoreType.DMA((2,))],
        compiler_params=_cparams("arbitrary"),
        name="moe_combine",
    )(idx3, idx3, y_tiles, x, top_w, gate.reshape(bsz, 1, d))


def _route(route):
    t = route.shape[0]
    top_w = route[:, :TOP_K]
    flat_e = route[:, TOP_K:2 * TOP_K].astype(jnp.int32).reshape(-1)
    onehot = (flat_e[:, None] == jnp.arange(N_EXPERTS, dtype=jnp.int32)[None, :]).astype(jnp.int32)
    csum = jnp.cumsum(onehot, axis=0)
    counts = csum[-1]
    padded = ((counts + MOE_TILE - 1) // MOE_TILE) * MOE_TILE
    ends = jnp.cumsum(padded)
    starts = ends - padded
    slot_of_pair = jnp.sum(onehot * (csum - 1 + starts[None, :]), axis=1)
    n_fill = N_EXPERTS * MOE_TILE
    fill_end = jnp.cumsum(padded - counts)
    fill_key = jnp.sum((jnp.arange(n_fill, dtype=jnp.int32)[:, None] >= fill_end[None, :]).astype(jnp.int32), axis=1)
    n_pairs = t * TOP_K
    pos_bits = (n_pairs + n_fill - 1).bit_length()
    assert (N_EXPERTS + 1) << pos_bits < 2 ** 31
    keys = (jnp.concatenate([flat_e, fill_key]) << pos_bits) + jnp.arange(n_pairs + n_fill, dtype=jnp.int32)
    entry = lax.sort(keys) & ((1 << pos_bits) - 1)
    src_tok = jnp.where(entry < n_pairs, entry // TOP_K, 0)
    n_tiles = (n_pairs + n_fill) // MOE_TILE
    tile_start = jnp.arange(n_tiles, dtype=jnp.int32) * MOE_TILE
    tile_expert = jnp.sum((tile_start[:, None] >= ends[None, :]).astype(jnp.int32), axis=1)
    n_used = (ends[-1] // MOE_TILE).astype(jnp.int32)
    last_e = jnp.sum(jnp.where(jnp.arange(n_tiles) == n_used - 1, tile_expert, 0))
    tile_expert = jnp.where(jnp.arange(n_tiles) < n_used, tile_expert, last_e).astype(jnp.int32)
    return (top_w, (slot_of_pair * SUBLANES).astype(jnp.int32), (src_tok * SUBLANES).astype(jnp.int32),
            tile_expert, n_used.reshape(1))


def _moe_layer(x, g, sc, sh, gate, rw, rb, w1, b1, w2, b2, layer, seq):
    h_tiles, route = _norm_router(x, g, sc, sh, rw, rb, seq)
    top_w, slot8_of_pair, src_row8, tile_expert, n_used = _route(route)
    y_tiles = _expert_ffn(h_tiles, src_row8, tile_expert, n_used, w1, b1, w2, b2, layer)
    return _moe_combine(y_tiles, slot8_of_pair, top_w, x, gate, seq)


def _chunk_pos(shape):
    return lax.broadcasted_iota(jnp.int32, shape, 0) & (CHUNK - 1)


def _inchunk_cumsum(x):
    pos = _chunk_pos(x.shape)
    sh = 1
    while sh < CHUNK:
        x = x + jnp.where(pos >= sh, pltpu.roll(x, sh, axis=0), 0.0)
        sh *= 2
    return x


def _chunk_last_rows(b):
    rows, d = b.shape
    b3 = b.reshape(rows // CHUNK, CHUNK, d)
    return jnp.broadcast_to(b3[:, CHUNK - 1:CHUNK, :], b3.shape).reshape(rows, d)


def _chunk_causal_mask(rows):
    r = lax.broadcasted_iota(jnp.int32, (rows, rows), 0)
    c = lax.broadcasted_iota(jnp.int32, (rows, rows), 1)
    return jnp.logical_and(r >= c, (r // CHUNK) == (c // CHUNK))


def _linattn_block(q, k, v, b, b_rem, chunk_decay, state_ref):
    rows = q.shape[0]
    q_in = (q * jnp.exp(b)).astype(BF16)
    k_in = (k * jnp.exp(-b)).astype(BF16)
    k_st = (k * jnp.exp(b_rem)).astype(BF16)
    vb = v.astype(BF16)
    scores = lax.dot_general(q_in, k_in, (((1,), (1,)), ((), ())), preferred_element_type=F32)
    scores = jnp.where(_chunk_causal_mask(rows), scores, 0.0)
    o_intra = jnp.dot(scores.astype(BF16), vb, preferred_element_type=F32)
    state = state_ref[...]
    o_inter = []
    for c in range(rows // CHUNK):
        sl = slice(c * CHUNK, (c + 1) * CHUNK)
        o_inter.append(jnp.dot(q_in[sl], state.astype(BF16), preferred_element_type=F32))
        upd = lax.dot_general(k_st[sl], vb[sl], (((0,), (0,)), ((), ())), preferred_element_type=F32)
        state = chunk_decay(c) * state + upd
    state_ref[...] = state
    return o_intra + jnp.concatenate(o_inter, axis=0)


HEADS_PER_STEP = 4


def _ret_kernel(q_ref, k_ref, v_ref, g_ref, cos_ref, sin_ref, lg_ref, gn_ref, o_ref, state_ref, *, k_scale):
    @pl.when(pl.program_id(2) == 0)
    def _():
        state_ref[...] = jnp.zeros_like(state_ref)

    rows = q_ref.shape[0]
    dk = q_ref.shape[1] // HEADS_PER_STEP
    dv = v_ref.shape[1] // HEADS_PER_STEP
    half = dk // 2
    cos, sin = cos_ref[...], sin_ref[...]

    def rope(t):
        t1, t2 = t[:, :half], t[:, half:]
        return jnp.concatenate([t1 * cos - t2 * sin, t1 * sin + t2 * cos], axis=1)

    pos = _chunk_pos((rows, dk)).astype(F32)
    for hh in range(HEADS_PER_STEP):
        ks, vs = slice(hh * dk, (hh + 1) * dk), slice(hh * dv, (hh + 1) * dv)
        lg = lg_ref[hh]
        b = (pos + 1.0) * lg
        b_rem = (CHUNK - 1.0 - pos) * lg
        decay = jnp.exp(CHUNK * lg[:, 0:1])
        q = rope(q_ref[:, ks])
        k = rope(k_ref[:, ks]) * k_scale
        o = _linattn_block(q, k, v_ref[:, vs], b, b_rem, lambda c: decay, state_ref.at[hh])
        g = g_ref[:, vs]
        o_ref[:, vs] = _row_norm(o, True) * gn_ref[hh] * (g * jax.nn.sigmoid(g))


def _retention_mixer(proj, gn_g, bsz, seq):
    n_heads, dv = gn_g.shape
    dk = dv // 2
    lb = SEQ_BLOCK
    hps = HEADS_PER_STEP
    n_hp = n_heads // hps
    p3 = proj.reshape(bsz, seq, proj.shape[-1])
    half = dk // 2
    inv = ROPE_BASE ** (-jnp.arange(half, dtype=F32) / half)
    ang = jnp.arange(seq, dtype=F32)[:, None] * inv[None, :]
    cos, sin = jnp.cos(ang), jnp.sin(ang)
    log_gamma = jnp.log1p(-jnp.exp2(-5.0 - jnp.arange(n_heads, dtype=F32)))
    lg = jnp.broadcast_to(log_gamma[:, None, None], (n_heads, 1, dk))
    out = pl.pallas_call(
        functools.partial(_ret_kernel, k_scale=dk ** -0.5),
        grid=(bsz, n_hp, seq // lb),
        in_specs=[pl.BlockSpec((None, lb, hps * dk), lambda b, h, s: (b, s, h)),
                  pl.BlockSpec((None, lb, hps * dk), lambda b, h, s: (b, s, n_hp + h)),
                  pl.BlockSpec((None, lb, hps * dv), lambda b, h, s: (b, s, n_hp + h)),
                  pl.BlockSpec((None, lb, hps * dv), lambda b, h, s: (b, s, 2 * n_hp + h)),
                  pl.BlockSpec((lb, half), lambda b, h, s: (s, 0)),
                  pl.BlockSpec((lb, half), lambda b, h, s: (s, 0)),
                  pl.BlockSpec((hps, 1, dk), lambda b, h, s: (h, 0, 0)),
                  pl.BlockSpec((hps, 1, dv), lambda b, h, s: (h, 0, 0))],
        out_specs=pl.BlockSpec((None, lb, hps * dv), lambda b, h, s: (b, s, h)),
        out_shape=jax.ShapeDtypeStruct((bsz, seq, n_heads * dv), F32),
        scratch_shapes=[pltpu.VMEM((hps, dk, dv), F32)],
        compiler_params=_cparams("arbitrary", "arbitrary", "arbitrary"),
        name="retention",
    )(p3, p3, p3, p3, cos, sin, lg, gn_g.reshape(n_heads, 1, dv))
    return out.reshape(bsz * seq, n_heads * dv)


def _gla_kernel(q_ref, k_ref, v_ref, r_ref, glow_ref, wg_ref, bg_ref, gn_ref, o_ref, state_ref, *, q_scale):
    @pl.when(pl.program_id(2) == 0)
    def _():
        state_ref[...] = jnp.zeros_like(state_ref)

    dk = q_ref.shape[1] // HEADS_PER_STEP
    dv = v_ref.shape[1] // HEADS_PER_STEP

    rows = q_ref.shape[0]
    z = jnp.dot(glow_ref[...], wg_ref[...], precision=HI, preferred_element_type=F32) + bg_ref[...]
    la_all = jax.nn.log_sigmoid(z) / GLA_GATE_NORM
    chunk_sel = (lax.broadcasted_iota(jnp.int32, (rows, LANES), 0) // CHUNK
                 == lax.broadcasted_iota(jnp.int32, (rows, LANES), 1)).astype(F32)
    for hh in range(HEADS_PER_STEP):
        ks, vs = slice(hh * dk, (hh + 1) * dk), slice(hh * dv, (hh + 1) * dv)
        la = la_all[:, ks]
        b = _inchunk_cumsum(la)
        b_rem = _chunk_last_rows(b) - b
        decay = jnp.exp(lax.dot_general(la, chunk_sel, (((0,), (0,)), ((), ())),
                                        precision=HI, preferred_element_type=F32))
        o = _linattn_block(q_ref[:, ks] * q_scale, k_ref[:, ks], v_ref[:, vs], b, b_rem,
                           lambda c: decay[:, c:c + 1], state_ref.at[hh])
        r = r_ref[:, vs]
        o_ref[:, vs] = _row_norm(o, False) * gn_ref[hh] * (r * jax.nn.sigmoid(r))


def _gla_proj_weight(w_in, dq, dvv):
    q_k_v = w_in[:, :2 * dq + dvv]
    g_low = w_in[:, 2 * dq + dvv:2 * dq + dvv + GLA_GATE_RANK]
    r = w_in[:, 2 * dq + dvv + GLA_GATE_RANK:]
    pad = jnp.zeros((w_in.shape[0], LANES - GLA_GATE_RANK), w_in.dtype)
    return jnp.concatenate([q_k_v, r, g_low, pad], axis=1)


def _gla_mixer(proj, w_gate, b_gate, gn_g, bsz, seq):
    n_heads, dv = gn_g.shape
    dk = dv // 2
    dq, dvv = n_heads * dk, n_heads * dv
    lb = SEQ_BLOCK
    hps = HEADS_PER_STEP
    n_hp = n_heads // hps
    p3 = proj.reshape(bsz, seq, proj.shape[-1])
    wg = jnp.pad(w_gate, ((0, LANES - GLA_GATE_RANK), (0, 0)))
    out = pl.pallas_call(
        functools.partial(_gla_kernel, q_scale=dk ** -0.5),
        grid=(bsz, n_hp, seq // lb),
        in_specs=[pl.BlockSpec((None, lb, hps * dk), lambda b, h, s: (b, s, h)),
                  pl.BlockSpec((None, lb, hps * dk), lambda b, h, s: (b, s, n_hp + h)),
                  pl.BlockSpec((None, lb, hps * dv), lambda b, h, s: (b, s, n_hp + h)),
                  pl.BlockSpec((None, lb, hps * dv), lambda b, h, s: (b, s, 2 * n_hp + h)),
                  pl.BlockSpec((None, lb, LANES), lambda b, h, s: (b, s, (2 * dq + 2 * dvv) // LANES)),
                  pl.BlockSpec((LANES, hps * dk), lambda b, h, s: (0, h)),
                  pl.BlockSpec((1, hps * dk), lambda b, h, s: (0, h)),
                  pl.BlockSpec((hps, 1, dv), lambda b, h, s: (h, 0, 0))],
        out_specs=pl.BlockSpec((None, lb, hps * dv), lambda b, h, s: (b, s, h)),
        out_shape=jax.ShapeDtypeStruct((bsz, seq, dvv), F32),
        scratch_shapes=[pltpu.VMEM((hps, dk, dv), F32)],
        compiler_params=_cparams("arbitrary", "arbitrary", "arbitrary"),
        name="gla",
    )(p3, p3, p3, p3, p3, wg, b_gate.reshape(1, dq), gn_g.reshape(n_heads, 1, dv))
    return out.reshape(bsz * seq, dvv)


def _att_kernel(q_ref, k_ref, v_ref, bias_ref, o_ref, kpad, vpad, *, scale, dh):
    seq, width = q_ref.shape
    n_pair = width // dh
    kpad[0:ATT_LEFT, :] = jnp.zeros((ATT_LEFT, width), F32)
    vpad[0:ATT_LEFT, :] = jnp.zeros((ATT_LEFT, width), F32)
    kpad[ATT_LEFT:ATT_LEFT + seq, :] = k_ref[...]
    vpad[ATT_LEFT:ATT_LEFT + seq, :] = v_ref[...]
    lane = lax.broadcasted_iota(jnp.int32, (ATT_QBLOCK, width), 1)
    jcol = lax.broadcasted_iota(jnp.int32, (n_pair * ATT_QBLOCK, ATT_WINDOW), 1)
    in_head = [jnp.logical_and(lane >= h * dh, lane < (h + 1) * dh) for h in range(n_pair)]

    def block(start, left_edge):
        q = q_ref[pl.ds(start, ATT_QBLOCK), :] * scale
        kb = kpad[pl.ds(start, ATT_WINDOW), :].astype(BF16)
        vb = vpad[pl.ds(start, ATT_WINDOW), :].astype(BF16)
        q2 = jnp.concatenate([jnp.where(m, q, 0.0) for m in in_head], axis=0).astype(BF16)
        s = lax.dot_general(q2, kb, (((1,), (1,)), ((), ())), preferred_element_type=F32) + bias_ref[...]
        if left_edge:
            s = jnp.where(jcol >= ATT_LEFT - start, s, -jnp.inf)
        p = jnp.exp(s - jnp.max(s, axis=-1, keepdims=True))
        o2 = jnp.dot(p.astype(BF16), vb, preferred_element_type=F32) / jnp.sum(p, axis=-1, keepdims=True)
        out = o2[0:ATT_QBLOCK, :]
        for h in range(1, n_pair):
            out = jnp.where(in_head[h], o2[h * ATT_QBLOCK:(h + 1) * ATT_QBLOCK, :], out)
        o_ref[pl.ds(start, ATT_QBLOCK), :] = out

    n_edge = ATT_LEFT // ATT_QBLOCK
    for c in range(n_edge):
        block(c * ATT_QBLOCK, True)

    def body(c, carry):
        block(pl.multiple_of(c * ATT_QBLOCK, ATT_QBLOCK), False)
        return carry

    lax.fori_loop(n_edge, seq // ATT_QBLOCK, body, 0, unroll=4)


def _attention_mixer(proj, rel_bias, bsz, seq):
    d = proj.shape[-1] // 3
    n_heads = rel_bias.shape[0]
    dh = d // n_heads
    n_pair = LANES // dh
    groups = n_heads // n_pair
    qo = jnp.arange(ATT_QBLOCK)[:, None]
    kj = jnp.arange(ATT_WINDOW)[None, :]
    lo = (qo // CHUNK) * CHUNK
    in_window = jnp.logical_and(kj >= lo, kj < lo + ATT_LEFT + CHUNK)
    d_max = ATT_QBLOCK - 1 + ATT_LEFT
    d_min = -(ATT_WINDOW - 1 - ATT_LEFT)
    n_rel = rel_bias.shape[1]
    ext = jnp.concatenate(
        [jnp.broadcast_to(rel_bias[:, :1], (n_heads, -(CHUNK - 1) - d_min)), rel_bias,
         jnp.broadcast_to(rel_bias[:, n_rel - 1:], (n_heads, d_max - ATT_MAX_REL))], axis=1)
    rows = [ext[:, q_ + ATT_LEFT - (ATT_WINDOW - 1) - d_min:q_ + ATT_LEFT - d_min + 1] for q_ in range(ATT_QBLOCK)]
    table = jnp.flip(jnp.stack(rows, axis=1), axis=2).astype(F32)
    bias = jnp.where(in_window[None], table, -jnp.inf)
    bias = bias.reshape(groups, n_pair * ATT_QBLOCK, ATT_WINDOW)
    p3 = proj.reshape(bsz, seq, 3 * d)
    out = pl.pallas_call(
        functools.partial(_att_kernel, scale=dh ** -0.5, dh=dh),
        grid=(bsz, groups),
        in_specs=[pl.BlockSpec((None, seq, LANES), lambda b, g: (b, 0, g)),
                  pl.BlockSpec((None, seq, LANES), lambda b, g: (b, 0, groups + g)),
                  pl.BlockSpec((None, seq, LANES), lambda b, g: (b, 0, 2 * groups + g)),
                  pl.BlockSpec((None, n_pair * ATT_QBLOCK, ATT_WINDOW), lambda b, g: (g, 0, 0))],
        out_specs=pl.BlockSpec((None, seq, LANES), lambda b, g: (b, 0, g)),
        out_shape=jax.ShapeDtypeStruct((bsz, seq, d), F32),
        scratch_shapes=[pltpu.VMEM((ATT_LEFT + seq, LANES), F32), pltpu.VMEM((ATT_LEFT + seq, LANES), F32)],
        compiler_params=_cparams("arbitrary", "arbitrary"),
        name="chunk_attention",
    )(p3, p3, p3, bias)
    return out.reshape(bsz * seq, d)


def _mlstm_pre_kernel(xm_ref, cw_ref, cb_ref, wq_ref, wk_ref, wv_ref, wg_ref, bg_ref,
                      q_ref, k_ref, v_ref, xc_ref, g_ref, xpad, *, per, k_scale):
    i = pl.program_id(0)
    tm, inner = xm_ref.shape

    @pl.when(i % per == 0)
    def _():
        xpad[0:CONV_PAD, :] = jnp.zeros((CONV_PAD, inner), F32)

    @pl.when(i % per != 0)
    def _():
        xpad[0:CONV_PAD, :] = xpad[tm:tm + CONV_PAD, :]

    xm = xm_ref[...]
    xpad[CONV_PAD:CONV_PAD + tm, :] = xm
    acc = jnp.broadcast_to(cb_ref[...], (tm, inner))
    for j in range(MLSTM_CONV):
        off = CONV_PAD - (MLSTM_CONV - 1) + j
        acc = acc + cw_ref[j:j + 1, :] * xpad[off:off + tm, :]
    xc = acc * jax.nn.sigmoid(acc)
    xc_ref[...] = xc.astype(BF16)

    def blockdiag(t, w_ref):
        tb = t.astype(BF16)
        return jnp.concatenate(
            [jnp.dot(tb[:, g * LANES:(g + 1) * LANES], w_ref[g], preferred_element_type=F32)
             for g in range(inner // LANES)], axis=1)

    q = blockdiag(xc, wq_ref)
    k = blockdiag(xc, wk_ref) * k_scale
    v = blockdiag(xm, wv_ref)
    qb, kb, vb = q.astype(BF16), k.astype(BF16), v.astype(BF16)
    q_ref[...] = qb
    k_ref[...] = kb
    v_ref[...] = vb
    g_ref[...] = (jnp.dot(qb, wg_ref[0:inner, :], preferred_element_type=F32)
                  + jnp.dot(kb, wg_ref[inner:2 * inner, :], preferred_element_type=F32)
                  + jnp.dot(vb, wg_ref[2 * inner:3 * inner, :], preferred_element_type=F32) + bg_ref[...])


def _blockdiag_tiles(w):
    nb, c, _ = w.shape
    per = LANES // c
    wt = w.reshape(nb // per, per, c, c)
    t = jnp.einsum('gpcd,pq->gpcqd', wt, jnp.eye(per, dtype=w.dtype))
    return t.reshape(nb // per, LANES, LANES).astype(BF16)


def _mlstm_rec_kernel(q_ref, k_ref, v_ref, g_ref, xc_ref, z_ref, gn_ref, skip_ref, o_ref,
                      c_ref, n_ref, m_ref, *, n_heads):
    head0 = pl.program_id(1) * HEADS_PER_STEP
    dh = q_ref.shape[1] // HEADS_PER_STEP

    @pl.when(pl.program_id(2) == 0)
    def _():
        c_ref[...] = jnp.zeros_like(c_ref)
        n_ref[...] = jnp.zeros_like(n_ref)
        m_ref[...] = jnp.zeros_like(m_ref)

    rows = q_ref.shape[0]
    n_chunks = rows // CHUNK
    mask = _chunk_causal_mask(rows)
    lane = lax.broadcasted_iota(jnp.int32, (rows, LANES), 1)
    sub = lax.broadcasted_iota(jnp.int32, (LANES, rows), 0)

    def col(t, idx):
        return jnp.sum(jnp.where(lane == idx, t, 0.0), axis=1, keepdims=True)

    def row(t_t, idx):
        return jnp.sum(jnp.where(sub == idx, t_t, 0.0), axis=0, keepdims=True)

    def per_chunk(vals):
        return jnp.concatenate([jnp.broadcast_to(x, (CHUNK, 1)) for x in vals], axis=0)

    gates = g_ref[...]
    bcum = _inchunk_cumsum(jax.nn.log_sigmoid(gates))
    gates_t, bcum_t = gates.T, bcum.T
    for hh in range(HEADS_PER_STEP):
        hs = slice(hh * dh, (hh + 1) * dh)
        i_idx, f_idx = head0 + hh, n_heads + head0 + hh
        i_col, b_col = col(gates, i_idx), col(bcum, f_idx)
        b_last_rows = _chunk_last_rows(b_col)
        log_w = b_last_rows - b_col + i_col
        m_start, m_next, keep = [], [], []
        m_c = m_ref[hh, :, 0:1]
        for c in range(n_chunks):
            sl = slice(c * CHUNK, (c + 1) * CHUNK)
            log_keep = b_last_rows[c * CHUNK:c * CHUNK + 1, :] + m_c
            m_n = jnp.maximum(log_keep, jnp.max(log_w[sl], axis=0, keepdims=True))
            m_start.append(m_c)
            m_next.append(m_n)
            keep.append(jnp.exp(log_keep - m_n))
            m_c = m_n
        m_ref[hh] = jnp.broadcast_to(m_c, (1, LANES))
        log_intra = jnp.where(mask, b_col - row(bcum_t, f_idx) + row(gates_t, i_idx), -jnp.inf)
        log_inter = b_col + per_chunk(m_start)
        m_row = jnp.maximum(log_inter, jnp.max(log_intra, axis=1, keepdims=True))
        w_intra = jnp.exp(log_intra - m_row)
        w_inter = jnp.exp(log_inter - m_row)
        qb, kb, vb = q_ref[:, hs], k_ref[:, hs], v_ref[:, hs]
        q, k = qb.astype(F32), kb.astype(F32)
        s = lax.dot_general(qb, kb, (((1,), (1,)), ((), ())), preferred_element_type=F32) * w_intra
        num_intra = jnp.dot(s.astype(BF16), vb, preferred_element_type=F32)
        den_intra = jnp.sum(s, axis=1, keepdims=True)
        wk = jnp.exp(log_w - per_chunk(m_next)) * k
        wkb = wk.astype(BF16)
        cmat, nvec = c_ref[hh], n_ref[hh]
        num_inter, den_inter = [], []
        for c in range(n_chunks):
            sl = slice(c * CHUNK, (c + 1) * CHUNK)
            num_inter.append(jnp.dot(qb[sl], cmat.astype(BF16), preferred_element_type=F32))
            den_inter.append(jnp.sum(q[sl] * nvec, axis=1, keepdims=True))
            upd = lax.dot_general(wkb[sl], vb[sl], (((0,), (0,)), ((), ())), preferred_element_type=F32)
            cmat = keep[c] * cmat + upd
            nvec = keep[c] * nvec + jnp.sum(wk[sl], axis=0, keepdims=True)
        c_ref[hh] = cmat
        n_ref[hh] = nvec
        num = w_inter * jnp.concatenate(num_inter, axis=0) + num_intra
        den = w_inter * jnp.concatenate(den_inter, axis=0) + den_intra
        hc = num / jnp.maximum(jnp.abs(den), jnp.exp(-m_row))
        z = z_ref[:, hs]
        o_ref[:, hs] = ((_row_norm(hc, True) * gn_ref[hh] + skip_ref[:, hs] * xc_ref[:, hs])
                        * (z * jax.nn.sigmoid(z)))


def _mlstm_mixer(proj, conv_w, conv_b, w_q, w_k, w_v, w_gates, b_gates, gn_g, skip, bsz, seq):
    t = proj.shape[0]
    inner = proj.shape[1] // 2
    n_heads, dh = gn_g.shape
    tm = MLSTM_PRE_TILE
    n_g = w_gates.shape[1]
    wg = jnp.pad(w_gates, ((0, 0), (0, LANES - n_g))).astype(BF16)
    bg = jnp.pad(b_gates.reshape(1, n_g), ((0, 0), (0, LANES - n_g)))
    tile_spec = pl.BlockSpec((tm, inner), lambda i: (i, 0))
    bd_spec = pl.BlockSpec((inner // LANES, LANES, LANES), lambda i: (0, 0, 0))
    q, k, v, xc, gates = pl.pallas_call(
        functools.partial(_mlstm_pre_kernel, per=seq // tm, k_scale=dh ** -0.5),
        grid=(t // tm,),
        in_specs=[tile_spec,
                  pl.BlockSpec((MLSTM_CONV, inner), lambda i: (0, 0)),
                  pl.BlockSpec((1, inner), lambda i: (0, 0)),
                  bd_spec, bd_spec, bd_spec,
                  pl.BlockSpec((3 * inner, LANES), lambda i: (0, 0)),
                  pl.BlockSpec((1, LANES), lambda i: (0, 0))],
        out_specs=[tile_spec, tile_spec, tile_spec, tile_spec, pl.BlockSpec((tm, LANES), lambda i: (i, 0))],
        out_shape=[jax.ShapeDtypeStruct((t, inner), BF16)] * 4 + [jax.ShapeDtypeStruct((t, LANES), F32)],
        scratch_shapes=[pltpu.VMEM((tm + CONV_PAD, inner), F32)],
        compiler_params=_cparams("arbitrary"),
        name="mlstm_pre",
    )(proj, conv_w, conv_b.reshape(1, inner), _blockdiag_tiles(w_q), _blockdiag_tiles(w_k),
      _blockdiag_tiles(w_v), wg, bg)

    lb = SEQ_BLOCK
    hps = HEADS_PER_STEP
    n_hp = n_heads // hps
    r3 = lambda a: a.reshape(bsz, seq, a.shape[-1])
    head_spec = pl.BlockSpec((None, lb, hps * dh), lambda b, h, s: (b, s, h))
    out = pl.pallas_call(
        functools.partial(_mlstm_rec_kernel, n_heads=n_heads),
        grid=(bsz, n_hp, seq // lb),
        in_specs=[head_spec, head_spec, head_spec,
                  pl.BlockSpec((None, lb, LANES), lambda b, h, s: (b, s, 0)),
                  head_spec,
                  pl.BlockSpec((None, lb, hps * dh), lambda b, h, s: (b, s, n_hp + h)),
                  pl.BlockSpec((hps, 1, dh), lambda b, h, s: (h, 0, 0)),
                  pl.BlockSpec((1, hps * dh), lambda b, h, s: (0, h))],
        out_specs=head_spec,
        out_shape=jax.ShapeDtypeStruct((bsz, seq, inner), F32),
        scratch_shapes=[pltpu.VMEM((hps, dh, dh), F32), pltpu.VMEM((hps, 1, dh), F32),
                        pltpu.VMEM((hps, 1, LANES), F32)],
        compiler_params=_cparams("arbitrary", "arbitrary", "arbitrary"),
        name="mlstm_rec",
    )(r3(q), r3(k), r3(v), r3(gates), r3(xc), r3(proj), gn_g.reshape(n_heads, 1, dh), skip.reshape(1, inner))
    return out.reshape(t, inner)


def kernel(x, c, ada_w, ada_b, norm_mix_g, norm_ffn_g, norm_final_g, ret_w_in, ret_gn_g, ret_w_out, att_w_in, att_rel_bias, att_w_out, gla_w_in, gla_w_gate, gla_b_gate, gla_gn_g, gla_w_out, mlstm_w_in, mlstm_conv_w, mlstm_conv_b, mlstm_w_q, mlstm_w_k, mlstm_w_v, mlstm_w_gates, mlstm_b_gates, mlstm_gn_g, mlstm_skip, mlstm_w_out, router_w, router_b, moe_w1, moe_b1, moe_w2, moe_b2):
    bsz, seq, d = x.shape
    depth, n_exp, _, dh2 = moe_w1.shape
    xt = x.reshape(bsz * seq, d)
    mod = _ada_mod(c, ada_w, ada_b)
    w1_all = moe_w1.reshape(depth * n_exp, d, dh2)
    w2_all = moe_w2.reshape(depth * n_exp, dh2 // 2, d)
    b1_all = moe_b1.reshape(depth * n_exp, 1, dh2)
    b2_all = moe_b2.reshape(depth * n_exp, 1, d)
    for i in range(depth):
        kind, j = i % 4, i // 4
        sh_a, sc_a, g_a, sh_f, sc_f, g_f = jnp.split(mod[i], 6, axis=-1)
        if kind == 0:
            proj = _norm_mm(xt, norm_mix_g[i], sc_a, sh_a, ret_w_in[j].astype(BF16), seq)
            y = _retention_mixer(proj, ret_gn_g[j], bsz, seq)
            w_out = ret_w_out[j]
        elif kind == 1:
            proj = _norm_mm(xt, norm_mix_g[i], sc_a, sh_a, att_w_in[j].astype(BF16), seq)
            y = _attention_mixer(proj, att_rel_bias[j], bsz, seq)
            w_out = att_w_out[j]
        elif kind == 2:
            n_heads, dv = gla_gn_g[j].shape
            w_gla = _gla_proj_weight(gla_w_in[j], n_heads * dv // 2, n_heads * dv).astype(BF16)
            proj = _norm_mm(xt, norm_mix_g[i], sc_a, sh_a, w_gla, seq)
            y = _gla_mixer(proj, gla_w_gate[j], gla_b_gate[j], gla_gn_g[j], bsz, seq)
            w_out = gla_w_out[j]
        else:
            proj = _norm_mm(xt, norm_mix_g[i], sc_a, sh_a, mlstm_w_in[j].astype(BF16), seq)
            y = _mlstm_mixer(proj, mlstm_conv_w[j], mlstm_conv_b[j], mlstm_w_q[j], mlstm_w_k[j], mlstm_w_v[j],
                             mlstm_w_gates[j], mlstm_b_gates[j], mlstm_gn_g[j], mlstm_skip[j], bsz, seq)
            w_out = mlstm_w_out[j]
        xt = _mm_res(y, w_out.astype(BF16), xt, g_a, seq)
        xt = _moe_layer(xt, norm_ffn_g[i], sc_f, sh_f, g_f, router_w[i], router_b[i],
                        w1_all, b1_all, w2_all, b2_all, i, seq)
    return _final_norm(xt, norm_final_g).reshape(bsz, seq, d)
```

```python
import functools

import jax
import jax.numpy as jnp
from jax import lax
from jax.experimental import pallas as pl
from jax.experimental.pallas import tpu as pltpu

F32 = jnp.float32
BF16 = jnp.bfloat16
HI = lax.Precision.HIGHEST

CHUNK = 64
EPS = 1e-6
ROPE_BASE = 10000.0
ATT_LEFT_CHUNKS = 8
ATT_MAX_REL = 256
GLA_GATE_RANK = 16
GLA_GATE_NORM = 16.0
MLSTM_CONV = 4
N_EXPERTS = 32
TOP_K = 4
SWIGLU_LIMIT = 7.0
SWIGLU_ALPHA = 1.702

VMEM_LIMIT_BYTES = 56 * 1024 * 1024
LANES = 128
SUBLANES = 8
MOE_TILE = 512
COMBINE_TILE = 256
COPIES_PER_ITER = 32
SEQ_BLOCK = 256
MLSTM_PRE_TILE = 256
CONV_PAD = 8
ATT_QBLOCK = 2 * CHUNK
ATT_LEFT = ATT_LEFT_CHUNKS * CHUNK
ATT_WINDOW = ATT_LEFT + ATT_QBLOCK


def _cparams(*sem):
    return pltpu.CompilerParams(dimension_semantics=sem, vmem_limit_bytes=VMEM_LIMIT_BYTES)


def _row_norm(x, center):
    if center:
        x = x - jnp.mean(x, axis=-1, keepdims=True)
    return x * lax.rsqrt(jnp.mean(x * x, axis=-1, keepdims=True) + EPS)


def _load_row_tiles(ref, rows):
    return jnp.concatenate([ref[pl.ds(j, rows, stride=SUBLANES), :] for j in range(SUBLANES)], axis=1)


def _store_row_tiles(ref, val):
    rows = val.shape[0]
    for j in range(SUBLANES):
        ref[pl.ds(j, rows, stride=SUBLANES), :] = val[:, j * LANES:(j + 1) * LANES]


def _ada_kernel(c_ref, w_ref, b_ref, o_ref):
    c = c_ref[...]
    cond = c * jax.nn.sigmoid(c)
    o_ref[0] = jnp.dot(cond, w_ref[0], preferred_element_type=F32, precision=HI) + b_ref[0]


def _ada_mod(c, ada_w, ada_b):
    depth, d, n = ada_w.shape
    b = c.shape[0]
    tn = 1536
    return pl.pallas_call(
        _ada_kernel,
        grid=(depth, n // tn),
        in_specs=[pl.BlockSpec((b, d), lambda l, j: (0, 0)),
                  pl.BlockSpec((1, d, tn), lambda l, j: (l, 0, j)),
                  pl.BlockSpec((1, 1, tn), lambda l, j: (l, 0, j))],
        out_specs=pl.BlockSpec((1, b, tn), lambda l, j: (l, 0, j)),
        out_shape=jax.ShapeDtypeStruct((depth, b, n), F32),
        compiler_params=_cparams("arbitrary", "arbitrary"),
        name="ada_mod",
    )(c, ada_w, ada_b.reshape(depth, 1, n))


def _modulated_norm(x, g, sc, sh):
    ms = jnp.mean(x * x, axis=-1, keepdims=True)
    return (x * lax.rsqrt(ms + EPS) * g) * (1.0 + sc) + sh


def _norm_mm_kernel(x_ref, g_ref, sc_ref, sh_ref, w_ref, o_ref, h_scr):
    @pl.when(pl.program_id(1) == 0)
    def _():
        h = _modulated_norm(x_ref[...], g_ref[...], sc_ref[0], sh_ref[0])
        h_scr[...] = h.astype(BF16)

    o_ref[...] = jnp.dot(h_scr[...], w_ref[...], preferred_element_type=F32)


NORM_MM_MAX_COLS = 2048


def _col_tile(n):
    best = LANES
    for m in range(1, n // LANES + 1):
        if (n // LANES) % m == 0 and m * LANES <= NORM_MM_MAX_COLS:
            best = m * LANES
    return best


def _norm_mm(x, g, sc, sh, w, seq, tm=1024):
    t, d = x.shape
    n = w.shape[1]
    tn = _col_tile(n)
    per = seq // tm
    bsz = sc.shape[0]
    return pl.pallas_call(
        _norm_mm_kernel,
        grid=(t // tm, n // tn),
        in_specs=[pl.BlockSpec((tm, d), lambda i, j: (i, 0)),
                  pl.BlockSpec((1, d), lambda i, j: (0, 0)),
                  pl.BlockSpec((1, 1, d), lambda i, j: (i // per, 0, 0)),
                  pl.BlockSpec((1, 1, d), lambda i, j: (i // per, 0, 0)),
                  pl.BlockSpec((d, tn), lambda i, j: (0, j))],
        out_specs=pl.BlockSpec((tm, tn), lambda i, j: (i, j)),
        out_shape=jax.ShapeDtypeStruct((t, n), F32),
        scratch_shapes=[pltpu.VMEM((tm, d), BF16)],
        compiler_params=_cparams("arbitrary", "arbitrary"),
        name="norm_mm",
    )(x, g.reshape(1, d), sc.reshape(bsz, 1, d), sh.reshape(bsz, 1, d), w)


def _mm_res_kernel(y_ref, w_ref, x_ref, gate_ref, o_ref):
    acc = jnp.dot(y_ref[...].astype(BF16), w_ref[...], preferred_element_type=F32)
    o_ref[...] = x_ref[...] + gate_ref[0] * acc


def _mm_res(y, w, x, gate, seq, tm=512):
    t, k = y.shape
    d = w.shape[1]
    per = seq // tm
    bsz = gate.shape[0]
    return pl.pallas_call(
        _mm_res_kernel,
        grid=(t // tm,),
        in_specs=[pl.BlockSpec((tm, k), lambda i: (i, 0)),
                  pl.BlockSpec((k, d), lambda i: (0, 0)),
                  pl.BlockSpec((tm, d), lambda i: (i, 0)),
                  pl.BlockSpec((1, 1, d), lambda i: (i // per, 0, 0))],
        out_specs=pl.BlockSpec((tm, d), lambda i: (i, 0)),
        out_shape=jax.ShapeDtypeStruct((t, d), F32),
        compiler_params=_cparams("arbitrary"),
        name="mm_res",
    )(y, w, x, gate.reshape(bsz, 1, d))


def _norm_router_kernel(x_ref, g_ref, sc_ref, sh_ref, rw_ref, rb_ref, h_ref, route_ref, *, n_experts):
    h = _modulated_norm(x_ref[...], g_ref[...], sc_ref[0], sh_ref[0])
    _store_row_tiles(h_ref, h)
    logits = jnp.dot(h, rw_ref[...], preferred_element_type=F32, precision=HI) + rb_ref[...]
    lane = lax.broadcasted_iota(jnp.int32, logits.shape, 1)
    rem = jnp.where(lane < n_experts, logits, -jnp.inf)
    vals, idxs = [], []
    for _ in range(TOP_K):
        m = jnp.max(rem, axis=1, keepdims=True)
        idx = jnp.min(jnp.where(rem == m, lane, LANES), axis=1, keepdims=True)
        vals.append(m)
        idxs.append(idx)
        rem = jnp.where(lane == idx, -jnp.inf, rem)
    ex = [jnp.exp(v - vals[0]) for v in vals]
    inv = 1.0 / sum(ex[1:], ex[0])
    out = jnp.zeros(logits.shape, F32)
    for k in range(TOP_K):
        out = jnp.where(lane == k, ex[k] * inv, out)
        out = jnp.where(lane == TOP_K + k, idxs[k].astype(F32), out)
    route_ref[...] = out


def _norm_router(x, g, sc, sh, rw, rb, seq, tm=512):
    t, d = x.shape
    per = seq // tm
    bsz = sc.shape[0]
    e = rw.shape[1]
    rw_p = jnp.pad(rw, ((0, 0), (0, LANES - e)))
    rb_p = jnp.pad(rb.reshape(1, e), ((0, 0), (0, LANES - e)))
    return pl.pallas_call(
        functools.partial(_norm_router_kernel, n_experts=e),
        grid=(t // tm,),
        in_specs=[pl.BlockSpec((tm, d), lambda i: (i, 0)),
                  pl.BlockSpec((1, d), lambda i: (0, 0)),
                  pl.BlockSpec((1, 1, d), lambda i: (i // per, 0, 0)),
                  pl.BlockSpec((1, 1, d), lambda i: (i // per, 0, 0)),
                  pl.BlockSpec((d, LANES), lambda i: (0, 0)),
                  pl.BlockSpec((1, LANES), lambda i: (0, 0))],
        out_specs=[pl.BlockSpec((tm * SUBLANES, LANES), lambda i: (i, 0)),
                   pl.BlockSpec((tm, LANES), lambda i: (i, 0))],
        out_shape=[jax.ShapeDtypeStruct((t * SUBLANES, LANES), F32),
                   jax.ShapeDtypeStruct((t, LANES), F32)],
        compiler_params=_cparams("arbitrary"),
        name="norm_router",
    )(x, g.reshape(1, d), sc.reshape(bsz, 1, d), sh.reshape(bsz, 1, d), rw_p, rb_p)


def _row_tile_copy(src_hbm, src_row8, dst_vmem, dst_row, sem):
    return pltpu.make_async_copy(
        src_hbm.at[pl.ds(pl.multiple_of(src_row8, SUBLANES), SUBLANES)],
        dst_vmem.at[pl.ds(pl.multiple_of(dst_row * SUBLANES, SUBLANES), SUBLANES)], sem)


def _expert_kernel(te_ref, nt_ref, idx_cur_ref, idx_nxt_ref, h_hbm, w1_ref, b1_ref, w2_ref, b2_ref,
                   o_ref, xbuf, w1_bf, w2_bf, sem):
    i = pl.program_id(0)
    n_used = nt_ref[0]
    tm = xbuf.shape[1] // SUBLANES
    slot = i % 2

    @pl.when(jnp.logical_or(i == 0, te_ref[i] != te_ref[jnp.maximum(i - 1, 0)]))
    def _():
        w1_bf[...] = w1_ref[0].astype(BF16)
        w2_bf[...] = w2_ref[0].astype(BF16)

    def issue(idx_ref, s):
        def body(g, carry):
            for u in range(COPIES_PER_ITER):
                r = g * COPIES_PER_ITER + u
                _row_tile_copy(h_hbm, idx_ref[0, 0, r], xbuf.at[s], r, sem.at[s]).start(priority=u % 2)
            return carry
        lax.fori_loop(0, tm // COPIES_PER_ITER, body, 0)

    @pl.when(jnp.logical_and(i == 0, n_used > 0))
    def _():
        issue(idx_cur_ref, 0)

    @pl.when(i + 1 < n_used)
    def _():
        issue(idx_nxt_ref, 1 - slot)

    @pl.when(i < n_used)
    def _():
        pltpu.make_async_copy(h_hbm.at[pl.ds(0, tm * SUBLANES)], xbuf.at[slot], sem.at[slot]).wait()
        x = _load_row_tiles(xbuf.at[slot], tm).astype(BF16)
        hid = jnp.dot(x, w1_bf[...], preferred_element_type=F32) + b1_ref[0]
        de = hid.shape[1] // 2
        gate = jnp.minimum(hid[:, :de], SWIGLU_LIMIT)
        lin = jnp.clip(hid[:, de:], -SWIGLU_LIMIT, SWIGLU_LIMIT)
        act = gate * jax.nn.sigmoid(SWIGLU_ALPHA * gate) * (lin + 1.0)
        y = jnp.dot(act.astype(BF16), w2_bf[...], preferred_element_type=F32) + b2_ref[0]
        _store_row_tiles(o_ref, y)

    @pl.when(i >= n_used)
    def _():
        o_ref[...] = jnp.zeros_like(o_ref)


def _expert_ffn(h_tiles, src_row8, tile_expert, n_tiles_used, w1, b1, w2, b2, layer):
    p = src_row8.shape[0]
    tm = MOE_TILE
    nt = p // tm
    _, d, dh = w1.shape
    idx3 = src_row8.reshape(nt, 1, tm)
    base = layer * N_EXPERTS
    grid_spec = pltpu.PrefetchScalarGridSpec(
        num_scalar_prefetch=2,
        grid=(nt,),
        in_specs=[
            pl.BlockSpec((1, 1, tm), lambda i, te, nu: (i, 0, 0), memory_space=pltpu.SMEM),
            pl.BlockSpec((1, 1, tm), lambda i, te, nu: (jnp.minimum(i + 1, nt - 1), 0, 0),
                         memory_space=pltpu.SMEM),
            pl.BlockSpec(memory_space=pl.ANY),
            pl.BlockSpec((1, d, dh), lambda i, te, nu: (base + te[i], 0, 0)),
            pl.BlockSpec((1, 1, dh), lambda i, te, nu: (base + te[i], 0, 0)),
            pl.BlockSpec((1, dh // 2, d), lambda i, te, nu: (base + te[i], 0, 0)),
            pl.BlockSpec((1, 1, d), lambda i, te, nu: (base + te[i], 0, 0)),
        ],
        out_specs=pl.BlockSpec((tm * SUBLANES, LANES), lambda i, te, nu: (i, 0)),
        scratch_shapes=[pltpu.VMEM((2, tm * SUBLANES, LANES), F32), pltpu.VMEM((d, dh), BF16),
                        pltpu.VMEM((dh // 2, d), BF16), pltpu.SemaphoreType.DMA((2,))],
    )
    return pl.pallas_call(
        _expert_kernel,
        grid_spec=grid_spec,
        out_shape=jax.ShapeDtypeStruct((p * SUBLANES, LANES), F32),
        compiler_params=_cparams("arbitrary"),
        name="expert_ffn",
    )(tile_expert, n_tiles_used, idx3, idx3, h_tiles, w1, b1, w2, b2)


def _combine_kernel(idx_cur_ref, idx_nxt_ref, y_hbm, x_ref, w_ref, gate_ref, fg_ref, o_ref, ybuf, sem, *,
                    final_norm):
    i = pl.program_id(0)
    n = pl.num_programs(0)
    tc = x_ref.shape[0]
    slot = i % 2

    def issue(idx_ref, s):
        rows_per_iter = COPIES_PER_ITER // TOP_K

        def body(g, carry):
            for u in range(rows_per_iter):
                r = g * rows_per_iter + u
                for k in range(TOP_K):
                    _row_tile_copy(y_hbm, idx_ref[0, 0, r * TOP_K + k], ybuf.at[s, k], r,
                                   sem.at[s]).start(priority=k % 2)
            return carry
        lax.fori_loop(0, tc // rows_per_iter, body, 0)

    @pl.when(i == 0)
    def _():
        issue(idx_cur_ref, 0)

    @pl.when(i + 1 < n)
    def _():
        issue(idx_nxt_ref, 1 - slot)

    for k in range(TOP_K):
        pltpu.make_async_copy(y_hbm.at[pl.ds(0, tc * SUBLANES)], ybuf.at[slot, k], sem.at[slot]).wait()
    w = w_ref[...]
    acc = w[:, 0:1] * _load_row_tiles(ybuf.at[slot, 0], tc)
    for k in range(1, TOP_K):
        acc = acc + w[:, k:k + 1] * _load_row_tiles(ybuf.at[slot, k], tc)
    out = x_ref[...] + gate_ref[0] * acc
    if final_norm:
        out = out * lax.rsqrt(jnp.mean(out * out, axis=-1, keepdims=True) + EPS) * fg_ref[...]
    o_ref[...] = out


def _moe_combine(y_tiles, slot8_of_pair, top_w, x, gate, final_g, final_norm, seq):
    t, d = x.shape
    tc = COMBINE_TILE
    nt = t // tc
    per = seq // tc
    bsz = gate.shape[0]
    idx3 = slot8_of_pair.reshape(nt, 1, tc * TOP_K)
    return pl.pallas_call(
        functools.partial(_combine_kernel, final_norm=final_norm),
        grid=(nt,),
        in_specs=[
            pl.BlockSpec((1, 1, tc * TOP_K), lambda i: (i, 0, 0), memory_space=pltpu.SMEM),
            pl.BlockSpec((1, 1, tc * TOP_K), lambda i: (jnp.minimum(i + 1, nt - 1), 0, 0),
                         memory_space=pltpu.SMEM),
            pl.BlockSpec(memory_space=pl.ANY),
            pl.BlockSpec((tc, d), lambda i: (i, 0)),
            pl.BlockSpec((tc, TOP_K), lambda i: (i, 0)),
            pl.BlockSpec((1, 1, d), lambda i: (i // per, 0, 0)),
            pl.BlockSpec((1, d), lambda i: (0, 0)),
        ],
        out_specs=pl.BlockSpec((tc, d), lambda i: (i, 0)),
        out_shape=jax.ShapeDtypeStruct((t, d), F32),
        scratch_shapes=[pltpu.VMEM((2, TOP_K, tc * SUBLANES, LANES), F32), pltpu.SemaphoreType.DMA((2,))],
        compiler_params=_cparams("arbitrary"),
        name="moe_combine",
    )(idx3, idx3, y_tiles, x, top_w, gate.reshape(bsz, 1, d), final_g.reshape(1, d))


def _route(route):
    t = route.shape[0]
    top_w = route[:, :TOP_K]
    flat_e = route[:, TOP_K:2 * TOP_K].astype(jnp.int32).reshape(-1)
    onehot = (flat_e[:, None] == jnp.arange(N_EXPERTS, dtype=jnp.int32)[None, :]).astype(jnp.int32)
    csum = jnp.cumsum(onehot, axis=0)
    counts = csum[-1]
    padded = ((counts + MOE_TILE - 1) // MOE_TILE) * MOE_TILE
    ends = jnp.cumsum(padded)
    starts = ends - padded
    slot_of_pair = jnp.sum(onehot * (csum - 1 + starts[None, :]), axis=1)
    n_fill = N_EXPERTS * MOE_TILE
    fill_end = jnp.cumsum(padded - counts)
    fill_key = jnp.sum((jnp.arange(n_fill, dtype=jnp.int32)[:, None] >= fill_end[None, :]).astype(jnp.int32), axis=1)
    n_pairs = t * TOP_K
    pos_bits = (n_pairs + n_fill - 1).bit_length()
    assert (N_EXPERTS + 1) << pos_bits < 2 ** 31
    keys = (jnp.concatenate([flat_e, fill_key]) << pos_bits) + jnp.arange(n_pairs + n_fill, dtype=jnp.int32)
    entry = lax.sort(keys) & ((1 << pos_bits) - 1)
    src_tok = jnp.where(entry < n_pairs, entry // TOP_K, 0)
    n_tiles = (n_pairs + n_fill) // MOE_TILE
    tile_start = jnp.arange(n_tiles, dtype=jnp.int32) * MOE_TILE
    tile_expert = jnp.sum((tile_start[:, None] >= ends[None, :]).astype(jnp.int32), axis=1)
    n_used = (ends[-1] // MOE_TILE).astype(jnp.int32)
    last_e = jnp.sum(jnp.where(jnp.arange(n_tiles) == n_used - 1, tile_expert, 0))
    tile_expert = jnp.where(jnp.arange(n_tiles) < n_used, tile_expert, last_e).astype(jnp.int32)
    return (top_w, (slot_of_pair * SUBLANES).astype(jnp.int32), (src_tok * SUBLANES).astype(jnp.int32),
            tile_expert, n_used.reshape(1))


def _moe_layer(x, g, sc, sh, gate, rw, rb, w1, b1, w2, b2, layer, final_g, final_norm, seq):
    h_tiles, route = _norm_router(x, g, sc, sh, rw, rb, seq)
    top_w, slot8_of_pair, src_row8, tile_expert, n_used = _route(route)
    y_tiles = _expert_ffn(h_tiles, src_row8, tile_expert, n_used, w1, b1, w2, b2, layer)
    return _moe_combine(y_tiles, slot8_of_pair, top_w, x, gate, final_g, final_norm, seq)


def _chunk_pos(shape):
    return lax.broadcasted_iota(jnp.int32, shape, 0) & (CHUNK - 1)


def _inchunk_cumsum(x):
    pos = _chunk_pos(x.shape)
    sh = 1
    while sh < CHUNK:
        x = x + jnp.where(pos >= sh, pltpu.roll(x, sh, axis=0), 0.0)
        sh *= 2
    return x


def _chunk_last_rows(b):
    rows, d = b.shape
    b3 = b.reshape(rows // CHUNK, CHUNK, d)
    return jnp.broadcast_to(b3[:, CHUNK - 1:CHUNK, :], b3.shape).reshape(rows, d)


def _chunk_causal_mask(rows):
    r = lax.broadcasted_iota(jnp.int32, (rows, rows), 0)
    c = lax.broadcasted_iota(jnp.int32, (rows, rows), 1)
    return jnp.logical_and(r >= c, (r // CHUNK) == (c // CHUNK))


def _linattn_block(q, k, v, b, b_rem, chunk_decay, state_ref):
    rows = q.shape[0]
    q_in = (q * jnp.exp(b)).astype(BF16)
    k_in = (k * jnp.exp(-b)).astype(BF16)
    k_st = (k * jnp.exp(b_rem)).astype(BF16)
    vb = v.astype(BF16)
    scores = lax.dot_general(q_in, k_in, (((1,), (1,)), ((), ())), preferred_element_type=F32)
    scores = jnp.where(_chunk_causal_mask(rows), scores, 0.0)
    o_intra = jnp.dot(scores.astype(BF16), vb, preferred_element_type=F32)
    state = state_ref[...]
    o_inter = []
    for c in range(rows // CHUNK):
        sl = slice(c * CHUNK, (c + 1) * CHUNK)
        o_inter.append(jnp.dot(q_in[sl], state.astype(BF16), preferred_element_type=F32))
        upd = lax.dot_general(k_st[sl], vb[sl], (((0,), (0,)), ((), ())), preferred_element_type=F32)
        state = chunk_decay(c) * state + upd
    state_ref[...] = state
    return o_intra + jnp.concatenate(o_inter, axis=0)


HEADS_PER_STEP = 4


def _ret_kernel(q_ref, k_ref, v_ref, g_ref, cos_ref, sin_ref, lg_ref, gn_ref, o_ref, state_ref, *, k_scale):
    @pl.when(pl.program_id(2) == 0)
    def _():
        state_ref[...] = jnp.zeros_like(state_ref)

    rows = q_ref.shape[0]
    dk = q_ref.shape[1] // HEADS_PER_STEP
    dv = v_ref.shape[1] // HEADS_PER_STEP
    half = dk // 2
    cos, sin = cos_ref[...], sin_ref[...]

    def rope(t):
        t1, t2 = t[:, :half], t[:, half:]
        return jnp.concatenate([t1 * cos - t2 * sin, t1 * sin + t2 * cos], axis=1)

    pos = _chunk_pos((rows, dk)).astype(F32)
    for hh in range(HEADS_PER_STEP):
        ks, vs = slice(hh * dk, (hh + 1) * dk), slice(hh * dv, (hh + 1) * dv)
        lg = lg_ref[hh]
        b = (pos + 1.0) * lg
        b_rem = (CHUNK - 1.0 - pos) * lg
        decay = jnp.exp(CHUNK * lg[:, 0:1])
        q = rope(q_ref[:, ks])
        k = rope(k_ref[:, ks]) * k_scale
        o = _linattn_block(q, k, v_ref[:, vs], b, b_rem, lambda c: decay, state_ref.at[hh])
        g = g_ref[:, vs]
        o_ref[:, vs] = _row_norm(o, True) * gn_ref[hh] * (g * jax.nn.sigmoid(g))


def _retention_mixer(proj, gn_g, bsz, seq):
    n_heads, dv = gn_g.shape
    dk = dv // 2
    lb = SEQ_BLOCK
    hps = HEADS_PER_STEP
    n_hp = n_heads // hps
    p3 = proj.reshape(bsz, seq, proj.shape[-1])
    half = dk // 2
    inv = ROPE_BASE ** (-jnp.arange(half, dtype=F32) / half)
    ang = jnp.arange(seq, dtype=F32)[:, None] * inv[None, :]
    cos, sin = jnp.cos(ang), jnp.sin(ang)
    log_gamma = jnp.log1p(-jnp.exp2(-5.0 - jnp.arange(n_heads, dtype=F32)))
    lg = jnp.broadcast_to(log_gamma[:, None, None], (n_heads, 1, dk))
    out = pl.pallas_call(
        functools.partial(_ret_kernel, k_scale=dk ** -0.5),
        grid=(bsz, n_hp, seq // lb),
        in_specs=[pl.BlockSpec((None, lb, hps * dk), lambda b, h, s: (b, s, h)),
                  pl.BlockSpec((None, lb, hps * dk), lambda b, h, s: (b, s, n_hp + h)),
                  pl.BlockSpec((None, lb, hps * dv), lambda b, h, s: (b, s, n_hp + h)),
                  pl.BlockSpec((None, lb, hps * dv), lambda b, h, s: (b, s, 2 * n_hp + h)),
                  pl.BlockSpec((lb, half), lambda b, h, s: (s, 0)),
                  pl.BlockSpec((lb, half), lambda b, h, s: (s, 0)),
                  pl.BlockSpec((hps, 1, dk), lambda b, h, s: (h, 0, 0)),
                  pl.BlockSpec((hps, 1, dv), lambda b, h, s: (h, 0, 0))],
        out_specs=pl.BlockSpec((None, lb, hps * dv), lambda b, h, s: (b, s, h)),
        out_shape=jax.ShapeDtypeStruct((bsz, seq, n_heads * dv), F32),
        scratch_shapes=[pltpu.VMEM((hps, dk, dv), F32)],
        compiler_params=_cparams("arbitrary", "arbitrary", "arbitrary"),
        name="retention",
    )(p3, p3, p3, p3, cos, sin, lg, gn_g.reshape(n_heads, 1, dv))
    return out.reshape(bsz * seq, n_heads * dv)


def _gla_kernel(q_ref, k_ref, v_ref, r_ref, glow_ref, wg_ref, bg_ref, gn_ref, o_ref, state_ref, *, q_scale):
    @pl.when(pl.program_id(2) == 0)
    def _():
        state_ref[...] = jnp.zeros_like(state_ref)

    dk = q_ref.shape[1] // HEADS_PER_STEP
    dv = v_ref.shape[1] // HEADS_PER_STEP

    rows = q_ref.shape[0]
    z = jnp.dot(glow_ref[...], wg_ref[...], precision=HI, preferred_element_type=F32) + bg_ref[...]
    la_all = jax.nn.log_sigmoid(z) / GLA_GATE_NORM
    chunk_sel = (lax.broadcasted_iota(jnp.int32, (rows, LANES), 0) // CHUNK
                 == lax.broadcasted_iota(jnp.int32, (rows, LANES), 1)).astype(F32)
    for hh in range(HEADS_PER_STEP):
        ks, vs = slice(hh * dk, (hh + 1) * dk), slice(hh * dv, (hh + 1) * dv)
        la = la_all[:, ks]
        b = _inchunk_cumsum(la)
        b_rem = _chunk_last_rows(b) - b
        decay = jnp.exp(lax.dot_general(la, chunk_sel, (((0,), (0,)), ((), ())),
                                        precision=HI, preferred_element_type=F32))
        o = _linattn_block(q_ref[:, ks] * q_scale, k_ref[:, ks], v_ref[:, vs], b, b_rem,
                           lambda c: decay[:, c:c + 1], state_ref.at[hh])
        r = r_ref[:, vs]
        o_ref[:, vs] = _row_norm(o, False) * gn_ref[hh] * (r * jax.nn.sigmoid(r))


def _gla_proj_weight(w_in, dq, dvv):
    q_k_v = w_in[:, :2 * dq + dvv]
    g_low = w_in[:, 2 * dq + dvv:2 * dq + dvv + GLA_GATE_RANK]
    r = w_in[:, 2 * dq + dvv + GLA_GATE_RANK:]
    pad = jnp.zeros((w_in.shape[0], LANES - GLA_GATE_RANK), w_in.dtype)
    return jnp.concatenate([q_k_v, r, g_low, pad], axis=1)


def _gla_mixer(proj, w_gate, b_gate, gn_g, bsz, seq):
    n_heads, dv = gn_g.shape
    dk = dv // 2
    dq, dvv = n_heads * dk, n_heads * dv
    lb = SEQ_BLOCK
    hps = HEADS_PER_STEP
    n_hp = n_heads // hps
    p3 = proj.reshape(bsz, seq, proj.shape[-1])
    wg = jnp.pad(w_gate, ((0, LANES - GLA_GATE_RANK), (0, 0)))
    out = pl.pallas_call(
        functools.partial(_gla_kernel, q_scale=dk ** -0.5),
        grid=(bsz, n_hp, seq // lb),
        in_specs=[pl.BlockSpec((None, lb, hps * dk), lambda b, h, s: (b, s, h)),
                  pl.BlockSpec((None, lb, hps * dk), lambda b, h, s: (b, s, n_hp + h)),
                  pl.BlockSpec((None, lb, hps * dv), lambda b, h, s: (b, s, n_hp + h)),
                  pl.BlockSpec((None, lb, hps * dv), lambda b, h, s: (b, s, 2 * n_hp + h)),
                  pl.BlockSpec((None, lb, LANES), lambda b, h, s: (b, s, (2 * dq + 2 * dvv) // LANES)),
                  pl.BlockSpec((LANES, hps * dk), lambda b, h, s: (0, h)),
                  pl.BlockSpec((1, hps * dk), lambda b, h, s: (0, h)),
                  pl.BlockSpec((hps, 1, dv), lambda b, h, s: (h, 0, 0))],
        out_specs=pl.BlockSpec((None, lb, hps * dv), lambda b, h, s: (b, s, h)),
        out_shape=jax.ShapeDtypeStruct((bsz, seq, dvv), F32),
        scratch_shapes=[pltpu.VMEM((hps, dk, dv), F32)],
        compiler_params=_cparams("arbitrary", "arbitrary", "arbitrary"),
        name="gla",
    )(p3, p3, p3, p3, p3, wg, b_gate.reshape(1, dq), gn_g.reshape(n_heads, 1, dv))
    return out.reshape(bsz * seq, dvv)


def _att_kernel(q_ref, k_ref, v_ref, bias_ref, o_ref, kpad, vpad, *, scale, dh):
    seq, width = q_ref.shape
    n_pair = width // dh
    kpad[0:ATT_LEFT, :] = jnp.zeros((ATT_LEFT, width), F32)
    vpad[0:ATT_LEFT, :] = jnp.zeros((ATT_LEFT, width), F32)
    kpad[ATT_LEFT:ATT_LEFT + seq, :] = k_ref[...]
    vpad[ATT_LEFT:ATT_LEFT + seq, :] = v_ref[...]
    lane = lax.broadcasted_iota(jnp.int32, (ATT_QBLOCK, width), 1)
    jcol = lax.broadcasted_iota(jnp.int32, (n_pair * ATT_QBLOCK, ATT_WINDOW), 1)
    in_head = [jnp.logical_and(lane >= h * dh, lane < (h + 1) * dh) for h in range(n_pair)]

    def block(start, left_edge):
        q = q_ref[pl.ds(start, ATT_QBLOCK), :] * scale
        kb = kpad[pl.ds(start, ATT_WINDOW), :].astype(BF16)
        vb = vpad[pl.ds(start, ATT_WINDOW), :].astype(BF16)
        q2 = jnp.concatenate([jnp.where(m, q, 0.0) for m in in_head], axis=0).astype(BF16)
        s = lax.dot_general(q2, kb, (((1,), (1,)), ((), ())), preferred_element_type=F32) + bias_ref[...]
        if left_edge:
            s = jnp.where(jcol >= ATT_LEFT - start, s, -jnp.inf)
        p = jnp.exp(s - jnp.max(s, axis=-1, keepdims=True))
        o2 = jnp.dot(p.astype(BF16), vb, preferred_element_type=F32) / jnp.sum(p, axis=-1, keepdims=True)
        out = o2[0:ATT_QBLOCK, :]
        for h in range(1, n_pair):
            out = jnp.where(in_head[h], o2[h * ATT_QBLOCK:(h + 1) * ATT_QBLOCK, :], out)
        o_ref[pl.ds(start, ATT_QBLOCK), :] = out

    n_edge = ATT_LEFT // ATT_QBLOCK
    for c in range(n_edge):
        block(c * ATT_QBLOCK, True)

    def body(c, carry):
        block(pl.multiple_of(c * ATT_QBLOCK, ATT_QBLOCK), False)
        return carry

    lax.fori_loop(n_edge, seq // ATT_QBLOCK, body, 0, unroll=4)


def _attention_mixer(proj, rel_bias, bsz, seq):
    d = proj.shape[-1] // 3
    n_heads = rel_bias.shape[0]
    dh = d // n_heads
    n_pair = LANES // dh
    groups = n_heads // n_pair
    qo = jnp.arange(ATT_QBLOCK)[:, None]
    kj = jnp.arange(ATT_WINDOW)[None, :]
    lo = (qo // CHUNK) * CHUNK
    in_window = jnp.logical_and(kj >= lo, kj < lo + ATT_LEFT + CHUNK)
    d_max = ATT_QBLOCK - 1 + ATT_LEFT
    d_min = -(ATT_WINDOW - 1 - ATT_LEFT)
    n_rel = rel_bias.shape[1]
    ext = jnp.concatenate(
        [jnp.broadcast_to(rel_bias[:, :1], (n_heads, -(CHUNK - 1) - d_min)), rel_bias,
         jnp.broadcast_to(rel_bias[:, n_rel - 1:], (n_heads, d_max - ATT_MAX_REL))], axis=1)
    rows = [ext[:, q_ + ATT_LEFT - (ATT_WINDOW - 1) - d_min:q_ + ATT_LEFT - d_min + 1] for q_ in range(ATT_QBLOCK)]
    table = jnp.flip(jnp.stack(rows, axis=1), axis=2).astype(F32)
    bias = jnp.where(in_window[None], table, -jnp.inf)
    bias = bias.reshape(groups, n_pair * ATT_QBLOCK, ATT_WINDOW)
    p3 = proj.reshape(bsz, seq, 3 * d)
    out = pl.pallas_call(
        functools.partial(_att_kernel, scale=dh ** -0.5, dh=dh),
        grid=(bsz, groups),
        in_specs=[pl.BlockSpec((None, seq, LANES), lambda b, g: (b, 0, g)),
                  pl.BlockSpec((None, seq, LANES), lambda b, g: (b, 0, groups + g)),
                  pl.BlockSpec((None, seq, LANES), lambda b, g: (b, 0, 2 * groups + g)),
                  pl.BlockSpec((None, n_pair * ATT_QBLOCK, ATT_WINDOW), lambda b, g: (g, 0, 0))],
        out_specs=pl.BlockSpec((None, seq, LANES), lambda b, g: (b, 0, g)),
        out_shape=jax.ShapeDtypeStruct((bsz, seq, d), F32),
        scratch_shapes=[pltpu.VMEM((ATT_LEFT + seq, LANES), F32), pltpu.VMEM((ATT_LEFT + seq, LANES), F32)],
        compiler_params=_cparams("arbitrary", "arbitrary"),
        name="chunk_attention",
    )(p3, p3, p3, bias)
    return out.reshape(bsz * seq, d)


def _mlstm_pre_kernel(xm_ref, cw_ref, cb_ref, wq_ref, wk_ref, wv_ref, wg_ref, bg_ref,
                      q_ref, k_ref, v_ref, xc_ref, g_ref, xpad, *, per, k_scale):
    i = pl.program_id(0)
    tm, inner = xm_ref.shape

    @pl.when(i % per == 0)
    def _():
        xpad[0:CONV_PAD, :] = jnp.zeros((CONV_PAD, inner), F32)

    @pl.when(i % per != 0)
    def _():
        xpad[0:CONV_PAD, :] = xpad[tm:tm + CONV_PAD, :]

    xm = xm_ref[...]
    xpad[CONV_PAD:CONV_PAD + tm, :] = xm
    acc = jnp.broadcast_to(cb_ref[...], (tm, inner))
    for j in range(MLSTM_CONV):
        off = CONV_PAD - (MLSTM_CONV - 1) + j
        acc = acc + cw_ref[j:j + 1, :] * xpad[off:off + tm, :]
    xc = acc * jax.nn.sigmoid(acc)
    xc_ref[...] = xc.astype(BF16)

    def blockdiag(t, w_ref):
        tb = t.astype(BF16)
        return jnp.concatenate(
            [jnp.dot(tb[:, g * LANES:(g + 1) * LANES], w_ref[g], preferred_element_type=F32)
             for g in range(inner // LANES)], axis=1)

    q = blockdiag(xc, wq_ref)
    k = blockdiag(xc, wk_ref) * k_scale
    v = blockdiag(xm, wv_ref)
    qb, kb, vb = q.astype(BF16), k.astype(BF16), v.astype(BF16)
    q_ref[...] = qb
    k_ref[...] = kb
    v_ref[...] = vb
    g_ref[...] = (jnp.dot(qb, wg_ref[0:inner, :], preferred_element_type=F32)
                  + jnp.dot(kb, wg_ref[inner:2 * inner, :], preferred_element_type=F32)
                  + jnp.dot(vb, wg_ref[2 * inner:3 * inner, :], preferred_element_type=F32) + bg_ref[...])


def _blockdiag_tiles(w):
    nb, c, _ = w.shape
    per = LANES // c
    wt = w.reshape(nb // per, per, c, c)
    t = jnp.einsum('gpcd,pq->gpcqd', wt, jnp.eye(per, dtype=w.dtype))
    return t.reshape(nb // per, LANES, LANES).astype(BF16)


def _mlstm_rec_kernel(q_ref, k_ref, v_ref, g_ref, xc_ref, z_ref, gn_ref, skip_ref, o_ref,
                      c_ref, n_ref, m_ref, *, n_heads):
    head0 = pl.program_id(1) * HEADS_PER_STEP
    dh = q_ref.shape[1] // HEADS_PER_STEP

    @pl.when(pl.program_id(2) == 0)
    def _():
        c_ref[...] = jnp.zeros_like(c_ref)
        n_ref[...] = jnp.zeros_like(n_ref)
        m_ref[...] = jnp.zeros_like(m_ref)

    rows = q_ref.shape[0]
    n_chunks = rows // CHUNK
    mask = _chunk_causal_mask(rows)
    lane = lax.broadcasted_iota(jnp.int32, (rows, LANES), 1)
    sub = lax.broadcasted_iota(jnp.int32, (LANES, rows), 0)

    def col(t, idx):
        return jnp.sum(jnp.where(lane == idx, t, 0.0), axis=1, keepdims=True)

    def row(t_t, idx):
        return jnp.sum(jnp.where(sub == idx, t_t, 0.0), axis=0, keepdims=True)

    def per_chunk(vals):
        return jnp.concatenate([jnp.broadcast_to(x, (CHUNK, 1)) for x in vals], axis=0)

    gates = g_ref[...]
    bcum = _inchunk_cumsum(jax.nn.log_sigmoid(gates))
    gates_t, bcum_t = gates.T, bcum.T
    for hh in range(HEADS_PER_STEP):
        hs = slice(hh * dh, (hh + 1) * dh)
        i_idx, f_idx = head0 + hh, n_heads + head0 + hh
        i_col, b_col = col(gates, i_idx), col(bcum, f_idx)
        b_last_rows = _chunk_last_rows(b_col)
        log_w = b_last_rows - b_col + i_col
        m_start, m_next, keep = [], [], []
        m_c = m_ref[hh, :, 0:1]
        for c in range(n_chunks):
            sl = slice(c * CHUNK, (c + 1) * CHUNK)
            log_keep = b_last_rows[c * CHUNK:c * CHUNK + 1, :] + m_c
            m_n = jnp.maximum(log_keep, jnp.max(log_w[sl], axis=0, keepdims=True))
            m_start.append(m_c)
            m_next.append(m_n)
            keep.append(jnp.exp(log_keep - m_n))
            m_c = m_n
        m_ref[hh] = jnp.broadcast_to(m_c, (1, LANES))
        log_intra = jnp.where(mask, b_col - row(bcum_t, f_idx) + row(gates_t, i_idx), -jnp.inf)
        log_inter = b_col + per_chunk(m_start)
        m_row = jnp.maximum(log_inter, jnp.max(log_intra, axis=1, keepdims=True))
        w_intra = jnp.exp(log_intra - m_row)
        w_inter = jnp.exp(log_inter - m_row)
        qb, kb, vb = q_ref[:, hs], k_ref[:, hs], v_ref[:, hs]
        q, k = qb.astype(F32), kb.astype(F32)
        s = lax.dot_general(qb, kb, (((1,), (1,)), ((), ())), preferred_element_type=F32) * w_intra
        num_intra = jnp.dot(s.astype(BF16), vb, preferred_element_type=F32)
        den_intra = jnp.sum(s, axis=1, keepdims=True)
        wk = jnp.exp(log_w - per_chunk(m_next)) * k
        wkb = wk.astype(BF16)
        cmat, nvec = c_ref[hh], n_ref[hh]
        num_inter, den_inter = [], []
        for c in range(n_chunks):
            sl = slice(c * CHUNK, (c + 1) * CHUNK)
            num_inter.append(jnp.dot(qb[sl], cmat.astype(BF16), preferred_element_type=F32))
            den_inter.append(jnp.sum(q[sl] * nvec, axis=1, keepdims=True))
            upd = lax.dot_general(wkb[sl], vb[sl], (((0,), (0,)), ((), ())), preferred_element_type=F32)
            cmat = keep[c] * cmat + upd
            nvec = keep[c] * nvec + jnp.sum(wk[sl], axis=0, keepdims=True)
        c_ref[hh] = cmat
        n_ref[hh] = nvec
        num = w_inter * jnp.concatenate(num_inter, axis=0) + num_intra
        den = w_inter * jnp.concatenate(den_inter, axis=0) + den_intra
        hc = num / jnp.maximum(jnp.abs(den), jnp.exp(-m_row))
        z = z_ref[:, hs]
        o_ref[:, hs] = ((_row_norm(hc, True) * gn_ref[hh] + skip_ref[:, hs] * xc_ref[:, hs])
                        * (z * jax.nn.sigmoid(z)))


def _mlstm_mixer(proj, conv_w, conv_b, w_q, w_k, w_v, w_gates, b_gates, gn_g, skip, bsz, seq):
    t = proj.shape[0]
    inner = proj.shape[1] // 2
    n_heads, dh = gn_g.shape
    tm = MLSTM_PRE_TILE
    n_g = w_gates.shape[1]
    wg = jnp.pad(w_gates, ((0, 0), (0, LANES - n_g))).astype(BF16)
    bg = jnp.pad(b_gates.reshape(1, n_g), ((0, 0), (0, LANES - n_g)))
    tile_spec = pl.BlockSpec((tm, inner), lambda i: (i, 0))
    bd_spec = pl.BlockSpec((inner // LANES, LANES, LANES), lambda i: (0, 0, 0))
    q, k, v, xc, gates = pl.pallas_call(
        functools.partial(_mlstm_pre_kernel, per=seq // tm, k_scale=dh ** -0.5),
        grid=(t // tm,),
        in_specs=[tile_spec,
                  pl.BlockSpec((MLSTM_CONV, inner), lambda i: (0, 0)),
                  pl.BlockSpec((1, inner), lambda i: (0, 0)),
                  bd_spec, bd_spec, bd_spec,
                  pl.BlockSpec((3 * inner, LANES), lambda i: (0, 0)),
                  pl.BlockSpec((1, LANES), lambda i: (0, 0))],
        out_specs=[tile_spec, tile_spec, tile_spec, tile_spec, pl.BlockSpec((tm, LANES), lambda i: (i, 0))],
        out_shape=[jax.ShapeDtypeStruct((t, inner), BF16)] * 4 + [jax.ShapeDtypeStruct((t, LANES), F32)],
        scratch_shapes=[pltpu.VMEM((tm + CONV_PAD, inner), F32)],
        compiler_params=_cparams("arbitrary"),
        name="mlstm_pre",
    )(proj, conv_w, conv_b.reshape(1, inner), _blockdiag_tiles(w_q), _blockdiag_tiles(w_k),
      _blockdiag_tiles(w_v), wg, bg)

    lb = SEQ_BLOCK
    hps = HEADS_PER_STEP
    n_hp = n_heads // hps
    r3 = lambda a: a.reshape(bsz, seq, a.shape[-1])
    head_spec = pl.BlockSpec((None, lb, hps * dh), lambda b, h, s: (b, s, h))
    out = pl.pallas_call(
        functools.partial(_mlstm_rec_kernel, n_heads=n_heads),
        grid=(bsz, n_hp, seq // lb),
        in_specs=[head_spec, head_spec, head_spec,
                  pl.BlockSpec((None, lb, LANES), lambda b, h, s: (b, s, 0)),
                  head_spec,
                  pl.BlockSpec((None, lb, hps * dh), lambda b, h, s: (b, s, n_hp + h)),
                  pl.BlockSpec((hps, 1, dh), lambda b, h, s: (h, 0, 0)),
                  pl.BlockSpec((1, hps * dh), lambda b, h, s: (0, h))],
        out_specs=head_spec,
        out_shape=jax.ShapeDtypeStruct((bsz, seq, inner), F32),
        scratch_shapes=[pltpu.VMEM((hps, dh, dh), F32), pltpu.VMEM((hps, 1, dh), F32),
                        pltpu.VMEM((hps, 1, LANES), F32)],
        compiler_params=_cparams("arbitrary", "arbitrary", "arbitrary"),
        name="mlstm_rec",
    )(r3(q), r3(k), r3(v), r3(gates), r3(xc), r3(proj), gn_g.reshape(n_heads, 1, dh), skip.reshape(1, inner))
    return out.reshape(t, inner)


def kernel(x, c, ada_w, ada_b, norm_mix_g, norm_ffn_g, norm_final_g, ret_w_in, ret_gn_g, ret_w_out, att_w_in, att_rel_bias, att_w_out, gla_w_in, gla_w_gate, gla_b_gate, gla_gn_g, gla_w_out, mlstm_w_in, mlstm_conv_w, mlstm_conv_b, mlstm_w_q, mlstm_w_k, mlstm_w_v, mlstm_w_gates, mlstm_b_gates, mlstm_gn_g, mlstm_skip, mlstm_w_out, router_w, router_b, moe_w1, moe_b1, moe_w2, moe_b2):
    bsz, seq, d = x.shape
    depth, n_exp, _, dh2 = moe_w1.shape
    xt = x.reshape(bsz * seq, d)
    mod = _ada_mod(c, ada_w, ada_b)
    w1_all = moe_w1.reshape(depth * n_exp, d, dh2)
    w2_all = moe_w2.reshape(depth * n_exp, dh2 // 2, d)
    b1_all = moe_b1.reshape(depth * n_exp, 1, dh2)
    b2_all = moe_b2.reshape(depth * n_exp, 1, d)
    for i in range(depth):
        kind, j = i % 4, i // 4
        sh_a, sc_a, g_a, sh_f, sc_f, g_f = jnp.split(mod[i], 6, axis=-1)
        if kind == 0:
            proj = _norm_mm(xt, norm_mix_g[i], sc_a, sh_a, ret_w_in[j].astype(BF16), seq)
            y = _retention_mixer(proj, ret_gn_g[j], bsz, seq)
            w_out = ret_w_out[j]
        elif kind == 1:
            proj = _norm_mm(xt, norm_mix_g[i], sc_a, sh_a, att_w_in[j].astype(BF16), seq)
            y = _attention_mixer(proj, att_rel_bias[j], bsz, seq)
            w_out = att_w_out[j]
        elif kind == 2:
            n_heads, dv = gla_gn_g[j].shape
            w_gla = _gla_proj_weight(gla_w_in[j], n_heads * dv // 2, n_heads * dv).astype(BF16)
            proj = _norm_mm(xt, norm_mix_g[i], sc_a, sh_a, w_gla, seq)
            y = _gla_mixer(proj, gla_w_gate[j], gla_b_gate[j], gla_gn_g[j], bsz, seq)
            w_out = gla_w_out[j]
        else:
            proj = _norm_mm(xt, norm_mix_g[i], sc_a, sh_a, mlstm_w_in[j].astype(BF16), seq)
            y = _mlstm_mixer(proj, mlstm_conv_w[j], mlstm_conv_b[j], mlstm_w_q[j], mlstm_w_k[j], mlstm_w_v[j],
                             mlstm_w_gates[j], mlstm_b_gates[j], mlstm_gn_g[j], mlstm_skip[j], bsz, seq)
            w_out = mlstm_w_out[j]
        xt = _mm_res(y, w_out.astype(BF16), xt, g_a, seq)
        xt = _moe_layer(xt, norm_ffn_g[i], sc_f, sh_f, g_f, router_w[i], router_b[i],
                        w1_all, b1_all, w2_all, b2_all, i, norm_final_g, i == depth - 1, seq)
    return xt.reshape(bsz, seq, d)
```
